```python
import math
import jax, jax.numpy as jnp
from jax import lax
import numpy as np

D_MODEL = 1024
BATCH = 8
SEQ = 2048
DEPTH = 1

SSM_EXPAND = 2
SSM_D_INNER = SSM_EXPAND * D_MODEL
SSM_HEAD_DIM = 64
SSM_N_HEADS = SSM_D_INNER // SSM_HEAD_DIM
SSM_N_GROUPS = 4
SSM_HEADS_PER_GROUP = SSM_N_HEADS // SSM_N_GROUPS
SSM_D_STATE = 128
SSM_CONV = 4
SSM_CHUNK = 128
SSM_CONV_DIM = SSM_D_INNER + 2 * SSM_N_GROUPS * SSM_D_STATE

ATT_HEAD_DIM = 64
ATT_N_HEADS = D_MODEL // (2 * ATT_HEAD_DIM)
ATT_V_DIM = 2 * ATT_HEAD_DIM
ATT_Q_BLOCK = 128
ROPE_THETA = 500000.0
ROPE_DIM = ATT_HEAD_DIM // 4

FFN_HIDDEN = ((8 * D_MODEL + 3 * 256 - 1) // (3 * 256)) * 256

RMS_EPS = 1e-6

Z_END = SSM_D_INNER
XBC_END = Z_END + SSM_CONV_DIM
DT_END = XBC_END + SSM_N_HEADS
Q_END = DT_END + 2 * ATT_N_HEADS * ATT_HEAD_DIM
K_END = Q_END + 2 * ATT_N_HEADS * ATT_HEAD_DIM
V_END = K_END + ATT_N_HEADS * ATT_V_DIM
IN_COLS = V_END + 2 * D_MODEL

kernel_name = "hybrid_ssd_diffattn_gated_block"


def rmsnorm(x, w):
    xf = x.astype(jnp.float32)
    y = xf * lax.rsqrt(jnp.mean(xf * xf, axis=-1, keepdims=True) + RMS_EPS)
    return (y * w.astype(jnp.float32)).astype(x.dtype)


def rope_partial(x, positions):
    half = ROPE_DIM // 2
    inv_freq = ROPE_THETA ** (-jnp.arange(0, ROPE_DIM, 2, dtype=jnp.float32) / ROPE_DIM)
    ang = positions.astype(jnp.float32)[..., None] * inv_freq
    cos = jnp.cos(ang)[:, :, None, :]
    sin = jnp.sin(ang)[:, :, None, :]
    x1 = x[..., :half].astype(jnp.float32)
    x2 = x[..., half:ROPE_DIM].astype(jnp.float32)
    rot = jnp.concatenate([x1 * cos - x2 * sin, x2 * cos + x1 * sin], axis=-1)
    return jnp.concatenate([rot.astype(x.dtype), x[..., ROPE_DIM:]], axis=-1)


def causal_depthwise_conv(u, w, b):
    out = lax.conv_general_dilated(
        u, w[:, None, :].astype(u.dtype), window_strides=(1,), padding=[(SSM_CONV - 1, 0)],
        dimension_numbers=("NWC", "WIO", "NWC"), feature_group_count=u.shape[-1])
    return out + b


def ssd_chunked(xh, dt, A, Bm, Cm):
    b, S = xh.shape[0], xh.shape[1]
    nc = S // SSM_CHUNK
    G, R, P, N = SSM_N_GROUPS, SSM_HEADS_PER_GROUP, SSM_HEAD_DIM, SSM_D_STATE
    xdt = (xh * dt[..., None]).reshape(b, nc, SSM_CHUNK, G, R, P)
    dA = (dt * A).reshape(b, nc, SSM_CHUNK, G, R)
    Bc = Bm.reshape(b, nc, SSM_CHUNK, G, N)
    Cc = Cm.reshape(b, nc, SSM_CHUNK, G, N)
    a_cs = jnp.cumsum(dA, axis=2)
    causal = jnp.tril(jnp.ones((SSM_CHUNK, SSM_CHUNK), dtype=bool))
    seg = a_cs[:, :, :, None] - a_cs[:, :, None, :]
    decay = jnp.exp(jnp.where(causal[None, None, :, :, None, None], seg, -jnp.inf))
    cb = jnp.einsum("bclgn,bcsgn->bclsg", Cc, Bc)
    y_diag = jnp.einsum("bclsg,bclsgr,bcsgrp->bclgrp", cb, decay, xdt)
    decay_to_end = jnp.exp(a_cs[:, :, -1:] - a_cs)
    chunk_states = jnp.einsum("bclgn,bclgr,bclgrp->bcgrpn", Bc, decay_to_end, xdt)
    chunk_decay = jnp.exp(a_cs[:, :, -1])

    def step(h, inp):
        st, dec = inp
        return h * dec[..., None, None] + st, h

    h0 = jnp.zeros((b, G, R, P, N), dtype=chunk_states.dtype)
    _, h_in = lax.scan(step, h0, (jnp.moveaxis(chunk_states, 1, 0), jnp.moveaxis(chunk_decay, 1, 0)))
    h_in = jnp.moveaxis(h_in, 0, 1)
    y_off = jnp.einsum("bclgn,bcgrpn,bclgr->bclgrp", Cc, h_in, jnp.exp(a_cs))
    return (y_diag + y_off).reshape(b, S, SSM_N_HEADS, P)


def mamba2_mixer(z, xbc, dt_raw, conv_w, conv_b, dt_bias, a_log, d_skip, norm_w):
    b, S, _ = z.shape
    xbc = jax.nn.silu(causal_depthwise_conv(xbc, conv_w, conv_b))
    xs, Bm, Cm = jnp.split(xbc, [SSM_D_INNER, SSM_D_INNER + SSM_N_GROUPS * SSM_D_STATE], axis=-1)
    xh = xs.reshape(b, S, SSM_N_HEADS, SSM_HEAD_DIM)
    Bm = Bm.reshape(b, S, SSM_N_GROUPS, SSM_D_STATE)
    Cm = Cm.reshape(b, S, SSM_N_GROUPS, SSM_D_STATE)
    dt = jax.nn.softplus((dt_raw + dt_bias).astype(jnp.float32))
    A = -jnp.exp(a_log.astype(jnp.float32))
    y = ssd_chunked(xh, dt, A, Bm, Cm) + xh * d_skip[:, None]
    y = y.reshape(b, S, SSM_D_INNER) * jax.nn.silu(z)
    g = y.reshape(b, S, SSM_N_GROUPS, SSM_D_INNER // SSM_N_GROUPS).astype(jnp.float32)
    g = g * lax.rsqrt(jnp.mean(g * g, axis=-1, keepdims=True) + RMS_EPS)
    return (g.reshape(b, S, SSM_D_INNER) * norm_w.astype(jnp.float32)).astype(z.dtype)


def diff_attention(q, k, v, positions, lam_q1, lam_k1, lam_q2, lam_k2, subln_w, lam_init):
    b, S, _ = q.shape
    H, d = ATT_N_HEADS, ATT_HEAD_DIM
    q = rope_partial(q.reshape(b, S, 2 * H, d), positions).reshape(b, S, H, 2, d)
    k = rope_partial(k.reshape(b, S, 2 * H, d), positions).reshape(b, S, H, 2, d)
    v = v.reshape(b, S, H, ATT_V_DIM)
    lam = (jnp.exp(jnp.sum(lam_q1.astype(jnp.float32) * lam_k1.astype(jnp.float32)))
           - jnp.exp(jnp.sum(lam_q2.astype(jnp.float32) * lam_k2.astype(jnp.float32))) + lam_init)
    nb = S // ATT_Q_BLOCK
    qb = (q * (d ** -0.5)).reshape(b, nb, ATT_Q_BLOCK, H, 2, d).transpose(1, 0, 2, 3, 4, 5)
    key_pos = jnp.arange(S)

    def block(args):
        qi, i = args
        s = jnp.einsum("bqhmd,bkhmd->bhmqk", qi, k).astype(jnp.float32)
        q_pos = i * ATT_Q_BLOCK + jnp.arange(ATT_Q_BLOCK)
        mask = key_pos[None, :] <= q_pos[:, None]
        p = jax.nn.softmax(jnp.where(mask, s, -jnp.inf), axis=-1)
        w = p[:, :, 0] - lam * p[:, :, 1]
        return jnp.einsum("bhqk,bkhe->bqhe", w.astype(v.dtype), v)

    o = lax.map(block, (qb, jnp.arange(nb)))
    o = o.transpose(1, 0, 2, 3, 4).reshape(b, S, H, ATT_V_DIM)
    o = rmsnorm(o, subln_w) * (1.0 - lam_init)
    return o.reshape(b, S, H * ATT_V_DIM)


def setup_inputs(seed: int = 0) -> dict:
    key = jax.random.key(seed)
    ks = jax.random.split(key, 24)
    f32 = jnp.float32

    def nrm(k, shape, scale):
        return jax.random.normal(k, shape, f32) * scale

    def gain(k, n):
        return 1.0 + 0.02 * jax.random.normal(k, (DEPTH, n), f32)

    x = jax.random.normal(ks[0], (BATCH, SEQ, D_MODEL), f32)
    offsets = jax.random.randint(ks[1], (BATCH, 1), 0, 4096, dtype=jnp.int32)
    positions = jnp.arange(SEQ, dtype=jnp.int32)[None, :] + offsets
    u = jax.random.uniform(ks[5], (DEPTH, SSM_N_HEADS), f32)
    dt0 = jnp.exp(u * (math.log(0.1) - math.log(0.001)) + math.log(0.001))
    dt_bias = dt0 + jnp.log(-jnp.expm1(-dt0))
    a_log = jnp.log(jax.random.uniform(ks[6], (DEPTH, SSM_N_HEADS), f32, 1.0, 16.0))
    return {
        "x": x,
        "positions": positions,
        "w_in": nrm(ks[2], (DEPTH, D_MODEL, IN_COLS), D_MODEL ** -0.5),
        "conv_w": nrm(ks[3], (DEPTH, SSM_CONV, SSM_CONV_DIM), SSM_CONV ** -0.5),
        "conv_b": nrm(ks[4], (DEPTH, SSM_CONV_DIM), 0.02),
        "dt_bias": dt_bias,
        "a_log": a_log,
        "d_skip": 1.0 + 0.02 * jax.random.normal(ks[7], (DEPTH, SSM_N_HEADS), f32),
        "ssm_norm_w": gain(ks[8], SSM_D_INNER),
        "w_ssm_out": nrm(ks[9], (DEPTH, SSM_D_INNER, D_MODEL), SSM_D_INNER ** -0.5),
        "lam_q1": nrm(ks[10], (DEPTH, ATT_HEAD_DIM), 0.1),
        "lam_k1": nrm(ks[11], (DEPTH, ATT_HEAD_DIM), 0.1),
        "lam_q2": nrm(ks[12], (DEPTH, ATT_HEAD_DIM), 0.1),
        "lam_k2": nrm(ks[13], (DEPTH, ATT_HEAD_DIM), 0.1),
        "attn_subln_w": gain(ks[14], ATT_V_DIM),
        "w_attn_out": nrm(ks[15], (DEPTH, ATT_N_HEADS * ATT_V_DIM, D_MODEL), (ATT_N_HEADS * ATT_V_DIM) ** -0.5),
        "w_mix_out": nrm(ks[16], (DEPTH, D_MODEL, D_MODEL), D_MODEL ** -0.5),
        "norm_pre_mix": gain(ks[17], D_MODEL),
        "norm_post_mix": gain(ks[18], D_MODEL),
        "norm_pre_ffn": gain(ks[19], D_MODEL),
        "norm_post_ffn": gain(ks[20], D_MODEL),
        "w_ffn_gate": nrm(ks[21], (DEPTH, D_MODEL, FFN_HIDDEN), D_MODEL ** -0.5),
        "w_ffn_up": nrm(ks[22], (DEPTH, D_MODEL, FFN_HIDDEN), D_MODEL ** -0.5),
        "w_ffn_down": nrm(ks[23], (DEPTH, FFN_HIDDEN, D_MODEL), FFN_HIDDEN ** -0.5),
    }


def reference(x, positions, w_in, conv_w, conv_b, dt_bias, a_log, d_skip, ssm_norm_w, w_ssm_out,
              lam_q1, lam_k1, lam_q2, lam_k2, attn_subln_w, w_attn_out, w_mix_out,
              norm_pre_mix, norm_post_mix, norm_pre_ffn, norm_post_ffn,
              w_ffn_gate, w_ffn_up, w_ffn_down):
    for l in range(DEPTH):
        lam_init = 0.8 - 0.6 * math.exp(-0.3 * l)
        h = rmsnorm(x, norm_pre_mix[l])
        proj = h @ w_in[l]
        z, xbc, dt_raw, q, k, v, gate_logits = jnp.split(
            proj, [Z_END, XBC_END, DT_END, Q_END, K_END, V_END], axis=-1)
        y_ssm = mamba2_mixer(z, xbc, dt_raw, conv_w[l], conv_b[l], dt_bias[l], a_log[l],
                             d_skip[l], ssm_norm_w[l]) @ w_ssm_out[l]
        y_att = diff_attention(q, k, v, positions, lam_q1[l], lam_k1[l], lam_q2[l], lam_k2[l],
                               attn_subln_w[l], lam_init) @ w_attn_out[l]
        gates = jax.nn.sigmoid(gate_logits.astype(jnp.float32)).astype(x.dtype)
        g_ssm, g_att = jnp.split(gates, 2, axis=-1)
        mixed = (g_ssm * y_ssm + g_att * y_att) @ w_mix_out[l]
        x = x + rmsnorm(mixed, norm_post_mix[l])
        h = rmsnorm(x, norm_pre_ffn[l])
        f = (jax.nn.silu(h @ w_ffn_gate[l]) * (h @ w_ffn_up[l])) @ w_ffn_down[l]
        x = x + rmsnorm(f, norm_post_ffn[l])
    return x
```

```python
import functools
import math

import jax
import jax.numpy as jnp
from jax import lax
from jax.experimental import pallas as pl
from jax.experimental.pallas import tpu as pltpu

F32 = jnp.float32
BF16 = jnp.bfloat16

LANES = 128

SSM_HEAD_DIM = 64
SSM_N_GROUPS = 4
SSM_D_STATE = 128
SSM_CONV = 4
SSM_CHUNK = 128
ATT_HEAD_DIM = 64
ATT_V_DIM = 2 * ATT_HEAD_DIM
ROPE_THETA = 500000.0
ROPE_DIM = ATT_HEAD_DIM // 4
RMS_EPS = 1e-6

MIB = 1024 * 1024


def _rms(x, w):
    return x * lax.rsqrt(jnp.mean(x * x, axis=-1, keepdims=True) + RMS_EPS) * w


def _silu(x):
    return x * jax.nn.sigmoid(x)


def _dot(a, b):
    return jnp.dot(a, b, preferred_element_type=F32)


def _dot_nt(a, b):
    return lax.dot_general(a, b, (((1,), (1,)), ((), ())), preferred_element_type=F32)


def _dot_exact(a, b):
    return jnp.dot(a, b, preferred_element_type=F32, precision=lax.Precision.HIGHEST)


def _const_spec(shape):
    nd = len(shape)
    return pl.BlockSpec(shape, lambda *_: (0,) * nd, pipeline_mode=pl.Buffered(1))


def _inproj_kernel(x_ref, pos_ref, nw_ref, w_ref, wdt_ref, dtb_ref, invf_ref,
                   out_ref, dt_ref, h_scr, cos_scr, s1_scr, s2_scr, *, j_q, j_k):
    j = pl.program_id(1)

    @pl.when(j == 0)
    def _():
        hb = _rms(x_ref[...], nw_ref[...]).astype(BF16)
        h_scr[...] = hb
        dt_ref[...] = jax.nn.softplus(_dot(hb, wdt_ref[...]) + dtb_ref[...])
        ang = pos_ref[...].astype(F32) * invf_ref[...]
        lane = lax.broadcasted_iota(jnp.int32, ang.shape, 1) % ATT_HEAD_DIM
        sn = jnp.sin(ang)
        half = ROPE_DIM // 2
        cos_scr[...] = jnp.cos(ang)
        s1_scr[...] = jnp.where((lane >= half) & (lane < ROPE_DIM), sn, 0.0)
        s2_scr[...] = jnp.where(lane < half, -sn, 0.0)

    acc = _dot(h_scr[...], w_ref[...])
    is_rope = (j == j_q) | (j == j_k)

    @pl.when(is_rope)
    def _():
        scale = jnp.where(j == j_q, ATT_HEAD_DIM ** -0.5, 1.0).astype(F32)
        half = ROPE_DIM // 2
        for c in range(acc.shape[1] // LANES):
            a = acc[:, c * LANES:(c + 1) * LANES]
            r = (a * cos_scr[...] + pltpu.roll(a, half, 1) * s1_scr[...]
                 + pltpu.roll(a, LANES - half, 1) * s2_scr[...])
            out_ref[:, c * LANES:(c + 1) * LANES] = (r * scale).astype(out_ref.dtype)

    @pl.when(jnp.logical_not(is_rope))
    def _():
        out_ref[...] = acc.astype(out_ref.dtype)


def _in_proj(x2d, pos2d, norm_w, w_cat, w_dt, dt_bias, inv_freq_lane, *, j_q, j_k, tm=1024, tn=1024):
    t, d = x2d.shape
    n = w_cat.shape[1]
    grid = (t // tm, n // tn)
    return pl.pallas_call(
        functools.partial(_inproj_kernel, j_q=j_q, j_k=j_k),
        grid=grid,
        in_specs=[
            pl.BlockSpec((tm, d), lambda i, j: (i, 0)),
            pl.BlockSpec((tm, 1), lambda i, j: (i, 0)),
            pl.BlockSpec((1, d), lambda i, j: (0, 0)),
            pl.BlockSpec((d, tn), lambda i, j: (0, j)),
            pl.BlockSpec((d, LANES), lambda i, j: (0, 0)),
            pl.BlockSpec((1, LANES), lambda i, j: (0, 0)),
            pl.BlockSpec((1, LANES), lambda i, j: (0, 0)),
        ],
        out_specs=[
            pl.BlockSpec((tm, tn), lambda i, j: (i, j)),
            pl.BlockSpec((tm, LANES), lambda i, j: (i, 0)),
        ],
        out_shape=[
            jax.ShapeDtypeStruct((t, n), BF16),
            jax.ShapeDtypeStruct((t, LANES), F32),
        ],
        scratch_shapes=[
            pltpu.VMEM((tm, d), BF16),
            pltpu.VMEM((tm, LANES), F32),
            pltpu.VMEM((tm, LANES), F32),
            pltpu.VMEM((tm, LANES), F32),
        ],
        compiler_params=pltpu.CompilerParams(
            dimension_semantics=("parallel", "arbitrary"),
            vmem_limit_bytes=48 * MIB),
        name="in_proj",
    )(x2d, pos2d, norm_w, w_cat, w_dt, dt_bias, inv_freq_lane)


def _ssd_kernel(xs_ref, z_ref, bc_ref, dt_ref, cwx_ref, cbx_ref, cwbc_ref, cbbc_ref,
                alog_ref, dskip_ref, nw_ref, e_ref, tri_ref,
                y_ref, xpad, bcpad, hstate, ydiag, *, n_groups, d_state, head_dim):
    c = pl.program_id(1)
    L = SSM_CHUNK
    tail = 8
    d_inner = xs_ref.shape[1]
    gw = d_inner // n_groups

    @pl.when(c == 0)
    def _():
        xpad[0:tail, :] = jnp.zeros((tail, xpad.shape[1]), F32)
        bcpad[0:tail, :] = jnp.zeros((tail, bcpad.shape[1]), F32)
        hstate[...] = jnp.zeros(hstate.shape, F32)

    xpad[tail:tail + L, :] = xs_ref[...].astype(F32)
    bcpad[tail:tail + L, :] = bc_ref[...].astype(F32)

    def conv_silu(pad, w_ref, b_ref):
        acc = b_ref[...]
        for k in range(SSM_CONV):
            off = tail - (SSM_CONV - 1) + k
            acc = acc + pad[off:off + L, :] * w_ref[k:k + 1, :]
        return _silu(acc)

    xs = conv_silu(xpad, cwx_ref, cbx_ref)
    bc = conv_silu(bcpad, cwbc_ref, cbbc_ref)
    xpad[0:tail, :] = xpad[L:L + tail, :]
    bcpad[0:tail, :] = bcpad[L:L + tail, :]

    dt = dt_ref[...]
    a_neg = -jnp.exp(alog_ref[...])
    a_cs = _dot_exact(tri_ref[...], dt * a_neg)
    a_cs_t = a_cs.T
    ea = jnp.exp(a_cs)
    dte = jnp.exp(a_cs[L - 1:L, :] - a_cs)
    expanded = _dot_exact(jnp.concatenate([dt, ea, dte], axis=0), e_ref[...])
    dt_e = expanded[0:L]
    ea_e = expanded[L:2 * L]
    dte_e = expanded[2 * L:3 * L]

    xdt = xs * dt_e
    xdt_b = xdt.astype(BF16)
    xw_b = (xdt * dte_e).astype(BF16)

    row = lax.broadcasted_iota(jnp.int32, (L, L), 0)
    col = lax.broadcasted_iota(jnp.int32, (L, L), 1)
    causal = col <= row
    lane = lax.broadcasted_iota(jnp.int32, (L, LANES), 1)
    heads_per_group = gw // head_dim
    pairs_per_group = heads_per_group // 2

    y_off = []
    for g in range(n_groups):
        b_g = bc[:, g * d_state:(g + 1) * d_state]
        c_g = bc[:, (n_groups + g) * d_state:(n_groups + g + 1) * d_state].astype(BF16)
        cb = _dot_nt(c_g, b_g.astype(BF16))
        for p2 in range(pairs_per_group):
            pair = g * pairs_per_group + p2
            ms = []
            for h in (2 * pair, 2 * pair + 1):
                seg = a_cs[:, h:h + 1] - a_cs_t[h:h + 1, :]
                dec = jnp.exp(jnp.where(causal, seg, -jnp.inf))
                ms.append((cb * dec).astype(BF16))
            lhs = jnp.concatenate(ms, axis=1)
            xp = xdt_b[:, pair * LANES:(pair + 1) * LANES]
            zero = jnp.zeros_like(xp)
            rhs = jnp.concatenate([jnp.where(lane < head_dim, xp, zero),
                                   jnp.where(lane >= head_dim, xp, zero)], axis=0)
            ydiag[:, pair * LANES:(pair + 1) * LANES] = _dot(lhs, rhs)
        h_g = hstate[g]
        ea_g = ea_e[:, g * gw:(g + 1) * gw]
        y_off.append(_dot(c_g, h_g.astype(BF16)) * ea_g)
        new = _dot(b_g.T.astype(BF16), xw_b[:, g * gw:(g + 1) * gw])
        hstate[g] = h_g * ea_g[L - 1:L, :] + new

    zf = z_ref[...].astype(F32)
    gate = _silu(zf)
    for g in range(n_groups):
        sl = slice(g * gw, (g + 1) * gw)
        yg = (ydiag[:, sl] + y_off[g] + xs[:, sl] * dskip_ref[:, sl]) * gate[:, sl]
        y_ref[:, sl] = _rms(yg, nw_ref[:, sl]).astype(y_ref.dtype)


def _ssd(proj, dt, cw_x, cb_x, cw_bc, cb_bc, a_log, d_skip_e, norm_w, e_mat, tri,
         *, batch, seq, d_inner, col_xs, col_z, col_bc):
    L = SSM_CHUNK
    nc = seq // L
    t = batch * seq
    bc_w = 2 * SSM_N_GROUPS * SSM_D_STATE
    row = lambda b, c: b * nc + c
    kern = functools.partial(_ssd_kernel, n_groups=SSM_N_GROUPS, d_state=SSM_D_STATE,
                             head_dim=SSM_HEAD_DIM)
    return pl.pallas_call(
        kern,
        grid=(batch, nc),
        in_specs=[
            pl.BlockSpec((L, d_inner), lambda b, c: (row(b, c), col_xs // d_inner)),
            pl.BlockSpec((L, d_inner), lambda b, c: (row(b, c), col_z // d_inner)),
            pl.BlockSpec((L, bc_w), lambda b, c: (row(b, c), col_bc // bc_w)),
            pl.BlockSpec((L, LANES), lambda b, c: (row(b, c), 0)),
            pl.BlockSpec((SSM_CONV, d_inner), lambda b, c: (0, 0)),
            pl.BlockSpec((1, d_inner), lambda b, c: (0, 0)),
            pl.BlockSpec((SSM_CONV, bc_w), lambda b, c: (0, 0)),
            pl.BlockSpec((1, bc_w), lambda b, c: (0, 0)),
            pl.BlockSpec((1, LANES), lambda b, c: (0, 0)),
            pl.BlockSpec((1, d_inner), lambda b, c: (0, 0)),
            pl.BlockSpec((1, d_inner), lambda b, c: (0, 0)),
            pl.BlockSpec((LANES, d_inner), lambda b, c: (0, 0)),
            pl.BlockSpec((L, L), lambda b, c: (0, 0)),
        ],
        out_specs=pl.BlockSpec((L, d_inner), lambda b, c: (row(b, c), 0)),
        out_shape=jax.ShapeDtypeStruct((t, d_inner), BF16),
        scratch_shapes=[
            pltpu.VMEM((L + 8, d_inner), F32),
            pltpu.VMEM((L + 8, bc_w), F32),
            pltpu.VMEM((SSM_N_GROUPS, SSM_D_STATE, d_inner // SSM_N_GROUPS), F32),
            pltpu.VMEM((L, d_inner), F32),
        ],
        compiler_params=pltpu.CompilerParams(
            dimension_semantics=("parallel", "arbitrary"),
            vmem_limit_bytes=40 * MIB),
        name="ssd",
    )(proj, proj, proj, dt, cw_x, cb_x, cw_bc, cb_bc, a_log, d_skip_e, norm_w, e_mat, tri)


def _attn_kernel(q_ref, k_ref, v_ref, lq1_ref, lk1_ref, lq2_ref, lk2_ref, subw_ref,
                 o_ref, m_scr, l_scr, acc_scr, *, tq, lam_init):
    qi = pl.program_id(2)
    q = q_ref[...]
    lane = lax.broadcasted_iota(jnp.int32, q.shape, 1)
    zero = jnp.zeros_like(q)
    q2 = jnp.concatenate([jnp.where(lane < ATT_HEAD_DIM, q, zero),
                          jnp.where(lane >= ATT_HEAD_DIM, q, zero)], axis=0)
    m_scr[...] = jnp.full(m_scr.shape, -jnp.inf, F32)
    l_scr[...] = jnp.zeros(l_scr.shape, F32)
    acc_scr[...] = jnp.zeros(acc_scr.shape, F32)

    def step(j, masked):
        start = pl.multiple_of(j * tq, tq)
        k = k_ref[pl.ds(start, tq), :]
        v = v_ref[pl.ds(start, tq), :]
        s = _dot_nt(q2, k)
        if masked:
            r = lax.broadcasted_iota(jnp.int32, s.shape, 0)
            r = jnp.where(r >= tq, r - tq, r)
            cidx = lax.broadcasted_iota(jnp.int32, s.shape, 1)
            s = jnp.where(cidx <= r, s, -jnp.inf)
        m_prev = m_scr[...]
        m_new = jnp.maximum(m_prev, jnp.max(s, axis=-1, keepdims=True))
        alpha = jnp.exp(m_prev - m_new)
        p = jnp.exp(s - m_new)
        l_scr[...] = alpha * l_scr[...] + jnp.sum(p, axis=-1, keepdims=True)
        acc_scr[...] = alpha * acc_scr[...] + _dot(p.astype(BF16), v)
        m_scr[...] = m_new

    def body(j, carry):
        step(j, False)
        return carry

    lax.fori_loop(0, qi, body, 0)
    step(qi, True)

    o = acc_scr[...] / l_scr[...]
    lam = (jnp.exp(jnp.sum(lq1_ref[...] * lk1_ref[...]))
           - jnp.exp(jnp.sum(lq2_ref[...] * lk2_ref[...])) + lam_init)
    od = o[0:tq] - lam * o[tq:2 * tq]
    o_ref[...] = (_rms(od, subw_ref[...]) * (1.0 - lam_init)).astype(o_ref.dtype)


def _attention(proj, lq1, lk1, lq2, lk2, subw, *, batch, seq, n_heads, col_q, col_k, col_v,
               lam_init, tq=512):
    nq = seq // tq
    t = batch * seq
    hd = ATT_V_DIM
    kern = functools.partial(_attn_kernel, tq=tq, lam_init=lam_init)
    small = pl.BlockSpec((1, ATT_HEAD_DIM), lambda b, h, i: (0, 0))
    return pl.pallas_call(
        kern,
        grid=(batch, n_heads, nq),
        in_specs=[
            pl.BlockSpec((tq, hd), lambda b, h, i: (b * nq + i, col_q // hd + h)),
            pl.BlockSpec((seq, hd), lambda b, h, i: (b, col_k // hd + h)),
            pl.BlockSpec((seq, hd), lambda b, h, i: (b, col_v // hd + h)),
            small, small, small, small,
            pl.BlockSpec((1, hd), lambda b, h, i: (0, 0)),
        ],
        out_specs=pl.BlockSpec((tq, hd), lambda b, h, i: (b * nq + i, h)),
        out_shape=jax.ShapeDtypeStruct((t, n_heads * hd), BF16),
        scratch_shapes=[
            pltpu.VMEM((2 * tq, 1), F32),
            pltpu.VMEM((2 * tq, 1), F32),
            pltpu.VMEM((2 * tq, hd), F32),
        ],
        compiler_params=pltpu.CompilerParams(
            dimension_semantics=("parallel", "parallel", "arbitrary"),
            vmem_limit_bytes=32 * MIB),
        name="diff_attn",
    )(proj, proj, proj, lq1, lk1, lq2, lk2, subw)


def _mix_kernel(x_ref, ys_ref, ya_ref, npre_ref, npost_ref, wg_ref, wso_ref, wao_ref, wmix_ref, o_ref):
    x = x_ref[...]
    d = x.shape[1]
    h = _rms(x, npre_ref[...]).astype(BF16)
    gates = jax.nn.sigmoid(_dot(h, wg_ref[...]))
    y_ssm = _dot(ys_ref[...], wso_ref[...])
    y_att = _dot(ya_ref[...], wao_ref[...])
    blend = gates[:, 0:d] * y_ssm + gates[:, d:2 * d] * y_att
    mixed = _dot(blend.astype(BF16), wmix_ref[...])
    o_ref[...] = x + _rms(mixed, npost_ref[...])


def _mix(x2d, y_ssm, y_att, n_pre, n_post, w_gate, w_so, w_ao, w_mix, *, tm=512):
    t, d = x2d.shape
    rows = lambda w: pl.BlockSpec((tm, w), lambda i: (i, 0))
    return pl.pallas_call(
        _mix_kernel,
        grid=(t // tm,),
        in_specs=[rows(d), rows(y_ssm.shape[1]), rows(y_att.shape[1]),
                  _const_spec(n_pre.shape), _const_spec(n_post.shape),
                  _const_spec(w_gate.shape), _const_spec(w_so.shape),
                  _const_spec(w_ao.shape), _const_spec(w_mix.shape)],
        out_specs=rows(d),
        out_shape=jax.ShapeDtypeStruct((t, d), F32),
        compiler_params=pltpu.CompilerParams(
            dimension_semantics=("parallel",), vmem_limit_bytes=48 * MIB),
        name="mix",
    )(x2d, y_ssm, y_att, n_pre, n_post, w_gate, w_so, w_ao, w_mix)


def _ffn_kernel(x_ref, npre_ref, npost_ref, wg_ref, wu_ref, wd_ref, o_ref):
    x = x_ref[...]
    h = _rms(x, npre_ref[...]).astype(BF16)
    act = (_silu(_dot(h, wg_ref[...])) * _dot(h, wu_ref[...])).astype(BF16)
    f = _dot(act, wd_ref[...])
    o_ref[...] = x + _rms(f, npost_ref[...])


def _ffn(x2d, n_pre, n_post, w_g, w_u, w_d, *, tm=512):
    t, d = x2d.shape
    rows = pl.BlockSpec((tm, d), lambda i: (i, 0))
    return pl.pallas_call(
        _ffn_kernel,
        grid=(t // tm,),
        in_specs=[rows, _const_spec(n_pre.shape), _const_spec(n_post.shape),
                  _const_spec(w_g.shape), _const_spec(w_u.shape), _const_spec(w_d.shape)],
        out_specs=rows,
        out_shape=jax.ShapeDtypeStruct((t, d), F32),
        compiler_params=pltpu.CompilerParams(
            dimension_semantics=("parallel",), vmem_limit_bytes=56 * MIB),
        name="ffn",
    )(x2d, n_pre, n_post, w_g, w_u, w_d)


def kernel(x, positions, w_in, conv_w, conv_b, dt_bias, a_log, d_skip, ssm_norm_w, w_ssm_out, lam_q1, lam_k1, lam_q2, lam_k2, attn_subln_w, w_attn_out, w_mix_out, norm_pre_mix, norm_post_mix, norm_pre_ffn, norm_post_ffn, w_ffn_gate, w_ffn_up, w_ffn_down):
    batch, seq, d_model = x.shape
    depth = w_in.shape[0]
    d_inner = w_ssm_out.shape[1]
    n_ssm_heads = dt_bias.shape[1]
    bc_w = 2 * SSM_N_GROUPS * SSM_D_STATE
    att_w = w_attn_out.shape[1]
    n_att_heads = att_w // ATT_V_DIM
    t = batch * seq

    z_end = d_inner
    xbc_end = z_end + d_inner + bc_w
    dt_end = xbc_end + n_ssm_heads
    q_end = dt_end + att_w
    k_end = q_end + att_w
    v_end = k_end + att_w

    col_xs, col_z = 0, d_inner
    col_bc = 2 * d_inner
    col_q = col_bc + bc_w
    col_k = col_q + att_w
    col_v = col_k + att_w
    tn = att_w
    assert d_inner % tn == 0 and bc_w == tn and col_q % tn == 0

    inv_freq = ROPE_THETA ** (-jnp.arange(0, ROPE_DIM, 2, dtype=F32) / ROPE_DIM)
    lane = jnp.arange(LANES) % ATT_HEAD_DIM
    inv_freq_lane = jnp.where(lane < ROPE_DIM, inv_freq[lane % (ROPE_DIM // 2)], 0.0)[None, :].astype(F32)
    head_of_col = jnp.arange(d_inner) // SSM_HEAD_DIM
    e_mat = (jnp.arange(LANES)[:, None] == head_of_col[None, :]).astype(F32)
    tri = (jnp.arange(SSM_CHUNK)[None, :] <= jnp.arange(SSM_CHUNK)[:, None]).astype(F32)

    def pad_heads(v):
        return jnp.pad(v.astype(F32), (0, LANES - n_ssm_heads))[None, :]

    x2d = x.reshape(t, d_model)
    pos2d = positions.reshape(t, 1)
    for l in range(depth):
        lam_init = 0.8 - 0.6 * math.exp(-0.3 * l)
        wl = w_in[l]
        w_cat = jnp.concatenate(
            [wl[:, z_end:z_end + d_inner], wl[:, :z_end], wl[:, z_end + d_inner:xbc_end],
             wl[:, dt_end:v_end]], axis=1).astype(BF16)
        w_dt = jnp.pad(wl[:, xbc_end:dt_end], ((0, 0), (0, LANES - n_ssm_heads))).astype(BF16)
        w_gate = wl[:, v_end:].astype(BF16)

        proj, dt = _in_proj(x2d, pos2d, norm_pre_mix[l][None, :], w_cat, w_dt, pad_heads(dt_bias[l]),
                            inv_freq_lane, j_q=col_q // tn, j_k=col_k // tn, tn=tn)

        y_ssm = _ssd(proj, dt,
                     conv_w[l][:, :d_inner], conv_b[l][None, :d_inner],
                     conv_w[l][:, d_inner:], conv_b[l][None, d_inner:],
                     pad_heads(a_log[l]), jnp.repeat(d_skip[l], SSM_HEAD_DIM)[None, :],
                     ssm_norm_w[l][None, :], e_mat, tri,
                     batch=batch, seq=seq, d_inner=d_inner, col_xs=col_xs, col_z=col_z, col_bc=col_bc)

        y_att = _attention(proj, lam_q1[l][None, :], lam_k1[l][None, :], lam_q2[l][None, :],
                           lam_k2[l][None, :], attn_subln_w[l][None, :],
                           batch=batch, seq=seq, n_heads=n_att_heads,
                           col_q=col_q, col_k=col_k, col_v=col_v, lam_init=lam_init)

        x2d = _mix(x2d, y_ssm, y_att, norm_pre_mix[l][None, :], norm_post_mix[l][None, :],
                   w_gate, w_ssm_out[l].astype(BF16), w_attn_out[l].astype(BF16),
                   w_mix_out[l].astype(BF16))
        x2d = _ffn(x2d, norm_pre_ffn[l][None, :], norm_post_ffn[l][None, :],
                   w_ffn_gate[l].astype(BF16), w_ffn_up[l].astype(BF16), w_ffn_down[l].astype(BF16))
    return x2d.reshape(batch, seq, d_model)
```

```python
import functools
import math

import jax
import jax.numpy as jnp
from jax import lax
from jax.experimental import pallas as pl
from jax.experimental.pallas import tpu as pltpu

F32 = jnp.float32
BF16 = jnp.bfloat16

LANES = 128

SSM_HEAD_DIM = 64
SSM_N_GROUPS = 4
SSM_D_STATE = 128
SSM_CONV = 4
SSM_CHUNK = 128
ATT_HEAD_DIM = 64
ATT_V_DIM = 2 * ATT_HEAD_DIM
ROPE_THETA = 500000.0
ROPE_DIM = ATT_HEAD_DIM // 4
RMS_EPS = 1e-6
Q_SCALE = ATT_HEAD_DIM ** -0.5 * math.log2(math.e)

MIB = 1024 * 1024


def _rms(x, w):
    return x * lax.rsqrt(jnp.mean(x * x, axis=-1, keepdims=True) + RMS_EPS) * w


def _silu(x):
    return x * jax.nn.sigmoid(x)


def _dot(a, b):
    return jnp.dot(a, b, preferred_element_type=F32)


def _dot_nt(a, b):
    return lax.dot_general(a, b, (((1,), (1,)), ((), ())), preferred_element_type=F32)


def _split3(x):
    hi = x.astype(BF16)
    r = x - hi.astype(F32)
    mid = r.astype(BF16)
    lo = (r - mid.astype(F32)).astype(BF16)
    return hi, mid, lo


def _const_spec(shape):
    nd = len(shape)
    return pl.BlockSpec(shape, lambda *_: (0,) * nd, pipeline_mode=pl.Buffered(1))


def _inproj_kernel(x_ref, pos_ref, nw_ref, w_ref, wdt_ref, dtb_ref, invf_ref,
                   out_ref, dt_ref, h_scr, cos_scr, s1_scr, s2_scr, *, j_q, j_k):
    j = pl.program_id(1)

    @pl.when(j == 0)
    def _():
        hb = _rms(x_ref[...], nw_ref[...]).astype(BF16)
        h_scr[...] = hb
        dt_ref[...] = jax.nn.softplus(_dot(hb, wdt_ref[...]) + dtb_ref[...])
        ang = pos_ref[...].astype(F32) * invf_ref[...]
        lane = lax.broadcasted_iota(jnp.int32, ang.shape, 1) % ATT_HEAD_DIM
        sn = jnp.sin(ang)
        half = ROPE_DIM // 2
        cos_scr[...] = jnp.cos(ang)
        s1_scr[...] = jnp.where((lane >= half) & (lane < ROPE_DIM), sn, 0.0)
        s2_scr[...] = jnp.where(lane < half, -sn, 0.0)

    acc = _dot(h_scr[...], w_ref[...])
    is_rope = (j == j_q) | (j == j_k)

    @pl.when(is_rope)
    def _():
        scale = jnp.where(j == j_q, Q_SCALE, 1.0).astype(F32)
        half = ROPE_DIM // 2
        for c in range(acc.shape[1] // LANES):
            a = acc[:, c * LANES:(c + 1) * LANES]
            r = (a * cos_scr[...] + pltpu.roll(a, half, 1) * s1_scr[...]
                 + pltpu.roll(a, LANES - half, 1) * s2_scr[...])
            out_ref[:, c * LANES:(c + 1) * LANES] = (r * scale).astype(out_ref.dtype)

    @pl.when(jnp.logical_not(is_rope))
    def _():
        out_ref[...] = acc.astype(out_ref.dtype)


def _in_proj(x2d, pos2d, norm_w, w_cat, w_dt, dt_bias, inv_freq_lane, *, j_q, j_k, tm=1024, tn=1024):
    t, d = x2d.shape
    n = w_cat.shape[1]
    grid = (t // tm, n // tn)
    return pl.pallas_call(
        functools.partial(_inproj_kernel, j_q=j_q, j_k=j_k),
        grid=grid,
        in_specs=[
            pl.BlockSpec((tm, d), lambda i, j: (i, 0)),
            pl.BlockSpec((tm, 1), lambda i, j: (i, 0)),
            pl.BlockSpec((1, d), lambda i, j: (0, 0)),
            pl.BlockSpec((d, tn), lambda i, j: (0, j)),
            pl.BlockSpec((d, LANES), lambda i, j: (0, 0)),
            pl.BlockSpec((1, LANES), lambda i, j: (0, 0)),
            pl.BlockSpec((1, LANES), lambda i, j: (0, 0)),
        ],
        out_specs=[
            pl.BlockSpec((tm, tn), lambda i, j: (i, j)),
            pl.BlockSpec((tm, LANES), lambda i, j: (i, 0)),
        ],
        out_shape=[
            jax.ShapeDtypeStruct((t, n), BF16),
            jax.ShapeDtypeStruct((t, LANES), F32),
        ],
        scratch_shapes=[
            pltpu.VMEM((tm, d), BF16),
            pltpu.VMEM((tm, LANES), F32),
            pltpu.VMEM((tm, LANES), F32),
            pltpu.VMEM((tm, LANES), F32),
        ],
        compiler_params=pltpu.CompilerParams(
            dimension_semantics=("parallel", "arbitrary"),
            vmem_limit_bytes=48 * MIB),
        name="in_proj",
    )(x2d, pos2d, norm_w, w_cat, w_dt, dt_bias, inv_freq_lane)


def _ssd_kernel(xs_ref, z_ref, bc_ref, dt_ref, cwx_ref, cbx_ref, cwbc_ref, cbbc_ref,
                alog_ref, dskip_ref, nw_ref, e_ref, tri_ref, shift_ref,
                y_ref, xprev, bcprev, hstate, ydiag, *, n_groups, d_state, head_dim, n_heads):
    c = pl.program_id(1)
    L = SSM_CHUNK
    tail = 16
    d_inner = xs_ref.shape[1]
    gw = d_inner // n_groups

    @pl.when(c == 0)
    def _():
        xprev[...] = jnp.zeros(xprev.shape, BF16)
        bcprev[...] = jnp.zeros(bcprev.shape, BF16)
        hstate[...] = jnp.zeros(hstate.shape, F32)

    def conv_silu(cur_ref, prev, w_ref, b_ref):
        cur = cur_ref[...]
        sh = _dot(shift_ref[...], jnp.concatenate([prev[...], cur], axis=0))
        acc = b_ref[...] + cur.astype(F32) * w_ref[SSM_CONV - 1:SSM_CONV, :]
        for k in range(SSM_CONV - 1):
            acc = acc + sh[k * L:(k + 1) * L, :] * w_ref[k:k + 1, :]
        prev[L - tail:L, :] = cur[L - tail:L, :]
        return _silu(acc)

    xs = conv_silu(xs_ref, xprev, cwx_ref, cbx_ref)
    bc = conv_silu(bc_ref, bcprev, cwbc_ref, cbbc_ref)

    dt = dt_ref[...]
    a_neg = -jnp.exp(alog_ref[...])
    a_cs = _dot(tri_ref[...], jnp.concatenate(_split3(dt * a_neg), axis=0))
    a_cs_t = a_cs.T
    ea = jnp.exp(a_cs)
    dte = jnp.exp(a_cs[L - 1:L, :] - a_cs)
    stacked = jnp.concatenate([dt, ea, dte], axis=0)
    head_lane = lax.broadcasted_iota(jnp.int32, stacked.shape, 1)
    hi, mid, lo = _split3(jnp.where(head_lane < n_heads, stacked, 0.0))
    packed = (hi.astype(F32) + pltpu.roll(mid.astype(F32), n_heads, 1)
              + pltpu.roll(lo.astype(F32), 2 * n_heads, 1)).astype(BF16)
    expanded = _dot(packed, e_ref[...])
    dt_e = expanded[0:L]
    ea_e = expanded[L:2 * L]
    dte_e = expanded[2 * L:3 * L]

    xdt = xs * dt_e
    xdt_b = xdt.astype(BF16)
    xw_b = (xdt * dte_e).astype(BF16)

    row = lax.broadcasted_iota(jnp.int32, (L, L), 0)
    col = lax.broadcasted_iota(jnp.int32, (L, L), 1)
    causal = col <= row
    lane = lax.broadcasted_iota(jnp.int32, (L, LANES), 1)
    heads_per_group = gw // head_dim
    pairs_per_group = heads_per_group // 2

    y_off = []
    for g in range(n_groups):
        b_g = bc[:, g * d_state:(g + 1) * d_state]
        c_g = bc[:, (n_groups + g) * d_state:(n_groups + g + 1) * d_state].astype(BF16)
        cb = _dot_nt(c_g, b_g.astype(BF16))
        for p2 in range(pairs_per_group):
            pair = g * pairs_per_group + p2
            ms = []
            for h in (2 * pair, 2 * pair + 1):
                seg = a_cs[:, h:h + 1] - a_cs_t[h:h + 1, :]
                dec = jnp.exp(jnp.where(causal, seg, -jnp.inf))
                ms.append((cb * dec).astype(BF16))
            lhs = jnp.concatenate(ms, axis=1)
            xp = xdt_b[:, pair * LANES:(pair + 1) * LANES]
            zero = jnp.zeros_like(xp)
            rhs = jnp.concatenate([jnp.where(lane < head_dim, xp, zero),
                                   jnp.where(lane >= head_dim, xp, zero)], axis=0)
            ydiag[:, pair * LANES:(pair + 1) * LANES] = _dot(lhs, rhs)
        h_g = hstate[g]
        ea_g = ea_e[:, g * gw:(g + 1) * gw]
        y_off.append(_dot(c_g, h_g.astype(BF16)) * ea_g)
        new = _dot(b_g.T.astype(BF16), xw_b[:, g * gw:(g + 1) * gw])
        hstate[g] = h_g * ea_g[L - 1:L, :] + new

    zf = z_ref[...].astype(F32)
    gate = _silu(zf)
    for g in range(n_groups):
        sl = slice(g * gw, (g + 1) * gw)
        yg = (ydiag[:, sl] + y_off[g] + xs[:, sl] * dskip_ref[:, sl]) * gate[:, sl]
        y_ref[:, sl] = _rms(yg, nw_ref[:, sl]).astype(y_ref.dtype)


def _ssd(proj, dt, cw_x, cb_x, cw_bc, cb_bc, a_log, d_skip_e, norm_w, e_mat, tri, shift,
         *, batch, seq, d_inner, col_xs, col_z, col_bc):
    L = SSM_CHUNK
    nc = seq // L
    t = batch * seq
    bc_w = 2 * SSM_N_GROUPS * SSM_D_STATE
    row = lambda b, c: b * nc + c
    kern = functools.partial(_ssd_kernel, n_groups=SSM_N_GROUPS, d_state=SSM_D_STATE,
                             head_dim=SSM_HEAD_DIM, n_heads=d_inner // SSM_HEAD_DIM)
    return pl.pallas_call(
        kern,
        grid=(batch, nc),
        in_specs=[
            pl.BlockSpec((L, d_inner), lambda b, c: (row(b, c), col_xs // d_inner)),
            pl.BlockSpec((L, d_inner), lambda b, c: (row(b, c), col_z // d_inner)),
            pl.BlockSpec((L, bc_w), lambda b, c: (row(b, c), col_bc // bc_w)),
            pl.BlockSpec((L, LANES), lambda b, c: (row(b, c), 0)),
            pl.BlockSpec((SSM_CONV, d_inner), lambda b, c: (0, 0)),
            pl.BlockSpec((1, d_inner), lambda b, c: (0, 0)),
            pl.BlockSpec((SSM_CONV, bc_w), lambda b, c: (0, 0)),
            pl.BlockSpec((1, bc_w), lambda b, c: (0, 0)),
            pl.BlockSpec((1, LANES), lambda b, c: (0, 0)),
            pl.BlockSpec((1, d_inner), lambda b, c: (0, 0)),
            pl.BlockSpec((1, d_inner), lambda b, c: (0, 0)),
            pl.BlockSpec((LANES, d_inner), lambda b, c: (0, 0)),
            pl.BlockSpec((L, 3 * L), lambda b, c: (0, 0)),
            pl.BlockSpec(((SSM_CONV - 1) * L, 2 * L), lambda b, c: (0, 0)),
        ],
        out_specs=pl.BlockSpec((L, d_inner), lambda b, c: (row(b, c), 0)),
        out_shape=jax.ShapeDtypeStruct((t, d_inner), BF16),
        scratch_shapes=[
            pltpu.VMEM((L, d_inner), BF16),
            pltpu.VMEM((L, bc_w), BF16),
            pltpu.VMEM((SSM_N_GROUPS, SSM_D_STATE, d_inner // SSM_N_GROUPS), F32),
            pltpu.VMEM((L, d_inner), F32),
        ],
        compiler_params=pltpu.CompilerParams(
            dimension_semantics=("parallel", "arbitrary"),
            vmem_limit_bytes=40 * MIB),
        name="ssd",
    )(proj, proj, proj, dt, cw_x, cb_x, cw_bc, cb_bc, a_log, d_skip_e, norm_w, e_mat, tri, shift)


def _attn_kernel(q_ref, k_ref, v_ref, lq1_ref, lk1_ref, lq2_ref, lk2_ref, subw_ref,
                 o_ref, vt_scr, *, tq, lam_init):
    nq = vt_scr.shape[0]
    for jb in range(nq):
        vt_scr[jb] = v_ref[jb * tq:(jb + 1) * tq, :].T

    lam = (jnp.exp(jnp.sum(lq1_ref[...] * lk1_ref[...]))
           - jnp.exp(jnp.sum(lq2_ref[...] * lk2_ref[...])) + lam_init)
    subw = subw_ref[...] * (1.0 - lam_init)

    def scores(qi, j):
        q = q_ref[qi * tq:(qi + 1) * tq, :]
        lane = lax.broadcasted_iota(jnp.int32, q.shape, 1)
        zero = jnp.zeros_like(q)
        q2 = jnp.concatenate([jnp.where(lane < ATT_HEAD_DIM, q, zero),
                              jnp.where(lane >= ATT_HEAD_DIM, q, zero)], axis=0)
        return _dot_nt(k_ref[j * tq:(j + 1) * tq, :], q2)

    steps = [(qi, j) for qi in range(nq) for j in range(qi + 1)]
    s_next = scores(*steps[0])
    m = l = acc = None
    for t, (qi, j) in enumerate(steps):
        s = s_next
        if t + 1 < len(steps):
            s_next = scores(*steps[t + 1])
        if j == qi:
            kv = lax.broadcasted_iota(jnp.int32, s.shape, 0)
            r = lax.broadcasted_iota(jnp.int32, s.shape, 1)
            r = jnp.where(r >= tq, r - tq, r)
            s = jnp.where(kv <= r, s, -jnp.inf)
        s_max = jnp.max(s, axis=0, keepdims=True)
        if j == 0:
            m = s_max
            p = jnp.exp2(s - m)
            l = jnp.sum(p, axis=0, keepdims=True)
            acc = _dot(vt_scr[j], p.astype(BF16))
        else:
            m_new = jnp.maximum(m, s_max)
            alpha = jnp.exp2(m - m_new)
            p = jnp.exp2(s - m_new)
            l = alpha * l + jnp.sum(p, axis=0, keepdims=True)
            acc = alpha * acc + _dot(vt_scr[j], p.astype(BF16))
            m = m_new
        if j == qi:
            o = acc / l
            od = o[:, 0:tq] - lam * o[:, tq:2 * tq]
            od = od * lax.rsqrt(jnp.mean(od * od, axis=0, keepdims=True) + RMS_EPS)
            o_ref[qi * tq:(qi + 1) * tq, :] = (od * subw).T.astype(o_ref.dtype)


def _attention(proj, lq1, lk1, lq2, lk2, subw, *, batch, seq, n_heads, col_q, col_k, col_v,
               lam_init, tq=512):
    nq = seq // tq
    t = batch * seq
    hd = ATT_V_DIM
    kern = functools.partial(_attn_kernel, tq=tq, lam_init=lam_init)
    small = pl.BlockSpec((1, ATT_HEAD_DIM), lambda b, h: (0, 0))
    return pl.pallas_call(
        kern,
        grid=(batch, n_heads),
        in_specs=[
            pl.BlockSpec((seq, hd), lambda b, h: (b, col_q // hd + h)),
            pl.BlockSpec((seq, hd), lambda b, h: (b, col_k // hd + h)),
            pl.BlockSpec((seq, hd), lambda b, h: (b, col_v // hd + h)),
            small, small, small, small,
            pl.BlockSpec((hd, 1), lambda b, h: (0, 0)),
        ],
        out_specs=pl.BlockSpec((seq, hd), lambda b, h: (b, h)),
        out_shape=jax.ShapeDtypeStruct((t, n_heads * hd), BF16),
        scratch_shapes=[pltpu.VMEM((nq, hd, tq), BF16)],
        compiler_params=pltpu.CompilerParams(
            dimension_semantics=("parallel", "parallel"),
            vmem_limit_bytes=32 * MIB),
        name="diff_attn",
    )(proj, proj, proj, lq1, lk1, lq2, lk2, subw)


def _mix_kernel(x_ref, ys_ref, ya_ref, npre_ref, npost_ref, wg_ref, wso_ref, wao_ref, wmix_ref, o_ref):
    x = x_ref[...]
    d = x.shape[1]
    h = _rms(x, npre_ref[...]).astype(BF16)
    gates = jax.nn.sigmoid(_dot(h, wg_ref[...]))
    y_ssm = _dot(ys_ref[...], wso_ref[...])
    y_att = _dot(ya_ref[...], wao_ref[...])
    blend = gates[:, 0:d] * y_ssm + gates[:, d:2 * d] * y_att
    mixed = _dot(blend.astype(BF16), wmix_ref[...])
    o_ref[...] = x + _rms(mixed, npost_ref[...])


def _mix(x2d, y_ssm, y_att, n_pre, n_post, w_gate, w_so, w_ao, w_mix, *, tm=512):
    t, d = x2d.shape
    rows = lambda w: pl.BlockSpec((tm, w), lambda i: (i, 0))
    return pl.pallas_call(
        _mix_kernel,
        grid=(t // tm,),
        in_specs=[rows(d), rows(y_ssm.shape[1]), rows(y_att.shape[1]),
                  _const_spec(n_pre.shape), _const_spec(n_post.shape),
                  _const_spec(w_gate.shape), _const_spec(w_so.shape),
                  _const_spec(w_ao.shape), _const_spec(w_mix.shape)],
        out_specs=rows(d),
        out_shape=jax.ShapeDtypeStruct((t, d), F32),
        compiler_params=pltpu.CompilerParams(
            dimension_semantics=("parallel",), vmem_limit_bytes=48 * MIB),
        name="mix",
    )(x2d, y_ssm, y_att, n_pre, n_post, w_gate, w_so, w_ao, w_mix)


def _ffn_kernel(x_ref, npre_ref, npost_ref, wg_ref, wu_ref, wd_ref, o_ref):
    x = x_ref[...]
    h = _rms(x, npre_ref[...]).astype(BF16)
    act = (_silu(_dot(h, wg_ref[...])) * _dot(h, wu_ref[...])).astype(BF16)
    f = _dot(act, wd_ref[...])
    o_ref[...] = x + _rms(f, npost_ref[...])


def _ffn(x2d, n_pre, n_post, w_g, w_u, w_d, *, tm=512):
    t, d = x2d.shape
    rows = pl.BlockSpec((tm, d), lambda i: (i, 0))
    return pl.pallas_call(
        _ffn_kernel,
        grid=(t // tm,),
        in_specs=[rows, _const_spec(n_pre.shape), _const_spec(n_post.shape),
                  _const_spec(w_g.shape), _const_spec(w_u.shape), _const_spec(w_d.shape)],
        out_specs=rows,
        out_shape=jax.ShapeDtypeStruct((t, d), F32),
        compiler_params=pltpu.CompilerParams(
            dimension_semantics=("parallel",), vmem_limit_bytes=56 * MIB),
        name="ffn",
    )(x2d, n_pre, n_post, w_g, w_u, w_d)


def kernel(x, positions, w_in, conv_w, conv_b, dt_bias, a_log, d_skip, ssm_norm_w, w_ssm_out, lam_q1, lam_k1, lam_q2, lam_k2, attn_subln_w, w_attn_out, w_mix_out, norm_pre_mix, norm_post_mix, norm_pre_ffn, norm_post_ffn, w_ffn_gate, w_ffn_up, w_ffn_down):
    batch, seq, d_model = x.shape
    depth = w_in.shape[0]
    d_inner = w_ssm_out.shape[1]
    n_ssm_heads = dt_bias.shape[1]
    bc_w = 2 * SSM_N_GROUPS * SSM_D_STATE
    att_w = w_attn_out.shape[1]
    n_att_heads = att_w // ATT_V_DIM
    t = batch * seq

    z_end = d_inner
    xbc_end = z_end + d_inner + bc_w
    dt_end = xbc_end + n_ssm_heads
    q_end = dt_end + att_w
    k_end = q_end + att_w
    v_end = k_end + att_w

    col_xs, col_z = 0, d_inner
    col_bc = 2 * d_inner
    col_q = col_bc + bc_w
    col_k = col_q + att_w
    col_v = col_k + att_w
    tn = att_w
    assert d_inner % tn == 0 and bc_w == tn and col_q % tn == 0

    inv_freq = ROPE_THETA ** (-jnp.arange(0, ROPE_DIM, 2, dtype=F32) / ROPE_DIM)
    lane = jnp.arange(LANES) % ATT_HEAD_DIM
    inv_freq_lane = jnp.where(lane < ROPE_DIM, inv_freq[lane % (ROPE_DIM // 2)], 0.0)[None, :].astype(F32)
    head_of_col = jnp.arange(d_inner) // SSM_HEAD_DIM
    assert 3 * n_ssm_heads <= LANES
    term_row = jnp.arange(LANES)
    e_mat = ((term_row[:, None] % n_ssm_heads == head_of_col[None, :])
             & (term_row[:, None] < 3 * n_ssm_heads)).astype(BF16)
    tri = (jnp.arange(SSM_CHUNK)[None, :] <= jnp.arange(SSM_CHUNK)[:, None]).astype(BF16)
    tri = jnp.concatenate([tri, tri, tri], axis=1)
    sh_row = jnp.arange((SSM_CONV - 1) * SSM_CHUNK)
    sh_src = SSM_CHUNK + sh_row % SSM_CHUNK - (SSM_CONV - 1 - sh_row // SSM_CHUNK)
    shift = (jnp.arange(2 * SSM_CHUNK)[None, :] == sh_src[:, None]).astype(BF16)

    def pad_heads(v):
        return jnp.pad(v.astype(F32), (0, LANES - n_ssm_heads))[None, :]

    x2d = x.reshape(t, d_model)
    pos2d = positions.reshape(t, 1)
    for l in range(depth):
        lam_init = 0.8 - 0.6 * math.exp(-0.3 * l)
        wl = w_in[l]
        w_cat = jnp.concatenate(
            [wl[:, z_end:z_end + d_inner], wl[:, :z_end], wl[:, z_end + d_inner:xbc_end],
             wl[:, dt_end:v_end]], axis=1).astype(BF16)
        w_dt = jnp.pad(wl[:, xbc_end:dt_end], ((0, 0), (0, LANES - n_ssm_heads))).astype(BF16)
        w_gate = wl[:, v_end:].astype(BF16)

        proj, dt = _in_proj(x2d, pos2d, norm_pre_mix[l][None, :], w_cat, w_dt, pad_heads(dt_bias[l]),
                            inv_freq_lane, j_q=col_q // tn, j_k=col_k // tn, tn=tn)

        y_ssm = _ssd(proj, dt,
                     conv_w[l][:, :d_inner], conv_b[l][None, :d_inner],
                     conv_w[l][:, d_inner:], conv_b[l][None, d_inner:],
                     pad_heads(a_log[l]), jnp.repeat(d_skip[l], SSM_HEAD_DIM)[None, :],
                     ssm_norm_w[l][None, :], e_mat, tri, shift,
                     batch=batch, seq=seq, d_inner=d_inner, col_xs=col_xs, col_z=col_z, col_bc=col_bc)

        y_att = _attention(proj, lam_q1[l][None, :], lam_k1[l][None, :], lam_q2[l][None, :],
                           lam_k2[l][None, :], attn_subln_w[l][:, None],
                           batch=batch, seq=seq, n_heads=n_att_heads,
                           col_q=col_q, col_k=col_k, col_v=col_v, lam_init=lam_init)

        x2d = _mix(x2d, y_ssm, y_att, norm_pre_mix[l][None, :], norm_post_mix[l][None, :],
                   w_gate, w_ssm_out[l].astype(BF16), w_attn_out[l].astype(BF16),
                   w_mix_out[l].astype(BF16))
        x2d = _ffn(x2d, norm_pre_ffn[l][None, :], norm_post_ffn[l][None, :],
                   w_ffn_gate[l].astype(BF16), w_ffn_up[l].astype(BF16), w_ffn_down[l].astype(BF16))
    return x2d.reshape(batch, seq, d_model)
```

```python
import functools
import math

import jax
import jax.numpy as jnp
from jax import lax
from jax.experimental import pallas as pl
from jax.experimental.pallas import tpu as pltpu

F32 = jnp.float32
BF16 = jnp.bfloat16

LANES = 128

SSM_HEAD_DIM = 64
SSM_N_GROUPS = 4
SSM_D_STATE = 128
SSM_CONV = 4
SSM_CHUNK = 128
CONV_CARRY = 16
ATT_HEAD_DIM = 64
ATT_V_DIM = 2 * ATT_HEAD_DIM
ROPE_THETA = 500000.0
ROPE_DIM = ATT_HEAD_DIM // 4
RMS_EPS = 1e-6
Q_SCALE = ATT_HEAD_DIM ** -0.5 * math.log2(math.e)

MIB = 1024 * 1024


def _rms(x, w):
    return x * lax.rsqrt(jnp.mean(x * x, axis=-1, keepdims=True) + RMS_EPS) * w


def _silu(x):
    return x * jax.nn.sigmoid(x)


def _dot(a, b):
    return jnp.dot(a, b, preferred_element_type=F32)


def _dot_nt(a, b):
    return lax.dot_general(a, b, (((1,), (1,)), ((), ())), preferred_element_type=F32)


def _split3(x):
    hi = x.astype(BF16)
    r = x - hi.astype(F32)
    mid = r.astype(BF16)
    lo = (r - mid.astype(F32)).astype(BF16)
    return hi, mid, lo


def _const_spec(shape):
    nd = len(shape)
    return pl.BlockSpec(shape, lambda *_: (0,) * nd, pipeline_mode=pl.Buffered(1))


def _inproj_kernel(x_ref, pos_ref, nw_ref, w_ref, wdt_ref, dtb_ref, invf_ref,
                   out_ref, dt_ref, h_scr, cos_scr, s1_scr, s2_scr, *, col_q, col_k, col_v, sub):
    hb = _rms(x_ref[...], nw_ref[...]).astype(BF16)
    h_scr[...] = hb
    dt_ref[...] = jax.nn.softplus(_dot(hb, wdt_ref[...]) + dtb_ref[...])
    ang = pos_ref[...].astype(F32) * invf_ref[...]
    lane = lax.broadcasted_iota(jnp.int32, ang.shape, 1) % ATT_HEAD_DIM
    sn = jnp.sin(ang)
    half = ROPE_DIM // 2
    cos_scr[...] = jnp.cos(ang)
    s1_scr[...] = jnp.where((lane >= half) & (lane < ROPE_DIM), sn, 0.0)
    s2_scr[...] = jnp.where(lane < half, -sn, 0.0)

    for c0 in range(0, w_ref.shape[1], sub):
        acc = _dot(h_scr[...], w_ref[:, c0:c0 + sub])
        if col_q <= c0 < col_v:
            scale = Q_SCALE if c0 < col_k else 1.0
            for c in range(c0, c0 + sub, LANES):
                a = acc[:, c - c0:c - c0 + LANES]
                r = (a * cos_scr[...] + pltpu.roll(a, half, 1) * s1_scr[...]
                     + pltpu.roll(a, LANES - half, 1) * s2_scr[...])
                out_ref[:, c:c + LANES] = (r * scale).astype(out_ref.dtype)
        else:
            out_ref[:, c0:c0 + sub] = acc.astype(out_ref.dtype)


def _in_proj(x2d, pos2d, norm_w, w_cat, w_dt, dt_bias, inv_freq_lane, *, col_q, col_k, col_v,
             tm=512, sub=512):
    t, d = x2d.shape
    n = w_cat.shape[1]
    assert all(c % sub == 0 for c in (col_q, col_k, col_v, n))
    return pl.pallas_call(
        functools.partial(_inproj_kernel, col_q=col_q, col_k=col_k, col_v=col_v, sub=sub),
        grid=(t // tm,),
        in_specs=[
            pl.BlockSpec((tm, d), lambda i: (i, 0)),
            pl.BlockSpec((tm, 1), lambda i: (i, 0)),
            _const_spec((1, d)),
            _const_spec((d, n)),
            _const_spec((d, LANES)),
            _const_spec((1, LANES)),
            _const_spec((1, LANES)),
        ],
        out_specs=[
            pl.BlockSpec((tm, n), lambda i: (i, 0)),
            pl.BlockSpec((tm, LANES), lambda i: (i, 0)),
        ],
        out_shape=[
            jax.ShapeDtypeStruct((t, n), BF16),
            jax.ShapeDtypeStruct((t, LANES), F32),
        ],
        scratch_shapes=[
            pltpu.VMEM((tm, d), BF16),
            pltpu.VMEM((tm, LANES), F32),
            pltpu.VMEM((tm, LANES), F32),
            pltpu.VMEM((tm, LANES), F32),
        ],
        compiler_params=pltpu.CompilerParams(
            dimension_semantics=("parallel",),
            vmem_limit_bytes=52 * MIB),
        name="in_proj",
    )(x2d, pos2d, norm_w, w_cat, w_dt, dt_bias, inv_freq_lane)


def _ssd_kernel(xs_ref, z_ref, bc_ref, dt_ref, cwx_ref, cbx_ref, cwbc_ref, cbbc_ref,
                alog_ref, dskip_ref, nw_ref, e_ref, tri_ref, shift_ref,
                y_ref, xprev, bcprev, hstate, ydiag, *, n_groups, d_state, head_dim, n_heads):
    c = pl.program_id(1)
    L = SSM_CHUNK
    d_inner = xs_ref.shape[1]
    gw = d_inner // n_groups

    @pl.when(c == 0)
    def _():
        xprev[...] = jnp.zeros(xprev.shape, BF16)
        bcprev[...] = jnp.zeros(bcprev.shape, BF16)
        hstate[...] = jnp.zeros(hstate.shape, F32)

    def conv_silu(cur_ref, prev, w_ref, b_ref):
        cur = cur_ref[...]
        old = prev[...]
        taps = [cur * w_ref[k:k + 1, :] for k in range(SSM_CONV)]
        carry = [old * w_ref[k:k + 1, :] for k in range(SSM_CONV - 1)]
        pad = jnp.zeros((L - len(carry) * old.shape[0], cur.shape[1]), BF16)
        acc = b_ref[...] + _dot(shift_ref[...], jnp.concatenate(taps + carry + [pad], axis=0))
        prev[...] = cur[L - old.shape[0]:L, :]
        return _silu(acc)

    xs = conv_silu(xs_ref, xprev, cwx_ref, cbx_ref)
    bc = conv_silu(bc_ref, bcprev, cwbc_ref, cbbc_ref).astype(BF16)

    dt = dt_ref[...]
    a_neg = -jnp.exp(alog_ref[...]) * math.log2(math.e)
    a_cs = _dot(tri_ref[...], jnp.concatenate(_split3(dt * a_neg), axis=0))
    a_cs_t = a_cs.T
    ea = jnp.exp2(a_cs)
    dte = jnp.exp2(a_cs[L - 1:L, :] - a_cs)
    stacked = jnp.concatenate([dt, ea, dte], axis=0)
    head_lane = lax.broadcasted_iota(jnp.int32, stacked.shape, 1)
    hi, mid, lo = _split3(jnp.where(head_lane < n_heads, stacked, 0.0))
    packed = (hi.astype(F32) + pltpu.roll(mid.astype(F32), n_heads, 1)
              + pltpu.roll(lo.astype(F32), 2 * n_heads, 1)).astype(BF16)
    expanded = _dot(packed, e_ref[...])
    dt_e = expanded[0:L]
    ea_e = expanded[L:2 * L]
    dte_e = expanded[2 * L:3 * L]

    xdt = xs * dt_e
    xdt_b = xdt.astype(BF16)
    xw_b = (xdt * dte_e).astype(BF16)

    row = lax.broadcasted_iota(jnp.int32, (L, L), 0)
    col = lax.broadcasted_iota(jnp.int32, (L, L), 1)
    causal = col <= row
    lane = lax.broadcasted_iota(jnp.int32, (L, LANES), 1)
    heads_per_group = gw // head_dim
    pairs_per_group = heads_per_group // 2

    y_off = []
    for g in range(n_groups):
        b_g = bc[:, g * d_state:(g + 1) * d_state]
        c_g = bc[:, (n_groups + g) * d_state:(n_groups + g + 1) * d_state]
        cb = _dot_nt(c_g, b_g)
        for p2 in range(pairs_per_group):
            pair = g * pairs_per_group + p2
            ms = []
            for h in (2 * pair, 2 * pair + 1):
                seg = a_cs[:, h:h + 1] - a_cs_t[h:h + 1, :]
                dec = jnp.exp2(jnp.where(causal, seg, -jnp.inf))
                ms.append((cb * dec).astype(BF16))
            lhs = jnp.concatenate(ms, axis=1)
            xp = xdt_b[:, pair * LANES:(pair + 1) * LANES]
            zero = jnp.zeros_like(xp)
            rhs = jnp.concatenate([jnp.where(lane < head_dim, xp, zero),
                                   jnp.where(lane >= head_dim, xp, zero)], axis=0)
            ydiag[:, pair * LANES:(pair + 1) * LANES] = _dot(lhs, rhs)
        h_g = hstate[g]
        ea_g = ea_e[:, g * gw:(g + 1) * gw]
        y_off.append(_dot(c_g, h_g.astype(BF16)) * ea_g)
        new = _dot(b_g.T, xw_b[:, g * gw:(g + 1) * gw])
        hstate[g] = h_g * ea_g[L - 1:L, :] + new

    zf = z_ref[...].astype(F32)
    gate = _silu(zf)
    for g in range(n_groups):
        sl = slice(g * gw, (g + 1) * gw)
        yg = (ydiag[:, sl] + y_off[g] + xs[:, sl] * dskip_ref[:, sl]) * gate[:, sl]
        y_ref[:, sl] = _rms(yg, nw_ref[:, sl]).astype(y_ref.dtype)


def _ssd(proj, dt, cw_x, cb_x, cw_bc, cb_bc, a_log, d_skip_e, norm_w, e_mat, tri, shift,
         *, batch, seq, d_inner, col_xs, col_z, col_bc):
    L = SSM_CHUNK
    nc = seq // L
    t = batch * seq
    bc_w = 2 * SSM_N_GROUPS * SSM_D_STATE
    row = lambda b, c: b * nc + c
    kern = functools.partial(_ssd_kernel, n_groups=SSM_N_GROUPS, d_state=SSM_D_STATE,
                             head_dim=SSM_HEAD_DIM, n_heads=d_inner // SSM_HEAD_DIM)
    return pl.pallas_call(
        kern,
        grid=(batch, nc),
        in_specs=[
            pl.BlockSpec((L, d_inner), lambda b, c: (row(b, c), col_xs // d_inner)),
            pl.BlockSpec((L, d_inner), lambda b, c: (row(b, c), col_z // d_inner)),
            pl.BlockSpec((L, bc_w), lambda b, c: (row(b, c), col_bc // bc_w)),
            pl.BlockSpec((L, LANES), lambda b, c: (row(b, c), 0)),
            pl.BlockSpec((SSM_CONV, d_inner), lambda b, c: (0, 0)),
            pl.BlockSpec((1, d_inner), lambda b, c: (0, 0)),
            pl.BlockSpec((SSM_CONV, bc_w), lambda b, c: (0, 0)),
            pl.BlockSpec((1, bc_w), lambda b, c: (0, 0)),
            pl.BlockSpec((1, LANES), lambda b, c: (0, 0)),
            pl.BlockSpec((1, d_inner), lambda b, c: (0, 0)),
            pl.BlockSpec((1, d_inner), lambda b, c: (0, 0)),
            pl.BlockSpec((LANES, d_inner), lambda b, c: (0, 0)),
            pl.BlockSpec((L, 3 * L), lambda b, c: (0, 0)),
            pl.BlockSpec(shift.shape, lambda b, c: (0, 0)),
        ],
        out_specs=pl.BlockSpec((L, d_inner), lambda b, c: (row(b, c), 0)),
        out_shape=jax.ShapeDtypeStruct((t, d_inner), BF16),
        scratch_shapes=[
            pltpu.VMEM((CONV_CARRY, d_inner), BF16),
            pltpu.VMEM((CONV_CARRY, bc_w), BF16),
            pltpu.VMEM((SSM_N_GROUPS, SSM_D_STATE, d_inner // SSM_N_GROUPS), F32),
            pltpu.VMEM((L, d_inner), F32),
        ],
        compiler_params=pltpu.CompilerParams(
            dimension_semantics=("parallel", "arbitrary"),
            vmem_limit_bytes=40 * MIB),
        name="ssd",
    )(proj, proj, proj, dt, cw_x, cb_x, cw_bc, cb_bc, a_log, d_skip_e, norm_w, e_mat, tri, shift)


def _attn_kernel(q_ref, k_ref, v_ref, lq1_ref, lk1_ref, lq2_ref, lk2_ref, subw_ref,
                 o_ref, vt_scr, *, tq, lam_init):
    nq = vt_scr.shape[0]
    hd = v_ref.shape[1]
    for jb in range(nq):
        vt_scr[jb, 0:hd, :] = v_ref[jb * tq:(jb + 1) * tq, :].T
        vt_scr[jb, hd:, :] = jnp.ones((vt_scr.shape[1] - hd, tq), BF16)

    lam = (jnp.exp(jnp.sum(lq1_ref[...] * lk1_ref[...]))
           - jnp.exp(jnp.sum(lq2_ref[...] * lk2_ref[...])) + lam_init)
    subw = subw_ref[...] * (1.0 - lam_init)

    half = tq // 2
    steps = []
    for qi in range(nq):
        steps += [(qi, j * tq, tq, 0, False) for j in range(qi)]
        steps += [(qi, qi * tq, half, 0, True), (qi, qi * tq + half, half, half, True)]

    def pick(a, q0):
        return a if q0 == 0 else jnp.concatenate([a[:, q0:tq], a[:, tq + q0:2 * tq]], axis=1)

    def put(a, new, q0):
        if q0 == 0:
            return new
        w = tq - q0
        return jnp.concatenate([a[:, 0:q0], new[:, 0:w], a[:, tq:tq + q0], new[:, w:2 * w]], axis=1)

    def scores(qi, kv0, kvn, q0, masked):
        q = q_ref[qi * tq + q0:(qi + 1) * tq, :]
        lane = lax.broadcasted_iota(jnp.int32, q.shape, 1)
        zero = jnp.zeros_like(q)
        q2 = jnp.concatenate([jnp.where(lane < ATT_HEAD_DIM, q, zero),
                              jnp.where(lane >= ATT_HEAD_DIM, q, zero)], axis=0)
        return _dot_nt(k_ref[kv0:kv0 + kvn, :], q2)

    s_next = scores(*steps[0])
    m = acc = None
    for t, (qi, kv0, kvn, q0, masked) in enumerate(steps):
        s = s_next
        if t + 1 < len(steps):
            s_next = scores(*steps[t + 1])
        if masked:
            w = tq - q0
            kv = lax.broadcasted_iota(jnp.int32, s.shape, 0) + (kv0 - qi * tq)
            r = lax.broadcasted_iota(jnp.int32, s.shape, 1)
            r = jnp.where(r >= w, r - w, r) + q0
            s = jnp.where(kv <= r, s, -jnp.inf)
        vt = vt_scr[kv0 // tq][:, kv0 % tq:kv0 % tq + kvn]
        s_max = jnp.max(s, axis=0, keepdims=True)
        if kv0 == 0:
            m = s_max
            acc = _dot(vt, jnp.exp2(s - m).astype(BF16))
        else:
            m_old = pick(m, q0)
            m_new = jnp.maximum(m_old, s_max)
            alpha = jnp.exp2(m_old - m_new)
            p = jnp.exp2(s - m_new).astype(BF16)
            acc = put(acc, alpha * pick(acc, q0) + _dot(vt, p), q0)
            m = put(m, m_new, q0)
        if kv0 + kvn == (qi + 1) * tq:
            o = acc[0:hd] / acc[hd:hd + 1]
            od = o[:, 0:tq] - lam * o[:, tq:2 * tq]
            od = od * lax.rsqrt(jnp.mean(od * od, axis=0, keepdims=True) + RMS_EPS)
            o_ref[qi * tq:(qi + 1) * tq, :] = (od * subw).T.astype(o_ref.dtype)


def _attention(proj, lq1, lk1, lq2, lk2, subw, *, batch, seq, n_heads, col_q, col_k, col_v,
               lam_init, tq=512):
    nq = seq // tq
    t = batch * seq
    hd = ATT_V_DIM
    kern = functools.partial(_attn_kernel, tq=tq, lam_init=lam_init)
    small = pl.BlockSpec((1, ATT_HEAD_DIM), lambda b, h: (0, 0))
    return pl.pallas_call(
        kern,
        grid=(batch, n_heads),
        in_specs=[
            pl.BlockSpec((seq, hd), lambda b, h: (b, col_q // hd + h)),
            pl.BlockSpec((seq, hd), lambda b, h: (b, col_k // hd + h)),
            pl.BlockSpec((seq, hd), lambda b, h: (b, col_v // hd + h)),
            small, small, small, small,
            pl.BlockSpec((hd, 1), lambda b, h: (0, 0)),
        ],
        out_specs=pl.BlockSpec((seq, hd), lambda b, h: (b, h)),
        out_shape=jax.ShapeDtypeStruct((t, n_heads * hd), BF16),
        scratch_shapes=[pltpu.VMEM((nq, hd + 16, tq), BF16)],
        compiler_params=pltpu.CompilerParams(
            dimension_semantics=("parallel", "parallel"),
            vmem_limit_bytes=32 * MIB),
        name="diff_attn",
    )(proj, proj, proj, lq1, lk1, lq2, lk2, subw)


def _mix_kernel(x_ref, ys_ref, ya_ref, npre_ref, npost_ref, wg_ref, wso_ref, wao_ref, wmix_ref, o_ref):
    x = x_ref[...]
    d = x.shape[1]
    h = _rms(x, npre_ref[...]).astype(BF16)
    gates = jax.nn.sigmoid(_dot(h, wg_ref[...]))
    y_ssm = _dot(ys_ref[...], wso_ref[...])
    y_att = _dot(ya_ref[...], wao_ref[...])
    blend = gates[:, 0:d] * y_ssm + gates[:, d:2 * d] * y_att
    mixed = _dot(blend.astype(BF16), wmix_ref[...])
    o_ref[...] = x + _rms(mixed, npost_ref[...])


def _mix(x2d, y_ssm, y_att, n_pre, n_post, w_gate, w_so, w_ao, w_mix, *, tm=512):
    t, d = x2d.shape
    rows = lambda w: pl.BlockSpec((tm, w), lambda i: (i, 0))
    return pl.pallas_call(
        _mix_kernel,
        grid=(t // tm,),
        in_specs=[rows(d), rows(y_ssm.shape[1]), rows(y_att.shape[1]),
                  _const_spec(n_pre.shape), _const_spec(n_post.shape),
                  _const_spec(w_gate.shape), _const_spec(w_so.shape),
                  _const_spec(w_ao.shape), _const_spec(w_mix.shape)],
        out_specs=rows(d),
        out_shape=jax.ShapeDtypeStruct((t, d), F32),
        compiler_params=pltpu.CompilerParams(
            dimension_semantics=("parallel",), vmem_limit_bytes=48 * MIB),
        name="mix",
    )(x2d, y_ssm, y_att, n_pre, n_post, w_gate, w_so, w_ao, w_mix)


def _ffn_kernel(x_ref, npre_ref, npost_ref, wg_ref, wu_ref, wd_ref, o_ref):
    x = x_ref[...]
    h = _rms(x, npre_ref[...]).astype(BF16)
    act = (_silu(_dot(h, wg_ref[...])) * _dot(h, wu_ref[...])).astype(BF16)
    f = _dot(act, wd_ref[...])
    o_ref[...] = x + _rms(f, npost_ref[...])


def _ffn(x2d, n_pre, n_post, w_g, w_u, w_d, *, tm=512):
    t, d = x2d.shape
    rows = pl.BlockSpec((tm, d), lambda i: (i, 0))
    return pl.pallas_call(
        _ffn_kernel,
        grid=(t // tm,),
        in_specs=[rows, _const_spec(n_pre.shape), _const_spec(n_post.shape),
                  _const_spec(w_g.shape), _const_spec(w_u.shape), _const_spec(w_d.shape)],
        out_specs=rows,
        out_shape=jax.ShapeDtypeStruct((t, d), F32),
        compiler_params=pltpu.CompilerParams(
            dimension_semantics=("parallel",), vmem_limit_bytes=56 * MIB),
        name="ffn",
    )(x2d, n_pre, n_post, w_g, w_u, w_d)


def kernel(x, positions, w_in, conv_w, conv_b, dt_bias, a_log, d_skip, ssm_norm_w, w_ssm_out, lam_q1, lam_k1, lam_q2, lam_k2, attn_subln_w, w_attn_out, w_mix_out, norm_pre_mix, norm_post_mix, norm_pre_ffn, norm_post_ffn, w_ffn_gate, w_ffn_up, w_ffn_down):
    batch, seq, d_model = x.shape
    depth = w_in.shape[0]
    d_inner = w_ssm_out.shape[1]
    n_ssm_heads = dt_bias.shape[1]
    bc_w = 2 * SSM_N_GROUPS * SSM_D_STATE
    att_w = w_attn_out.shape[1]
    n_att_heads = att_w // ATT_V_DIM
    t = batch * seq

    z_end = d_inner
    xbc_end = z_end + d_inner + bc_w
    dt_end = xbc_end + n_ssm_heads
    q_end = dt_end + att_w
    k_end = q_end + att_w
    v_end = k_end + att_w

    col_xs, col_z = 0, d_inner
    col_bc = 2 * d_inner
    col_q = col_bc + bc_w
    col_k = col_q + att_w
    col_v = col_k + att_w

    inv_freq = ROPE_THETA ** (-jnp.arange(0, ROPE_DIM, 2, dtype=F32) / ROPE_DIM)
    lane = jnp.arange(LANES) % ATT_HEAD_DIM
    inv_freq_lane = jnp.where(lane < ROPE_DIM, inv_freq[lane % (ROPE_DIM // 2)], 0.0)[None, :].astype(F32)
    head_of_col = jnp.arange(d_inner) // SSM_HEAD_DIM
    assert 3 * n_ssm_heads <= LANES
    term_row = jnp.arange(LANES)
    e_mat = ((term_row[:, None] % n_ssm_heads == head_of_col[None, :])
             & (term_row[:, None] < 3 * n_ssm_heads)).astype(BF16)
    tri = (jnp.arange(SSM_CHUNK)[None, :] <= jnp.arange(SSM_CHUNK)[:, None]).astype(BF16)
    tri = jnp.concatenate([tri, tri, tri], axis=1)
    L = SSM_CHUNK
    t_out = jnp.arange(L)[:, None, None]
    tap = jnp.arange(SSM_CONV)[None, :, None]
    src = t_out - (SSM_CONV - 1) + tap
    src_col = jnp.where(src >= 0, tap * L + src, SSM_CONV * L + tap * CONV_CARRY + CONV_CARRY + src)
    shift = (jnp.arange((SSM_CONV + 1) * L)[None, None, :] == src_col).any(axis=1).astype(BF16)

    def pad_heads(v):
        return jnp.pad(v.astype(F32), (0, LANES - n_ssm_heads))[None, :]

    x2d = x.reshape(t, d_model)
    pos2d = positions.reshape(t, 1)
    for l in range(depth):
        lam_init = 0.8 - 0.6 * math.exp(-0.3 * l)
        wl = w_in[l]
        w_cat = jnp.concatenate(
            [wl[:, z_end:z_end + d_inner], wl[:, :z_end], wl[:, z_end + d_inner:xbc_end],
             wl[:, dt_end:v_end]], axis=1).astype(BF16)
        w_dt = jnp.pad(wl[:, xbc_end:dt_end], ((0, 0), (0, LANES - n_ssm_heads))).astype(BF16)
        w_gate = wl[:, v_end:].astype(BF16)

        proj, dt = _in_proj(x2d, pos2d, norm_pre_mix[l][None, :], w_cat, w_dt, pad_heads(dt_bias[l]),
                            inv_freq_lane, col_q=col_q, col_k=col_k, col_v=col_v)

        cw = conv_w[l].astype(BF16)
        y_ssm = _ssd(proj, dt,
                     cw[:, :d_inner], conv_b[l][None, :d_inner],
                     cw[:, d_inner:], conv_b[l][None, d_inner:],
                     pad_heads(a_log[l]), jnp.repeat(d_skip[l], SSM_HEAD_DIM)[None, :],
                     ssm_norm_w[l][None, :], e_mat, tri, shift,
                     batch=batch, seq=seq, d_inner=d_inner, col_xs=col_xs, col_z=col_z, col_bc=col_bc)

        y_att = _attention(proj, lam_q1[l][None, :], lam_k1[l][None, :], lam_q2[l][None, :],
                           lam_k2[l][None, :], attn_subln_w[l][:, None],
                           batch=batch, seq=seq, n_heads=n_att_heads,
                           col_q=col_q, col_k=col_k, col_v=col_v, lam_init=lam_init)

        x2d = _mix(x2d, y_ssm, y_att, norm_pre_mix[l][None, :], norm_post_mix[l][None, :],
                   w_gate, w_ssm_out[l].astype(BF16), w_attn_out[l].astype(BF16),
                   w_mix_out[l].astype(BF16))
        x2d = _ffn(x2d, norm_pre_ffn[l][None, :], norm_post_ffn[l][None, :],
                   w_ffn_gate[l].astype(BF16), w_ffn_up[l].astype(BF16), w_ffn_down[l].astype(BF16))
    return x2d.reshape(batch, seq, d_model)
```

```python
import functools
import math

import jax
import jax.numpy as jnp
from jax import lax
from jax.experimental import pallas as pl
from jax.experimental.pallas import tpu as pltpu

F32 = jnp.float32
BF16 = jnp.bfloat16

LANES = 128

SSM_HEAD_DIM = 64
SSM_N_GROUPS = 4
SSM_D_STATE = 128
SSM_CONV = 4
SSM_CHUNK = 128
CONV_CARRY = 16
ATT_HEAD_DIM = 64
ATT_V_DIM = 2 * ATT_HEAD_DIM
ROPE_THETA = 500000.0
ROPE_DIM = ATT_HEAD_DIM // 4
RMS_EPS = 1e-6
Q_SCALE = ATT_HEAD_DIM ** -0.5 * math.log2(math.e)

MIB = 1024 * 1024


def _rms(x, w):
    return x * lax.rsqrt(jnp.mean(x * x, axis=-1, keepdims=True) + RMS_EPS) * w


def _silu(x):
    return x * jax.nn.sigmoid(x)


def _dot(a, b):
    return jnp.dot(a, b, preferred_element_type=F32)


def _dot_nt(a, b):
    return lax.dot_general(a, b, (((1,), (1,)), ((), ())), preferred_element_type=F32)


def _split3(x):
    hi = x.astype(BF16)
    r = x - hi.astype(F32)
    mid = r.astype(BF16)
    lo = (r - mid.astype(F32)).astype(BF16)
    return hi, mid, lo


def _const_spec(shape):
    nd = len(shape)
    return pl.BlockSpec(shape, lambda *_: (0,) * nd, pipeline_mode=pl.Buffered(1))


def _inproj_kernel(x_ref, pos_ref, nw_ref, wa_ref, wb_ref, wdt_ref, dtb_ref, invf_ref,
                   out_ref, dt_ref, h_scr, cos_scr, s1_scr, s2_scr, *, col_q, col_k, col_v, sub):
    hb = _rms(x_ref[...], nw_ref[...]).astype(BF16)
    h_scr[...] = hb
    dt_ref[...] = jax.nn.softplus(_dot(hb, wdt_ref[...]) + dtb_ref[...])
    ang = pos_ref[...].astype(F32) * invf_ref[...]
    lane = lax.broadcasted_iota(jnp.int32, ang.shape, 1) % ATT_HEAD_DIM
    sn = jnp.sin(ang)
    half = ROPE_DIM // 2
    cos_scr[...] = jnp.cos(ang)
    s1_scr[...] = jnp.where((lane >= half) & (lane < ROPE_DIM), sn, 0.0)
    s2_scr[...] = jnp.where(lane < half, -sn, 0.0)

    for c0 in range(0, out_ref.shape[1], sub):
        w_ref, w0 = (wa_ref, c0) if c0 < col_q else (wb_ref, c0 - col_q)
        acc = _dot(h_scr[...], w_ref[:, w0:w0 + sub])
        if col_q <= c0 < col_v:
            scale = Q_SCALE if c0 < col_k else 1.0
            for c in range(c0, c0 + sub, LANES):
                a = acc[:, c - c0:c - c0 + LANES]
                r = (a * cos_scr[...] + pltpu.roll(a, half, 1) * s1_scr[...]
                     + pltpu.roll(a, LANES - half, 1) * s2_scr[...])
                out_ref[:, c:c + LANES] = (r * scale).astype(out_ref.dtype)
        else:
            out_ref[:, c0:c0 + sub] = acc.astype(out_ref.dtype)


def _in_proj(x2d, pos2d, norm_w, w_a, w_b, w_dt, dt_bias, inv_freq_lane, *, col_q, col_k, col_v,
             tm=512, sub=512):
    t, d = x2d.shape
    n = w_a.shape[1] + w_b.shape[1]
    assert w_a.shape[1] == col_q and all(c % sub == 0 for c in (col_q, col_k, col_v, n))
    return pl.pallas_call(
        functools.partial(_inproj_kernel, col_q=col_q, col_k=col_k, col_v=col_v, sub=sub),
        grid=(t // tm,),
        in_specs=[
            pl.BlockSpec((tm, d), lambda i: (i, 0)),
            pl.BlockSpec((tm, 1), lambda i: (i, 0)),
            _const_spec((1, d)),
            _const_spec(w_a.shape),
            _const_spec(w_b.shape),
            _const_spec((d, LANES)),
            _const_spec((1, LANES)),
            _const_spec((1, LANES)),
        ],
        out_specs=[
            pl.BlockSpec((tm, n), lambda i: (i, 0)),
            pl.BlockSpec((tm, LANES), lambda i: (i, 0)),
        ],
        out_shape=[
            jax.ShapeDtypeStruct((t, n), BF16),
            jax.ShapeDtypeStruct((t, LANES), F32),
        ],
        scratch_shapes=[
            pltpu.VMEM((tm, d), BF16),
            pltpu.VMEM((tm, LANES), F32),
            pltpu.VMEM((tm, LANES), F32),
            pltpu.VMEM((tm, LANES), F32),
        ],
        compiler_params=pltpu.CompilerParams(
            dimension_semantics=("parallel",),
            vmem_limit_bytes=52 * MIB),
        name="in_proj",
    )(x2d, pos2d, norm_w, w_a, w_b, w_dt, dt_bias, inv_freq_lane)


def _ssd_kernel(xs_ref, z_ref, bc_ref, dt_ref, cwx_ref, cbx_ref, cwbc_ref, cbbc_ref,
                alog_ref, dskip_ref, nw_ref, e_ref, tri_ref, shift_ref,
                y_ref, xprev, bcprev, hstate, *, n_groups, d_state, head_dim, chunks):
    L = SSM_CHUNK
    d_inner = xs_ref.shape[1]
    gw = d_inner // n_groups
    heads_per_group = gw // head_dim
    pairs_per_group = heads_per_group // 2

    @pl.when(pl.program_id(1) == 0)
    def _():
        xprev[...] = jnp.zeros(xprev.shape, BF16)
        bcprev[...] = jnp.zeros(bcprev.shape, BF16)
        hstate[...] = jnp.zeros(hstate.shape, F32)

    a_neg = -jnp.exp(alog_ref[...]) * math.log2(math.e)
    row = lax.broadcasted_iota(jnp.int32, (L, L), 0)
    col = lax.broadcasted_iota(jnp.int32, (L, L), 1)
    causal = col <= row
    lane = lax.broadcasted_iota(jnp.int32, (L, LANES), 1)

    def conv_silu(cur_ref, prev, w_ref, b_ref, r0):
        cur = cur_ref[r0:r0 + L, :]
        old = prev[...] if r0 == 0 else cur_ref[r0 - CONV_CARRY:r0, :]
        taps = [cur * w_ref[k:k + 1, :] for k in range(SSM_CONV)]
        carry = [old * w_ref[k:k + 1, :] for k in range(SSM_CONV - 1)]
        pad = jnp.zeros((L - len(carry) * CONV_CARRY, cur.shape[1]), BF16)
        acc = b_ref[...] + _dot(shift_ref[...], jnp.concatenate(taps + carry + [pad], axis=0))
        return _silu(acc)

    h = [hstate[g] for g in range(n_groups)]
    for ci in range(chunks):
        r0 = ci * L
        xs = conv_silu(xs_ref, xprev, cwx_ref, cbx_ref, r0)
        bc = conv_silu(bc_ref, bcprev, cwbc_ref, cbbc_ref, r0).astype(BF16)

        dt = dt_ref[r0:r0 + L, :]
        a_cs = _dot(tri_ref[...], jnp.concatenate(_split3(dt * a_neg), axis=0))
        a_cs_t = a_cs.T
        ea = jnp.exp2(a_cs)
        dte = jnp.exp2(a_cs[L - 1:L, :] - a_cs)
        stacked = jnp.concatenate([dt, ea, dte], axis=0).astype(BF16)
        expanded = _dot(stacked, e_ref[...])
        dt_e = expanded[0:L]
        ea_e = expanded[L:2 * L]
        dte_e = expanded[2 * L:3 * L]

        xdt = xs * dt_e
        xdt_b = xdt.astype(BF16)
        xw_b = (xdt * dte_e).astype(BF16)
        gate = _silu(z_ref[r0:r0 + L, :].astype(F32))

        for g in range(n_groups):
            b_g = bc[:, g * d_state:(g + 1) * d_state]
            c_g = bc[:, (n_groups + g) * d_state:(n_groups + g + 1) * d_state]
            cb = _dot_nt(c_g, b_g)
            y_diag = []
            for p2 in range(pairs_per_group):
                pair = g * pairs_per_group + p2
                ms = []
                for hh in (2 * pair, 2 * pair + 1):
                    seg = a_cs[:, hh:hh + 1] - a_cs_t[hh:hh + 1, :]
                    dec = jnp.exp2(jnp.where(causal, seg, -jnp.inf))
                    ms.append((cb * dec).astype(BF16))
                lhs = jnp.concatenate(ms, axis=1)
                xp = xdt_b[:, pair * LANES:(pair + 1) * LANES]
                zero = jnp.zeros_like(xp)
                rhs = jnp.concatenate([jnp.where(lane < head_dim, xp, zero),
                                       jnp.where(lane >= head_dim, xp, zero)], axis=0)
                y_diag.append(_dot(lhs, rhs))
            sl = slice(g * gw, (g + 1) * gw)
            ea_g = ea_e[:, sl]
            y_off = _dot(c_g, h[g].astype(BF16)) * ea_g
            h[g] = h[g] * ea_g[L - 1:L, :] + _dot(b_g.T, xw_b[:, sl])
            yg = (jnp.concatenate(y_diag, axis=1) + y_off + xs[:, sl] * dskip_ref[:, sl]) * gate[:, sl]
            y_ref[r0:r0 + L, sl] = _rms(yg, nw_ref[:, sl]).astype(y_ref.dtype)

    for g in range(n_groups):
        hstate[g] = h[g]
    xprev[...] = xs_ref[chunks * L - CONV_CARRY:chunks * L, :]
    bcprev[...] = bc_ref[chunks * L - CONV_CARRY:chunks * L, :]


def _ssd(proj, dt, cw_x, cb_x, cw_bc, cb_bc, a_log, d_skip_e, norm_w, e_mat, tri, shift,
         *, batch, seq, d_inner, col_xs, col_z, col_bc, chunks=2):
    L = SSM_CHUNK
    rows = chunks * L
    nc = seq // rows
    t = batch * seq
    bc_w = 2 * SSM_N_GROUPS * SSM_D_STATE
    row = lambda b, c: b * nc + c
    kern = functools.partial(_ssd_kernel, n_groups=SSM_N_GROUPS, d_state=SSM_D_STATE,
                             head_dim=SSM_HEAD_DIM, chunks=chunks)
    return pl.pallas_call(
        kern,
        grid=(batch, nc),
        in_specs=[
            pl.BlockSpec((rows, d_inner), lambda b, c: (row(b, c), col_xs // d_inner)),
            pl.BlockSpec((rows, d_inner), lambda b, c: (row(b, c), col_z // d_inner)),
            pl.BlockSpec((rows, bc_w), lambda b, c: (row(b, c), col_bc // bc_w)),
            pl.BlockSpec((rows, LANES), lambda b, c: (row(b, c), 0)),
            pl.BlockSpec((SSM_CONV, d_inner), lambda b, c: (0, 0)),
            pl.BlockSpec((1, d_inner), lambda b, c: (0, 0)),
            pl.BlockSpec((SSM_CONV, bc_w), lambda b, c: (0, 0)),
            pl.BlockSpec((1, bc_w), lambda b, c: (0, 0)),
            pl.BlockSpec((1, LANES), lambda b, c: (0, 0)),
            pl.BlockSpec((1, d_inner), lambda b, c: (0, 0)),
            pl.BlockSpec((1, d_inner), lambda b, c: (0, 0)),
            pl.BlockSpec((LANES, d_inner), lambda b, c: (0, 0)),
            pl.BlockSpec((L, 3 * L), lambda b, c: (0, 0)),
            pl.BlockSpec(shift.shape, lambda b, c: (0, 0)),
        ],
        out_specs=pl.BlockSpec((rows, d_inner), lambda b, c: (row(b, c), 0)),
        out_shape=jax.ShapeDtypeStruct((t, d_inner), BF16),
        scratch_shapes=[
            pltpu.VMEM((CONV_CARRY, d_inner), BF16),
            pltpu.VMEM((CONV_CARRY, bc_w), BF16),
            pltpu.VMEM((SSM_N_GROUPS, SSM_D_STATE, d_inner // SSM_N_GROUPS), F32),
        ],
        compiler_params=pltpu.CompilerParams(
            dimension_semantics=("parallel", "arbitrary"),
            vmem_limit_bytes=40 * MIB),
        name="ssd",
    )(proj, proj, proj, dt, cw_x, cb_x, cw_bc, cb_bc, a_log, d_skip_e, norm_w, e_mat, tri, shift)


def _attn_kernel(q_ref, k_ref, v_ref, lq1_ref, lk1_ref, lq2_ref, lk2_ref, subw_ref,
                 o_ref, vt_scr, *, tq, tk, lam_init):
    seq = q_ref.shape[0]
    for jb in range(seq // tk):
        vt_scr[jb] = v_ref[jb * tk:(jb + 1) * tk, :].T

    lam = (jnp.exp(jnp.sum(lq1_ref[...] * lk1_ref[...]))
           - jnp.exp(jnp.sum(lq2_ref[...] * lk2_ref[...])) + lam_init)
    subw = subw_ref[...] * (1.0 - lam_init)

    def scores(qi, j):
        q = q_ref[qi * tq:(qi + 1) * tq, :]
        lane = lax.broadcasted_iota(jnp.int32, q.shape, 1)
        zero = jnp.zeros_like(q)
        q2 = jnp.concatenate([jnp.where(lane < ATT_HEAD_DIM, q, zero),
                              jnp.where(lane >= ATT_HEAD_DIM, q, zero)], axis=0)
        return _dot_nt(k_ref[j * tk:(j + 1) * tk, :], q2)

    steps = [(qi, j) for qi in range(seq // tq) for j in range((qi + 1) * tq // tk)]
    s_next = scores(*steps[0])
    m = l = acc = None
    for t, (qi, j) in enumerate(steps):
        s = s_next
        if t + 1 < len(steps):
            s_next = scores(*steps[t + 1])
        if (j + 1) * tk > qi * tq:
            kv = lax.broadcasted_iota(jnp.int32, s.shape, 0) + (j * tk - qi * tq)
            r = lax.broadcasted_iota(jnp.int32, s.shape, 1)
            r = jnp.where(r >= tq, r - tq, r)
            s = jnp.where(kv <= r, s, -jnp.inf)
        s_max = jnp.max(s, axis=0, keepdims=True)
        if j == 0:
            m = s_max
            p = jnp.exp2(s - m)
            l = jnp.sum(p, axis=0, keepdims=True)
            acc = _dot(vt_scr[j], p.astype(BF16))
        else:
            m_new = jnp.maximum(m, s_max)
            alpha = jnp.exp2(m - m_new)
            p = jnp.exp2(s - m_new)
            l = alpha * l + jnp.sum(p, axis=0, keepdims=True)
            acc = alpha * acc + _dot(vt_scr[j], p.astype(BF16))
            m = m_new
        if (j + 1) * tk == (qi + 1) * tq:
            o = acc / l
            od = o[:, 0:tq] - lam * o[:, tq:2 * tq]
            od = od * lax.rsqrt(jnp.mean(od * od, axis=0, keepdims=True) + RMS_EPS)
            o_ref[qi * tq:(qi + 1) * tq, :] = (od * subw).T.astype(o_ref.dtype)


def _attention(proj, lq1, lk1, lq2, lk2, subw, *, batch, seq, n_heads, col_q, col_k, col_v,
               lam_init, tq=512, tk=512):
    t = batch * seq
    hd = ATT_V_DIM
    kern = functools.partial(_attn_kernel, tq=tq, tk=tk, lam_init=lam_init)
    small = pl.BlockSpec((1, ATT_HEAD_DIM), lambda b, h: (0, 0))
    return pl.pallas_call(
        kern,
        grid=(batch, n_heads),
        in_specs=[
            pl.BlockSpec((seq, hd), lambda b, h: (b, col_q // hd + h)),
            pl.BlockSpec((seq, hd), lambda b, h: (b, col_k // hd + h)),
            pl.BlockSpec((seq, hd), lambda b, h: (b, col_v // hd + h)),
            small, small, small, small,
            pl.BlockSpec((hd, 1), lambda b, h: (0, 0)),
        ],
        out_specs=pl.BlockSpec((seq, hd), lambda b, h: (b, h)),
        out_shape=jax.ShapeDtypeStruct((t, n_heads * hd), BF16),
        scratch_shapes=[pltpu.VMEM((seq // tk, hd, tk), BF16)],
        compiler_params=pltpu.CompilerParams(
            dimension_semantics=("parallel", "parallel"),
            vmem_limit_bytes=32 * MIB),
        name="diff_attn",
    )(proj, proj, proj, lq1, lk1, lq2, lk2, subw)


def _mix_kernel(x_ref, ys_ref, ya_ref, npre_ref, npost_ref, wg_ref, wso_ref, wao_ref, wmix_ref, o_ref):
    x = x_ref[...]
    d = x.shape[1]
    h = _rms(x, npre_ref[...]).astype(BF16)
    gates = jax.nn.sigmoid(_dot(h, wg_ref[...]))
    y_ssm = _dot(ys_ref[...], wso_ref[...])
    y_att = _dot(ya_ref[...], wao_ref[...])
    blend = gates[:, 0:d] * y_ssm + gates[:, d:2 * d] * y_att
    mixed = _dot(blend.astype(BF16), wmix_ref[...])
    o_ref[...] = x + _rms(mixed, npost_ref[...])


def _mix(x2d, y_ssm, y_att, n_pre, n_post, w_gate, w_so, w_ao, w_mix, *, tm=512):
    t, d = x2d.shape
    rows = lambda w: pl.BlockSpec((tm, w), lambda i: (i, 0))
    return pl.pallas_call(
        _mix_kernel,
        grid=(t // tm,),
        in_specs=[rows(d), rows(y_ssm.shape[1]), rows(y_att.shape[1]),
                  _const_spec(n_pre.shape), _const_spec(n_post.shape),
                  _const_spec(w_gate.shape), _const_spec(w_so.shape),
                  _const_spec(w_ao.shape), _const_spec(w_mix.shape)],
        out_specs=rows(d),
        out_shape=jax.ShapeDtypeStruct((t, d), F32),
        compiler_params=pltpu.CompilerParams(
            dimension_semantics=("parallel",), vmem_limit_bytes=48 * MIB),
        name="mix",
    )(x2d, y_ssm, y_att, n_pre, n_post, w_gate, w_so, w_ao, w_mix)


def _ffn_kernel(x_ref, npre_ref, npost_ref, wg_ref, wu_ref, wd_ref, o_ref):
    x = x_ref[...]
    h = _rms(x, npre_ref[...]).astype(BF16)
    act = (_silu(_dot(h, wg_ref[...])) * _dot(h, wu_ref[...])).astype(BF16)
    f = _dot(act, wd_ref[...])
    o_ref[...] = x + _rms(f, npost_ref[...])


def _ffn(x2d, n_pre, n_post, w_g, w_u, w_d, *, tm=512):
    t, d = x2d.shape
    rows = pl.BlockSpec((tm, d), lambda i: (i, 0))
    return pl.pallas_call(
        _ffn_kernel,
        grid=(t // tm,),
        in_specs=[rows, _const_spec(n_pre.shape), _const_spec(n_post.shape),
                  _const_spec(w_g.shape), _const_spec(w_u.shape), _const_spec(w_d.shape)],
        out_specs=rows,
        out_shape=jax.ShapeDtypeStruct((t, d), F32),
        compiler_params=pltpu.CompilerParams(
            dimension_semantics=("parallel",), vmem_limit_bytes=56 * MIB),
        name="ffn",
    )(x2d, n_pre, n_post, w_g, w_u, w_d)


def kernel(x, positions, w_in, conv_w, conv_b, dt_bias, a_log, d_skip, ssm_norm_w, w_ssm_out, lam_q1, lam_k1, lam_q2, lam_k2, attn_subln_w, w_attn_out, w_mix_out, norm_pre_mix, norm_post_mix, norm_pre_ffn, norm_post_ffn, w_ffn_gate, w_ffn_up, w_ffn_down):
    batch, seq, d_model = x.shape
    depth = w_in.shape[0]
    d_inner = w_ssm_out.shape[1]
    n_ssm_heads = dt_bias.shape[1]
    bc_w = 2 * SSM_N_GROUPS * SSM_D_STATE
    att_w = w_attn_out.shape[1]
    n_att_heads = att_w // ATT_V_DIM
    t = batch * seq

    z_end = d_inner
    xbc_end = z_end + d_inner + bc_w
    dt_end = xbc_end + n_ssm_heads
    q_end = dt_end + att_w
    k_end = q_end + att_w
    v_end = k_end + att_w

    col_z, col_xs = 0, d_inner
    col_bc = 2 * d_inner
    col_q = col_bc + bc_w
    col_k = col_q + att_w
    col_v = col_k + att_w

    inv_freq = ROPE_THETA ** (-jnp.arange(0, ROPE_DIM, 2, dtype=F32) / ROPE_DIM)
    lane = jnp.arange(LANES) % ATT_HEAD_DIM
    inv_freq_lane = jnp.where(lane < ROPE_DIM, inv_freq[lane % (ROPE_DIM // 2)], 0.0)[None, :].astype(F32)
    head_of_col = jnp.arange(d_inner) // SSM_HEAD_DIM
    assert n_ssm_heads <= LANES
    e_mat = (jnp.arange(LANES)[:, None] == head_of_col[None, :]).astype(BF16)
    tri = (jnp.arange(SSM_CHUNK)[None, :] <= jnp.arange(SSM_CHUNK)[:, None]).astype(BF16)
    tri = jnp.concatenate([tri, tri, tri], axis=1)
    L = SSM_CHUNK
    t_out = jnp.arange(L)[:, None, None]
    tap = jnp.arange(SSM_CONV)[None, :, None]
    src = t_out - (SSM_CONV - 1) + tap
    src_col = jnp.where(src >= 0, tap * L + src, SSM_CONV * L + tap * CONV_CARRY + CONV_CARRY + src)
    shift = (jnp.arange((SSM_CONV + 1) * L)[None, None, :] == src_col).any(axis=1).astype(BF16)

    def pad_heads(v):
        return jnp.pad(v.astype(F32), (0, LANES - n_ssm_heads))[None, :]

    x2d = x.reshape(t, d_model)
    pos2d = positions.reshape(t, 1)
    for l in range(depth):
        lam_init = 0.8 - 0.6 * math.exp(-0.3 * l)
        wl = w_in[l]
        w_ssm = wl[:, :xbc_end].astype(BF16)
        w_att = wl[:, dt_end:v_end].astype(BF16)
        w_dt = jnp.pad(wl[:, xbc_end:dt_end], ((0, 0), (0, LANES - n_ssm_heads))).astype(BF16)
        w_gate = wl[:, v_end:].astype(BF16)

        proj, dt = _in_proj(x2d, pos2d, norm_pre_mix[l][None, :], w_ssm, w_att, w_dt, pad_heads(dt_bias[l]),
                            inv_freq_lane, col_q=col_q, col_k=col_k, col_v=col_v)

        cw = conv_w[l].astype(BF16)
        y_ssm = _ssd(proj, dt,
                     cw[:, :d_inner], conv_b[l][None, :d_inner],
                     cw[:, d_inner:], conv_b[l][None, d_inner:],
                     pad_heads(a_log[l]), jnp.repeat(d_skip[l], SSM_HEAD_DIM)[None, :],
                     ssm_norm_w[l][None, :], e_mat, tri, shift,
                     batch=batch, seq=seq, d_inner=d_inner, col_xs=col_xs, col_z=col_z, col_bc=col_bc)

        y_att = _attention(proj, lam_q1[l][None, :], lam_k1[l][None, :], lam_q2[l][None, :],
                           lam_k2[l][None, :], attn_subln_w[l][:, None],
                           batch=batch, seq=seq, n_heads=n_att_heads,
                           col_q=col_q, col_k=col_k, col_v=col_v, lam_init=lam_init)

        x2d = _mix(x2d, y_ssm, y_att, norm_pre_mix[l][None, :], norm_post_mix[l][None, :],
                   w_gate, w_ssm_out[l].astype(BF16), w_attn_out[l].astype(BF16),
                   w_mix_out[l].astype(BF16))
        x2d = _ffn(x2d, norm_pre_ffn[l][None, :], norm_post_ffn[l][None, :],
                   w_ffn_gate[l].astype(BF16), w_ffn_up[l].astype(BF16), w_ffn_down[l].astype(BF16))
    return x2d.reshape(batch, seq, d_model)
```

```python
import functools
import math

import jax
import jax.numpy as jnp
import numpy as np
from jax import lax
from jax.experimental import pallas as pl
from jax.experimental.pallas import tpu as pltpu

F32 = jnp.float32
BF16 = jnp.bfloat16

LANES = 128

SSM_HEAD_DIM = 64
SSM_N_GROUPS = 4
SSM_D_STATE = 128
SSM_CONV = 4
SSM_CHUNK = 128
CONV_CARRY = 16
ATT_HEAD_DIM = 64
ATT_V_DIM = 2 * ATT_HEAD_DIM
ROPE_THETA = 500000.0
ROPE_DIM = ATT_HEAD_DIM // 4
RMS_EPS = 1e-6
Q_SCALE = ATT_HEAD_DIM ** -0.5 * math.log2(math.e)

MIB = 1024 * 1024


def _rms(x, w):
    return x * lax.rsqrt(jnp.mean(x * x, axis=-1, keepdims=True) + RMS_EPS) * w


def _silu(x):
    hx = 0.5 * x
    return hx + hx * jnp.tanh(hx)


def _dot(a, b):
    return jnp.dot(a, b, preferred_element_type=F32)


def _dot_nt(a, b):
    return lax.dot_general(a, b, (((1,), (1,)), ((), ())), preferred_element_type=F32)


def _split3(x):
    hi = x.astype(BF16)
    r = x - hi.astype(F32)
    mid = r.astype(BF16)
    lo = (r - mid.astype(F32)).astype(BF16)
    return hi, mid, lo


def _const_spec(shape):
    nd = len(shape)
    return pl.BlockSpec(shape, lambda *_: (0,) * nd, pipeline_mode=pl.Buffered(1))


def _wprep_kernel(w_ref, tail_ref, a_ref, qkv_ref, gate_ref, dt_ref, *, a_w, dt_w):
    rows = w_ref.shape[0]
    a_ref[...] = w_ref[:, 0:a_w].astype(BF16)
    lane = lax.broadcasted_iota(jnp.int32, (rows, LANES), 1)
    first = w_ref[:, a_w:a_w + LANES]
    dt_ref[...] = jnp.where(lane < dt_w, first, 0.0).astype(BF16)
    keep = LANES - dt_w
    lo = pltpu.roll(first, keep, 1)
    qkv_blocks = qkv_ref.shape[1] // LANES
    n_blocks = qkv_blocks + gate_ref.shape[1] // LANES
    for j in range(n_blocks):
        c = a_w + (j + 1) * LANES
        nxt = tail_ref[...] if j == n_blocks - 1 else w_ref[:, c:c + LANES]
        hi = pltpu.roll(nxt, keep, 1)
        blk = jnp.where(lane < keep, lo, hi).astype(BF16)
        if j < qkv_blocks:
            qkv_ref[:, j * LANES:(j + 1) * LANES] = blk
        else:
            gate_ref[:, (j - qkv_blocks) * LANES:(j - qkv_blocks + 1) * LANES] = blk
        lo = hi


def _prep_in_weight(w, *, a_w, dt_w, qkv_w, tr=128):
    d, total = w.shape
    gate_w = total - a_w - dt_w - qkv_w
    assert a_w % LANES == 0 and qkv_w % LANES == 0 and gate_w % LANES == 0 and 0 < dt_w < LANES
    tail = jnp.pad(w[:, total - dt_w:], ((0, 0), (0, LANES - dt_w)))
    rows = lambda width: pl.BlockSpec((tr, width), lambda i: (i, 0))
    return pl.pallas_call(
        functools.partial(_wprep_kernel, a_w=a_w, dt_w=dt_w),
        grid=(d // tr,),
        in_specs=[rows(total), rows(LANES)],
        out_specs=[rows(a_w), rows(qkv_w), rows(gate_w), rows(LANES)],
        out_shape=[jax.ShapeDtypeStruct((d, width), BF16) for width in (a_w, qkv_w, gate_w, LANES)],
        compiler_params=pltpu.CompilerParams(
            dimension_semantics=("parallel",), vmem_limit_bytes=32 * MIB),
        name="w_prep",
    )(w, tail)


def _inproj_kernel(x_ref, pos_ref, nw_ref, wa_ref, wb_ref, wdt_ref, dtb_ref, invf_ref,
                   out_ref, dt_ref, h_scr, cos_scr, s1_scr, s2_scr, *, col_q, col_k, col_v, sub):
    hb = _rms(x_ref[...], nw_ref[...]).astype(BF16)
    h_scr[...] = hb
    dt_ref[...] = jax.nn.softplus(_dot(hb, wdt_ref[...]) + dtb_ref[...])
    ang = pos_ref[...].astype(F32) * invf_ref[...]
    lane = lax.broadcasted_iota(jnp.int32, ang.shape, 1) % ATT_HEAD_DIM
    sn = jnp.sin(ang)
    half = ROPE_DIM // 2
    cos_scr[...] = jnp.cos(ang)
    s1_scr[...] = jnp.where((lane >= half) & (lane < ROPE_DIM), sn, 0.0)
    s2_scr[...] = jnp.where(lane < half, -sn, 0.0)

    for c0 in range(0, out_ref.shape[1], sub):
        w_ref, w0 = (wa_ref, c0) if c0 < col_q else (wb_ref, c0 - col_q)
        acc = _dot(h_scr[...], w_ref[:, w0:w0 + sub])
        if col_q <= c0 < col_v:
            scale = Q_SCALE if c0 < col_k else 1.0
            for c in range(c0, c0 + sub, LANES):
                a = acc[:, c - c0:c - c0 + LANES]
                r = (a * cos_scr[...] + pltpu.roll(a, half, 1) * s1_scr[...]
                     + pltpu.roll(a, LANES - half, 1) * s2_scr[...])
                out_ref[:, c:c + LANES] = (r * scale).astype(out_ref.dtype)
        else:
            out_ref[:, c0:c0 + sub] = acc.astype(out_ref.dtype)


def _in_proj(x2d, pos2d, norm_w, w_a, w_b, w_dt, dt_bias, inv_freq_lane, *, col_q, col_k, col_v,
             tm=512, sub=512):
    t, d = x2d.shape
    n = w_a.shape[1] + w_b.shape[1]
    assert w_a.shape[1] == col_q and all(c % sub == 0 for c in (col_q, col_k, col_v, n))
    return pl.pallas_call(
        functools.partial(_inproj_kernel, col_q=col_q, col_k=col_k, col_v=col_v, sub=sub),
        grid=(t // tm,),
        in_specs=[
            pl.BlockSpec((tm, d), lambda i: (i, 0)),
            pl.BlockSpec((tm, 1), lambda i: (i, 0)),
            _const_spec((1, d)),
            _const_spec(w_a.shape),
            _const_spec(w_b.shape),
            _const_spec((d, LANES)),
            _const_spec((1, LANES)),
            _const_spec((1, LANES)),
        ],
        out_specs=[
            pl.BlockSpec((tm, n), lambda i: (i, 0)),
            pl.BlockSpec((tm, LANES), lambda i: (i, 0)),
        ],
        out_shape=[
            jax.ShapeDtypeStruct((t, n), BF16),
            jax.ShapeDtypeStruct((t, LANES), F32),
        ],
        scratch_shapes=[
            pltpu.VMEM((tm, d), BF16),
            pltpu.VMEM((tm, LANES), F32),
            pltpu.VMEM((tm, LANES), F32),
            pltpu.VMEM((tm, LANES), F32),
        ],
        compiler_params=pltpu.CompilerParams(
            dimension_semantics=("parallel",),
            vmem_limit_bytes=52 * MIB),
        name="in_proj",
    )(x2d, pos2d, norm_w, w_a, w_b, w_dt, dt_bias, inv_freq_lane)


def _ssd_kernel(xs_ref, z_ref, bc_ref, dt_ref, cwx_ref, cbx_ref, cwbc_ref, cbbc_ref,
                alog_ref, dskip_ref, nw_ref, e_ref, tri_ref, shift_ref,
                y_ref, xprev, bcprev, hstate, *, n_groups, d_state, head_dim, chunks):
    L = SSM_CHUNK
    d_inner = xs_ref.shape[1]
    gw = d_inner // n_groups
    heads_per_group = gw // head_dim
    pairs_per_group = heads_per_group // 2

    @pl.when(pl.program_id(1) == 0)
    def _():
        xprev[...] = jnp.zeros(xprev.shape, BF16)
        bcprev[...] = jnp.zeros(bcprev.shape, BF16)
        hstate[...] = jnp.zeros(hstate.shape, F32)

    a_neg = -jnp.exp(alog_ref[...]) * math.log2(math.e)
    row = lax.broadcasted_iota(jnp.int32, (L, L), 0)
    col = lax.broadcasted_iota(jnp.int32, (L, L), 1)
    causal = col <= row
    lane = lax.broadcasted_iota(jnp.int32, (L, LANES), 1)

    def conv_silu(cur_ref, prev, w_ref, b_ref, r0):
        cur = cur_ref[r0:r0 + L, :]
        old = prev[...] if r0 == 0 else cur_ref[r0 - CONV_CARRY:r0, :]
        wk = [w_ref[k * CONV_CARRY:(k + 1) * CONV_CARRY, :] for k in range(SSM_CONV)]
        taps = [cur * jnp.concatenate([wk[k]] * (L // CONV_CARRY), axis=0) for k in range(SSM_CONV)]
        carry = [old * wk[k] for k in range(SSM_CONV - 1)]
        pad = jnp.zeros((L - len(carry) * CONV_CARRY, cur.shape[1]), BF16)
        acc = b_ref[...] + _dot(shift_ref[...], jnp.concatenate(taps + carry + [pad], axis=0))
        return _silu(acc)

    h = [hstate[g] for g in range(n_groups)]
    for ci in range(chunks):
        r0 = ci * L
        xs = conv_silu(xs_ref, xprev, cwx_ref, cbx_ref, r0)
        bc = conv_silu(bc_ref, bcprev, cwbc_ref, cbbc_ref, r0).astype(BF16)

        dt = dt_ref[r0:r0 + L, :]
        a_cs = _dot(tri_ref[...], jnp.concatenate(_split3(dt * a_neg), axis=0))
        a_cs_t = a_cs.T
        ea = jnp.exp2(a_cs)
        dte = jnp.exp2(a_cs[L - 1:L, :] - a_cs)
        stacked = jnp.concatenate([dt, ea, dte], axis=0).astype(BF16)
        expanded = _dot(stacked, e_ref[...])
        dt_e = expanded[0:L]
        ea_e = expanded[L:2 * L]
        dte_e = expanded[2 * L:3 * L]

        xdt = xs * dt_e
        xdt_b = xdt.astype(BF16)
        xw_b = (xdt * dte_e).astype(BF16)
        gate = _silu(z_ref[r0:r0 + L, :].astype(F32))

        for g in range(n_groups):
            b_g = bc[:, g * d_state:(g + 1) * d_state]
            c_g = bc[:, (n_groups + g) * d_state:(n_groups + g + 1) * d_state]
            cb = _dot_nt(c_g, b_g)
            y_diag = []
            for p2 in range(pairs_per_group):
                pair = g * pairs_per_group + p2
                ms = []
                for hh in (2 * pair, 2 * pair + 1):
                    seg = a_cs[:, hh:hh + 1] - a_cs_t[hh:hh + 1, :]
                    dec = jnp.exp2(jnp.where(causal, seg, -jnp.inf))
                    ms.append((cb * dec).astype(BF16))
                lhs = jnp.concatenate(ms, axis=1)
                xp = xdt_b[:, pair * LANES:(pair + 1) * LANES]
                zero = jnp.zeros_like(xp)
                rhs = jnp.concatenate([jnp.where(lane < head_dim, xp, zero),
                                       jnp.where(lane >= head_dim, xp, zero)], axis=0)
                y_diag.append(_dot(lhs, rhs))
            sl = slice(g * gw, (g + 1) * gw)
            ea_g = ea_e[:, sl]
            y_off = _dot(c_g, h[g].astype(BF16)) * ea_g
            h[g] = h[g] * ea_g[L - 1:L, :] + _dot(b_g.T, xw_b[:, sl])
            yg = (jnp.concatenate(y_diag, axis=1) + y_off + xs[:, sl] * dskip_ref[:, sl]) * gate[:, sl]
            y_ref[r0:r0 + L, sl] = _rms(yg, nw_ref[:, sl]).astype(y_ref.dtype)

    for g in range(n_groups):
        hstate[g] = h[g]
    xprev[...] = xs_ref[chunks * L - CONV_CARRY:chunks * L, :]
    bcprev[...] = bc_ref[chunks * L - CONV_CARRY:chunks * L, :]


def _ssd(proj, dt, cw_x, cb_x, cw_bc, cb_bc, a_log, d_skip_e, norm_w, e_mat, tri, shift,
         *, batch, seq, d_inner, col_xs, col_z, col_bc, chunks=2):
    L = SSM_CHUNK
    rows = chunks * L
    nc = seq // rows
    t = batch * seq
    bc_w = 2 * SSM_N_GROUPS * SSM_D_STATE
    row = lambda b, c: b * nc + c
    kern = functools.partial(_ssd_kernel, n_groups=SSM_N_GROUPS, d_state=SSM_D_STATE,
                             head_dim=SSM_HEAD_DIM, chunks=chunks)
    return pl.pallas_call(
        kern,
        grid=(batch, nc),
        in_specs=[
            pl.BlockSpec((rows, d_inner), lambda b, c: (row(b, c), col_xs // d_inner)),
            pl.BlockSpec((rows, d_inner), lambda b, c: (row(b, c), col_z // d_inner)),
            pl.BlockSpec((rows, bc_w), lambda b, c: (row(b, c), col_bc // bc_w)),
            pl.BlockSpec((rows, LANES), lambda b, c: (row(b, c), 0)),
            pl.BlockSpec((SSM_CONV * CONV_CARRY, d_inner), lambda b, c: (0, 0)),
            pl.BlockSpec((1, d_inner), lambda b, c: (0, 0)),
            pl.BlockSpec((SSM_CONV * CONV_CARRY, bc_w), lambda b, c: (0, 0)),
            pl.BlockSpec((1, bc_w), lambda b, c: (0, 0)),
            pl.BlockSpec((1, LANES), lambda b, c: (0, 0)),
            pl.BlockSpec((1, d_inner), lambda b, c: (0, 0)),
            pl.BlockSpec((1, d_inner), lambda b, c: (0, 0)),
            pl.BlockSpec((LANES, d_inner), lambda b, c: (0, 0)),
            pl.BlockSpec((L, 3 * L), lambda b, c: (0, 0)),
            pl.BlockSpec(shift.shape, lambda b, c: (0, 0)),
        ],
        out_specs=pl.BlockSpec((rows, d_inner), lambda b, c: (row(b, c), 0)),
        out_shape=jax.ShapeDtypeStruct((t, d_inner), BF16),
        scratch_shapes=[
            pltpu.VMEM((CONV_CARRY, d_inner), BF16),
            pltpu.VMEM((CONV_CARRY, bc_w), BF16),
            pltpu.VMEM((SSM_N_GROUPS, SSM_D_STATE, d_inner // SSM_N_GROUPS), F32),
        ],
        compiler_params=pltpu.CompilerParams(
            dimension_semantics=("parallel", "arbitrary"),
            vmem_limit_bytes=40 * MIB),
        name="ssd",
    )(proj, proj, proj, dt, cw_x, cb_x, cw_bc, cb_bc, a_log, d_skip_e, norm_w, e_mat, tri, shift)


def _attn_kernel(q_ref, k_ref, v_ref, lq1_ref, lk1_ref, lq2_ref, lk2_ref, subw_ref,
                 o_ref, vt_scr, *, tq, tk, lam_init):
    seq = q_ref.shape[0]
    for jb in range(seq // tk):
        vt_scr[jb] = v_ref[jb * tk:(jb + 1) * tk, :].T

    lam = (jnp.exp(jnp.sum(lq1_ref[...] * lk1_ref[...]))
           - jnp.exp(jnp.sum(lq2_ref[...] * lk2_ref[...])) + lam_init)
    subw = subw_ref[...] * (1.0 - lam_init)

    def scores(qi, j):
        q = q_ref[qi * tq:(qi + 1) * tq, :]
        lane = lax.broadcasted_iota(jnp.int32, q.shape, 1)
        zero = jnp.zeros_like(q)
        q2 = jnp.concatenate([jnp.where(lane < ATT_HEAD_DIM, q, zero),
                              jnp.where(lane >= ATT_HEAD_DIM, q, zero)], axis=0)
        return _dot_nt(k_ref[j * tk:(j + 1) * tk, :], q2)

    steps = [(qi, j) for qi in range(seq // tq) for j in range((qi + 1) * tq // tk)]
    s_next = scores(*steps[0])
    m = l = acc = None
    for t, (qi, j) in enumerate(steps):
        s = s_next
        if t + 1 < len(steps):
            s_next = scores(*steps[t + 1])
        if (j + 1) * tk > qi * tq:
            kv = lax.broadcasted_iota(jnp.int32, s.shape, 0) + (j * tk - qi * tq)
            r = lax.broadcasted_iota(jnp.int32, s.shape, 1)
            r = jnp.where(r >= tq, r - tq, r)
            s = jnp.where(kv <= r, s, -jnp.inf)
        s_max = jnp.max(s, axis=0, keepdims=True)
        if j == 0:
            m = s_max
            p = jnp.exp2(s - m)
            l = jnp.sum(p, axis=0, keepdims=True)
            acc = _dot(vt_scr[j], p.astype(BF16))
        else:
            m_new = jnp.maximum(m, s_max)
            alpha = jnp.exp2(m - m_new)
            p = jnp.exp2(s - m_new)
            l = alpha * l + jnp.sum(p, axis=0, keepdims=True)
            acc = alpha * acc + _dot(vt_scr[j], p.astype(BF16))
            m = m_new
        if (j + 1) * tk == (qi + 1) * tq:
            o = acc / l
            od = o[:, 0:tq] - lam * o[:, tq:2 * tq]
            od = od * lax.rsqrt(jnp.mean(od * od, axis=0, keepdims=True) + RMS_EPS)
            o_ref[qi * tq:(qi + 1) * tq, :] = (od * subw).T.astype(o_ref.dtype)


def _attention(proj, lq1, lk1, lq2, lk2, subw, *, batch, seq, n_heads, col_q, col_k, col_v,
               lam_init, tq=512, tk=512):
    t = batch * seq
    hd = ATT_V_DIM
    kern = functools.partial(_attn_kernel, tq=tq, tk=tk, lam_init=lam_init)
    small = pl.BlockSpec((1, ATT_HEAD_DIM), lambda b, h: (0, 0))
    return pl.pallas_call(
        kern,
        grid=(batch, n_heads),
        in_specs=[
            pl.BlockSpec((seq, hd), lambda b, h: (b, col_q // hd + h)),
            pl.BlockSpec((seq, hd), lambda b, h: (b, col_k // hd + h)),
            pl.BlockSpec((seq, hd), lambda b, h: (b, col_v // hd + h)),
            small, small, small, small,
            pl.BlockSpec((hd, 1), lambda b, h: (0, 0)),
        ],
        out_specs=pl.BlockSpec((seq, hd), lambda b, h: (b, h)),
        out_shape=jax.ShapeDtypeStruct((t, n_heads * hd), BF16),
        scratch_shapes=[pltpu.VMEM((seq // tk, hd, tk), BF16)],
        compiler_params=pltpu.CompilerParams(
            dimension_semantics=("parallel", "parallel"),
            vmem_limit_bytes=32 * MIB),
        name="diff_attn",
    )(proj, proj, proj, lq1, lk1, lq2, lk2, subw)


def _mix_kernel(x_ref, ys_ref, ya_ref, npre_ref, npost_ref, wg_ref, wso_ref, wao_ref, wmix_ref, o_ref):
    x = x_ref[...]
    d = x.shape[1]
    h = _rms(x, npre_ref[...]).astype(BF16)
    gates = jax.nn.sigmoid(_dot(h, wg_ref[...]))
    y_ssm = _dot(ys_ref[...], wso_ref[...])
    y_att = _dot(ya_ref[...], wao_ref[...])
    blend = gates[:, 0:d] * y_ssm + gates[:, d:2 * d] * y_att
    mixed = _dot(blend.astype(BF16), wmix_ref[...])
    o_ref[...] = x + _rms(mixed, npost_ref[...])


def _mix(x2d, y_ssm, y_att, n_pre, n_post, w_gate, w_so, w_ao, w_mix, *, tm=512):
    t, d = x2d.shape
    rows = lambda w: pl.BlockSpec((tm, w), lambda i: (i, 0))
    return pl.pallas_call(
        _mix_kernel,
        grid=(t // tm,),
        in_specs=[rows(d), rows(y_ssm.shape[1]), rows(y_att.shape[1]),
                  _const_spec(n_pre.shape), _const_spec(n_post.shape),
                  _const_spec(w_gate.shape), _const_spec(w_so.shape),
                  _const_spec(w_ao.shape), _const_spec(w_mix.shape)],
        out_specs=rows(d),
        out_shape=jax.ShapeDtypeStruct((t, d), F32),
        compiler_params=pltpu.CompilerParams(
            dimension_semantics=("parallel",), vmem_limit_bytes=48 * MIB),
        name="mix",
    )(x2d, y_ssm, y_att, n_pre, n_post, w_gate, w_so, w_ao, w_mix)


def _ffn_kernel(x_ref, npre_ref, npost_ref, wg_ref, wu_ref, wd_ref, o_ref):
    x = x_ref[...]
    h = _rms(x, npre_ref[...]).astype(BF16)
    act = (_silu(_dot(h, wg_ref[...])) * _dot(h, wu_ref[...])).astype(BF16)
    f = _dot(act, wd_ref[...])
    o_ref[...] = x + _rms(f, npost_ref[...])


def _ffn(x2d, n_pre, n_post, w_g, w_u, w_d, *, tm=512):
    t, d = x2d.shape
    rows = pl.BlockSpec((tm, d), lambda i: (i, 0))
    return pl.pallas_call(
        _ffn_kernel,
        grid=(t // tm,),
        in_specs=[rows, _const_spec(n_pre.shape), _const_spec(n_post.shape),
                  _const_spec(w_g.shape), _const_spec(w_u.shape), _const_spec(w_d.shape)],
        out_specs=rows,
        out_shape=jax.ShapeDtypeStruct((t, d), F32),
        compiler_params=pltpu.CompilerParams(
            dimension_semantics=("parallel",), vmem_limit_bytes=56 * MIB),
        name="ffn",
    )(x2d, n_pre, n_post, w_g, w_u, w_d)


def kernel(x, positions, w_in, conv_w, conv_b, dt_bias, a_log, d_skip, ssm_norm_w, w_ssm_out, lam_q1, lam_k1, lam_q2, lam_k2, attn_subln_w, w_attn_out, w_mix_out, norm_pre_mix, norm_post_mix, norm_pre_ffn, norm_post_ffn, w_ffn_gate, w_ffn_up, w_ffn_down):
    batch, seq, d_model = x.shape
    depth = w_in.shape[0]
    d_inner = w_ssm_out.shape[1]
    n_ssm_heads = dt_bias.shape[1]
    bc_w = 2 * SSM_N_GROUPS * SSM_D_STATE
    att_w = w_attn_out.shape[1]
    n_att_heads = att_w // ATT_V_DIM
    t = batch * seq

    z_end = d_inner
    xbc_end = z_end + d_inner + bc_w
    dt_end = xbc_end + n_ssm_heads
    q_end = dt_end + att_w
    k_end = q_end + att_w
    v_end = k_end + att_w

    col_z, col_xs = 0, d_inner
    col_bc = 2 * d_inner
    col_q = col_bc + bc_w
    col_k = col_q + att_w
    col_v = col_k + att_w

    inv_freq = ROPE_THETA ** (-jnp.arange(0, ROPE_DIM, 2, dtype=F32) / ROPE_DIM)
    lane = jnp.arange(LANES) % ATT_HEAD_DIM
    inv_freq_lane = jnp.where(lane < ROPE_DIM, inv_freq[lane % (ROPE_DIM // 2)], 0.0)[None, :].astype(F32)
    head_of_col = np.arange(d_inner) // SSM_HEAD_DIM
    assert n_ssm_heads <= LANES
    e_mat = jnp.asarray(np.arange(LANES)[:, None] == head_of_col[None, :], BF16)
    tri = np.arange(SSM_CHUNK)[None, :] <= np.arange(SSM_CHUNK)[:, None]
    tri = jnp.asarray(np.concatenate([tri, tri, tri], axis=1), BF16)
    L = SSM_CHUNK
    t_out = np.arange(L)[:, None, None]
    tap = np.arange(SSM_CONV)[None, :, None]
    src = t_out - (SSM_CONV - 1) + tap
    src_col = np.where(src >= 0, tap * L + src, SSM_CONV * L + tap * CONV_CARRY + CONV_CARRY + src)
    shift = jnp.asarray((np.arange((SSM_CONV + 1) * L)[None, None, :] == src_col).any(axis=1), BF16)

    def pad_heads(v):
        return jnp.pad(v.astype(F32), (0, LANES - n_ssm_heads))[None, :]

    x2d = x.reshape(t, d_model)
    pos2d = positions.reshape(t, 1)
    for l in range(depth):
        lam_init = 0.8 - 0.6 * math.exp(-0.3 * l)
        wl = w_in[l]
        w_ssm, w_att, w_gate, w_dt = _prep_in_weight(wl, a_w=xbc_end, dt_w=n_ssm_heads, qkv_w=3 * att_w)

        proj, dt = _in_proj(x2d, pos2d, norm_pre_mix[l][None, :], w_ssm, w_att, w_dt, pad_heads(dt_bias[l]),
                            inv_freq_lane, col_q=col_q, col_k=col_k, col_v=col_v)

        cw = jnp.repeat(conv_w[l].astype(BF16), CONV_CARRY, axis=0)
        y_ssm = _ssd(proj, dt,
                     cw[:, :d_inner], conv_b[l][None, :d_inner],
                     cw[:, d_inner:], conv_b[l][None, d_inner:],
                     pad_heads(a_log[l]), jnp.repeat(d_skip[l], SSM_HEAD_DIM)[None, :],
                     ssm_norm_w[l][None, :], e_mat, tri, shift,
                     batch=batch, seq=seq, d_inner=d_inner, col_xs=col_xs, col_z=col_z, col_bc=col_bc)

        y_att = _attention(proj, lam_q1[l][None, :], lam_k1[l][None, :], lam_q2[l][None, :],
                           lam_k2[l][None, :], attn_subln_w[l][:, None],
                           batch=batch, seq=seq, n_heads=n_att_heads,
                           col_q=col_q, col_k=col_k, col_v=col_v, lam_init=lam_init)

        x2d = _mix(x2d, y_ssm, y_att, norm_pre_mix[l][None, :], norm_post_mix[l][None, :],
                   w_gate, w_ssm_out[l].astype(BF16), w_attn_out[l].astype(BF16),
                   w_mix_out[l].astype(BF16))
        x2d = _ffn(x2d, norm_pre_ffn[l][None, :], norm_post_ffn[l][None, :],
                   w_ffn_gate[l].astype(BF16), w_ffn_up[l].astype(BF16), w_ffn_down[l].astype(BF16))
    return x2d.reshape(batch, seq, d_model)
```

```python
import functools
import math

import jax
import jax.numpy as jnp
import numpy as np
from jax import lax
from jax.experimental import pallas as pl
from jax.experimental.pallas import tpu as pltpu

F32 = jnp.float32
BF16 = jnp.bfloat16

LANES = 128

SSM_HEAD_DIM = 64
SSM_N_GROUPS = 4
SSM_D_STATE = 128
SSM_CONV = 4
SSM_CHUNK = 128
CONV_CARRY = 16
ATT_HEAD_DIM = 64
ATT_V_DIM = 2 * ATT_HEAD_DIM
ROPE_THETA = 500000.0
ROPE_DIM = ATT_HEAD_DIM // 4
RMS_EPS = 1e-6
Q_SCALE = ATT_HEAD_DIM ** -0.5 * math.log2(math.e)

MIB = 1024 * 1024


def _rms(x, w):
    return x * lax.rsqrt(jnp.mean(x * x, axis=-1, keepdims=True) + RMS_EPS) * w


def _silu(x):
    hx = 0.5 * x
    return hx + hx * jnp.tanh(hx)


def _dot(a, b):
    return jnp.dot(a, b, preferred_element_type=F32)


def _dot_nt(a, b):
    return lax.dot_general(a, b, (((1,), (1,)), ((), ())), preferred_element_type=F32)


def _split3(x):
    hi = x.astype(BF16)
    r = x - hi.astype(F32)
    mid = r.astype(BF16)
    lo = (r - mid.astype(F32)).astype(BF16)
    return hi, mid, lo


def _const_spec(shape):
    nd = len(shape)
    return pl.BlockSpec(shape, lambda *_: (0,) * nd, pipeline_mode=pl.Buffered(1))


def _inproj_kernel(x_ref, pos_ref, nw_ref, wt_ref, dtb_ref, invf_ref,
                   out_ref, dt_ref, h_scr, cos_scr, s1_scr, s2_scr,
                   *, row_dt, n_dt, row_q, col_q, col_k, col_v, sub):
    hb = _rms(x_ref[...], nw_ref[...]).astype(BF16)
    h_scr[...] = hb
    dt_lane = lax.broadcasted_iota(jnp.int32, dt_ref.shape, 1)
    dt = jax.nn.softplus(_dot_nt(hb, wt_ref[row_dt:row_dt + LANES, :]) + dtb_ref[...])
    dt_ref[...] = jnp.where(dt_lane < n_dt, dt, 0.0)
    ang = pos_ref[...].astype(F32) * invf_ref[...]
    lane = lax.broadcasted_iota(jnp.int32, ang.shape, 1) % ATT_HEAD_DIM
    sn = jnp.sin(ang)
    half = ROPE_DIM // 2
    cos_scr[...] = jnp.cos(ang)
    s1_scr[...] = jnp.where((lane >= half) & (lane < ROPE_DIM), sn, 0.0)
    s2_scr[...] = jnp.where(lane < half, -sn, 0.0)

    for c0 in range(0, out_ref.shape[1], sub):
        r0 = c0 if c0 < col_q else c0 - col_q + row_q
        acc = _dot_nt(h_scr[...], wt_ref[r0:r0 + sub, :])
        if col_q <= c0 < col_v:
            scale = Q_SCALE if c0 < col_k else 1.0
            for c in range(c0, c0 + sub, LANES):
                a = acc[:, c - c0:c - c0 + LANES]
                r = (a * cos_scr[...] + pltpu.roll(a, half, 1) * s1_scr[...]
                     + pltpu.roll(a, LANES - half, 1) * s2_scr[...])
                out_ref[:, c:c + LANES] = (r * scale).astype(out_ref.dtype)
        else:
            out_ref[:, c0:c0 + sub] = acc.astype(out_ref.dtype)


def _in_proj(x2d, pos2d, norm_w, w_t, dt_bias, inv_freq_lane, *, n_out, row_dt, n_dt, row_q,
             col_q, col_k, col_v, tm=512, sub=512):
    t, d = x2d.shape
    assert all(c % sub == 0 for c in (col_q, col_k, col_v, n_out))
    kern = functools.partial(_inproj_kernel, row_dt=row_dt, n_dt=n_dt, row_q=row_q,
                             col_q=col_q, col_k=col_k, col_v=col_v, sub=sub)
    return pl.pallas_call(
        kern,
        grid=(t // tm,),
        in_specs=[
            pl.BlockSpec((tm, d), lambda i: (i, 0)),
            pl.BlockSpec((tm, 1), lambda i: (i, 0)),
            _const_spec((1, d)),
            _const_spec(w_t.shape),
            _const_spec((1, LANES)),
            _const_spec((1, LANES)),
        ],
        out_specs=[
            pl.BlockSpec((tm, n_out), lambda i: (i, 0)),
            pl.BlockSpec((tm, LANES), lambda i: (i, 0)),
        ],
        out_shape=[
            jax.ShapeDtypeStruct((t, n_out), BF16),
            jax.ShapeDtypeStruct((t, LANES), F32),
        ],
        scratch_shapes=[
            pltpu.VMEM((tm, d), BF16),
            pltpu.VMEM((tm, LANES), F32),
            pltpu.VMEM((tm, LANES), F32),
            pltpu.VMEM((tm, LANES), F32),
        ],
        compiler_params=pltpu.CompilerParams(
            dimension_semantics=("parallel",),
            vmem_limit_bytes=56 * MIB),
        name="in_proj",
    )(x2d, pos2d, norm_w, w_t, dt_bias, inv_freq_lane)


def _ssd_kernel(xs_ref, z_ref, bc_ref, dt_ref, cwx_ref, cbx_ref, cwbc_ref, cbbc_ref,
                alog_ref, dskip_ref, nw_ref, e_ref, tri_ref, shift_ref,
                y_ref, xprev, bcprev, hstate, *, n_groups, d_state, head_dim, chunks):
    L = SSM_CHUNK
    d_inner = xs_ref.shape[1]
    gw = d_inner // n_groups
    heads_per_group = gw // head_dim
    pairs_per_group = heads_per_group // 2

    @pl.when(pl.program_id(1) == 0)
    def _():
        xprev[...] = jnp.zeros(xprev.shape, BF16)
        bcprev[...] = jnp.zeros(bcprev.shape, BF16)
        hstate[...] = jnp.zeros(hstate.shape, F32)

    a_neg = -jnp.exp(alog_ref[...]) * math.log2(math.e)
    row = lax.broadcasted_iota(jnp.int32, (L, L), 0)
    col = lax.broadcasted_iota(jnp.int32, (L, L), 1)
    causal = col <= row
    lane = lax.broadcasted_iota(jnp.int32, (L, LANES), 1)

    def conv_silu(cur_ref, prev, w_ref, b_ref, r0):
        cur = cur_ref[r0:r0 + L, :]
        old = prev[...] if r0 == 0 else cur_ref[r0 - CONV_CARRY:r0, :]
        wk = [w_ref[k * CONV_CARRY:(k + 1) * CONV_CARRY, :] for k in range(SSM_CONV)]
        taps = [cur * jnp.concatenate([wk[k]] * (L // CONV_CARRY), axis=0) for k in range(SSM_CONV)]
        carry = [old * wk[k] for k in range(SSM_CONV - 1)]
        pad = jnp.zeros((L - len(carry) * CONV_CARRY, cur.shape[1]), BF16)
        acc = b_ref[...] + _dot(shift_ref[...], jnp.concatenate(taps + carry + [pad], axis=0))
        return _silu(acc)

    h = [hstate[g] for g in range(n_groups)]
    for ci in range(chunks):
        r0 = ci * L
        xs = conv_silu(xs_ref, xprev, cwx_ref, cbx_ref, r0)
        bc = conv_silu(bc_ref, bcprev, cwbc_ref, cbbc_ref, r0).astype(BF16)

        dt = dt_ref[r0:r0 + L, :]
        a_cs = _dot(tri_ref[...], jnp.concatenate(_split3(dt * a_neg), axis=0))
        a_cs_t = a_cs.T
        ea = jnp.exp2(a_cs)
        dte = jnp.exp2(a_cs[L - 1:L, :] - a_cs)
        stacked = jnp.concatenate([dt, ea, dte], axis=0).astype(BF16)
        expanded = _dot(stacked, e_ref[...])
        dt_e = expanded[0:L]
        ea_e = expanded[L:2 * L]
        dte_e = expanded[2 * L:3 * L]

        xdt = xs * dt_e
        xdt_b = xdt.astype(BF16)
        xw_b = (xdt * dte_e).astype(BF16)
        gate = _silu(z_ref[r0:r0 + L, :].astype(F32))

        for g in range(n_groups):
            b_g = bc[:, g * d_state:(g + 1) * d_state]
            c_g = bc[:, (n_groups + g) * d_state:(n_groups + g + 1) * d_state]
            cb = _dot_nt(c_g, b_g)
            y_diag = []
            for p2 in range(pairs_per_group):
                pair = g * pairs_per_group + p2
                ms = []
                for hh in (2 * pair, 2 * pair + 1):
                    seg = a_cs[:, hh:hh + 1] - a_cs_t[hh:hh + 1, :]
                    dec = jnp.exp2(jnp.where(causal, seg, -jnp.inf))
                    ms.append((cb * dec).astype(BF16))
                lhs = jnp.concatenate(ms, axis=1)
                xp = xdt_b[:, pair * LANES:(pair + 1) * LANES]
                zero = jnp.zeros_like(xp)
                rhs = jnp.concatenate([jnp.where(lane < head_dim, xp, zero),
                                       jnp.where(lane >= head_dim, xp, zero)], axis=0)
                y_diag.append(_dot(lhs, rhs))
            sl = slice(g * gw, (g + 1) * gw)
            ea_g = ea_e[:, sl]
            y_off = _dot(c_g, h[g].astype(BF16)) * ea_g
            h[g] = h[g] * ea_g[L - 1:L, :] + _dot(b_g.T, xw_b[:, sl])
            yg = (jnp.concatenate(y_diag, axis=1) + y_off + xs[:, sl] * dskip_ref[:, sl]) * gate[:, sl]
            y_ref[r0:r0 + L, sl] = _rms(yg, nw_ref[:, sl]).astype(y_ref.dtype)

    for g in range(n_groups):
        hstate[g] = h[g]
    xprev[...] = xs_ref[chunks * L - CONV_CARRY:chunks * L, :]
    bcprev[...] = bc_ref[chunks * L - CONV_CARRY:chunks * L, :]


def _ssd(proj, dt, cw_x, cb_x, cw_bc, cb_bc, a_log, d_skip_e, norm_w, e_mat, tri, shift,
         *, batch, seq, d_inner, col_xs, col_z, col_bc, chunks=2):
    L = SSM_CHUNK
    rows = chunks * L
    nc = seq // rows
    t = batch * seq
    bc_w = 2 * SSM_N_GROUPS * SSM_D_STATE
    row = lambda b, c: b * nc + c
    kern = functools.partial(_ssd_kernel, n_groups=SSM_N_GROUPS, d_state=SSM_D_STATE,
                             head_dim=SSM_HEAD_DIM, chunks=chunks)
    return pl.pallas_call(
        kern,
        grid=(batch, nc),
        in_specs=[
            pl.BlockSpec((rows, d_inner), lambda b, c: (row(b, c), col_xs // d_inner)),
            pl.BlockSpec((rows, d_inner), lambda b, c: (row(b, c), col_z // d_inner)),
            pl.BlockSpec((rows, bc_w), lambda b, c: (row(b, c), col_bc // bc_w)),
            pl.BlockSpec((rows, LANES), lambda b, c: (row(b, c), 0)),
            pl.BlockSpec((SSM_CONV * CONV_CARRY, d_inner), lambda b, c: (0, 0)),
            pl.BlockSpec((1, d_inner), lambda b, c: (0, 0)),
            pl.BlockSpec((SSM_CONV * CONV_CARRY, bc_w), lambda b, c: (0, 0)),
            pl.BlockSpec((1, bc_w), lambda b, c: (0, 0)),
            pl.BlockSpec((1, LANES), lambda b, c: (0, 0)),
            pl.BlockSpec((1, d_inner), lambda b, c: (0, 0)),
            pl.BlockSpec((1, d_inner), lambda b, c: (0, 0)),
            pl.BlockSpec((LANES, d_inner), lambda b, c: (0, 0)),
            pl.BlockSpec((L, 3 * L), lambda b, c: (0, 0)),
            pl.BlockSpec(shift.shape, lambda b, c: (0, 0)),
        ],
        out_specs=pl.BlockSpec((rows, d_inner), lambda b, c: (row(b, c), 0)),
        out_shape=jax.ShapeDtypeStruct((t, d_inner), BF16),
        scratch_shapes=[
            pltpu.VMEM((CONV_CARRY, d_inner), BF16),
            pltpu.VMEM((CONV_CARRY, bc_w), BF16),
            pltpu.VMEM((SSM_N_GROUPS, SSM_D_STATE, d_inner // SSM_N_GROUPS), F32),
        ],
        compiler_params=pltpu.CompilerParams(
            dimension_semantics=("parallel", "arbitrary"),
            vmem_limit_bytes=40 * MIB),
        name="ssd",
    )(proj, proj, proj, dt, cw_x, cb_x, cw_bc, cb_bc, a_log, d_skip_e, norm_w, e_mat, tri, shift)


def _attn_kernel(q_ref, k_ref, v_ref, lq1_ref, lk1_ref, lq2_ref, lk2_ref, subw_ref,
                 o_ref, vt_scr, *, tq, tk, lam_init):
    seq = q_ref.shape[0]
    for jb in range(seq // tk):
        vt_scr[jb] = v_ref[jb * tk:(jb + 1) * tk, :].T

    lam = (jnp.exp(jnp.sum(lq1_ref[...] * lk1_ref[...]))
           - jnp.exp(jnp.sum(lq2_ref[...] * lk2_ref[...])) + lam_init)
    subw = subw_ref[...] * (1.0 - lam_init)

    def scores(qi, j):
        q = q_ref[qi * tq:(qi + 1) * tq, :]
        lane = lax.broadcasted_iota(jnp.int32, q.shape, 1)
        zero = jnp.zeros_like(q)
        q2 = jnp.concatenate([jnp.where(lane < ATT_HEAD_DIM, q, zero),
                              jnp.where(lane >= ATT_HEAD_DIM, q, zero)], axis=0)
        return _dot_nt(k_ref[j * tk:(j + 1) * tk, :], q2)

    steps = [(qi, j) for qi in range(seq // tq) for j in range((qi + 1) * tq // tk)]
    s_next = scores(*steps[0])
    m = l = acc = None
    for t, (qi, j) in enumerate(steps):
        s = s_next
        if t + 1 < len(steps):
            s_next = scores(*steps[t + 1])
        if (j + 1) * tk > qi * tq:
            kv = lax.broadcasted_iota(jnp.int32, s.shape, 0) + (j * tk - qi * tq)
            r = lax.broadcasted_iota(jnp.int32, s.shape, 1)
            r = jnp.where(r >= tq, r - tq, r)
            s = jnp.where(kv <= r, s, -jnp.inf)
        s_max = jnp.max(s, axis=0, keepdims=True)
        if j == 0:
            m = s_max
            p = jnp.exp2(s - m)
            l = jnp.sum(p, axis=0, keepdims=True)
            acc = _dot(vt_scr[j], p.astype(BF16))
        else:
            m_new = jnp.maximum(m, s_max)
            alpha = jnp.exp2(m - m_new)
            p = jnp.exp2(s - m_new)
            l = alpha * l + jnp.sum(p, axis=0, keepdims=True)
            acc = alpha * acc + _dot(vt_scr[j], p.astype(BF16))
            m = m_new
        if (j + 1) * tk == (qi + 1) * tq:
            o = acc / l
            od = o[:, 0:tq] - lam * o[:, tq:2 * tq]
            od = od * lax.rsqrt(jnp.mean(od * od, axis=0, keepdims=True) + RMS_EPS)
            o_ref[qi * tq:(qi + 1) * tq, :] = (od * subw).T.astype(o_ref.dtype)


def _attention(proj, lq1, lk1, lq2, lk2, subw, *, batch, seq, n_heads, col_q, col_k, col_v,
               lam_init, tq=512, tk=512):
    t = batch * seq
    hd = ATT_V_DIM
    kern = functools.partial(_attn_kernel, tq=tq, tk=tk, lam_init=lam_init)
    small = pl.BlockSpec((1, ATT_HEAD_DIM), lambda b, h: (0, 0))
    return pl.pallas_call(
        kern,
        grid=(batch, n_heads),
        in_specs=[
            pl.BlockSpec((seq, hd), lambda b, h: (b, col_q // hd + h)),
            pl.BlockSpec((seq, hd), lambda b, h: (b, col_k // hd + h)),
            pl.BlockSpec((seq, hd), lambda b, h: (b, col_v // hd + h)),
            small, small, small, small,
            pl.BlockSpec((hd, 1), lambda b, h: (0, 0)),
        ],
        out_specs=pl.BlockSpec((seq, hd), lambda b, h: (b, h)),
        out_shape=jax.ShapeDtypeStruct((t, n_heads * hd), BF16),
        scratch_shapes=[pltpu.VMEM((seq // tk, hd, tk), BF16)],
        compiler_params=pltpu.CompilerParams(
            dimension_semantics=("parallel", "parallel"),
            vmem_limit_bytes=32 * MIB),
        name="diff_attn",
    )(proj, proj, proj, lq1, lk1, lq2, lk2, subw)


def _mix_kernel(x_ref, ys_ref, ya_ref, npre_ref, npost_ref, wg_ref, wso_ref, wao_ref, wmix_ref, o_ref):
    x = x_ref[...]
    d = x.shape[1]
    h = _rms(x, npre_ref[...]).astype(BF16)
    gates = jax.nn.sigmoid(_dot_nt(h, wg_ref[...]))
    y_ssm = _dot(ys_ref[...], wso_ref[...])
    y_att = _dot(ya_ref[...], wao_ref[...])
    blend = gates[:, 0:d] * y_ssm + gates[:, d:2 * d] * y_att
    mixed = _dot(blend.astype(BF16), wmix_ref[...])
    o_ref[...] = x + _rms(mixed, npost_ref[...])


def _mix(x2d, y_ssm, y_att, n_pre, n_post, w_gate, w_so, w_ao, w_mix, *, tm=512):
    t, d = x2d.shape
    rows = lambda w: pl.BlockSpec((tm, w), lambda i: (i, 0))
    return pl.pallas_call(
        _mix_kernel,
        grid=(t // tm,),
        in_specs=[rows(d), rows(y_ssm.shape[1]), rows(y_att.shape[1]),
                  _const_spec(n_pre.shape), _const_spec(n_post.shape),
                  _const_spec(w_gate.shape), _const_spec(w_so.shape),
                  _const_spec(w_ao.shape), _const_spec(w_mix.shape)],
        out_specs=rows(d),
        out_shape=jax.ShapeDtypeStruct((t, d), F32),
        compiler_params=pltpu.CompilerParams(
            dimension_semantics=("parallel",), vmem_limit_bytes=48 * MIB),
        name="mix",
    )(x2d, y_ssm, y_att, n_pre, n_post, w_gate, w_so, w_ao, w_mix)


def _ffn_kernel(x_ref, npre_ref, npost_ref, wg_ref, wu_ref, wd_ref, o_ref):
    x = x_ref[...]
    h = _rms(x, npre_ref[...]).astype(BF16)
    act = (_silu(_dot(h, wg_ref[...])) * _dot(h, wu_ref[...])).astype(BF16)
    f = _dot(act, wd_ref[...])
    o_ref[...] = x + _rms(f, npost_ref[...])


def _ffn(x2d, n_pre, n_post, w_g, w_u, w_d, *, tm=512):
    t, d = x2d.shape
    rows = pl.BlockSpec((tm, d), lambda i: (i, 0))
    return pl.pallas_call(
        _ffn_kernel,
        grid=(t // tm,),
        in_specs=[rows, _const_spec(n_pre.shape), _const_spec(n_post.shape),
                  _const_spec(w_g.shape), _const_spec(w_u.shape), _const_spec(w_d.shape)],
        out_specs=rows,
        out_shape=jax.ShapeDtypeStruct((t, d), F32),
        compiler_params=pltpu.CompilerParams(
            dimension_semantics=("parallel",), vmem_limit_bytes=56 * MIB),
        name="ffn",
    )(x2d, n_pre, n_post, w_g, w_u, w_d)


def kernel(x, positions, w_in, conv_w, conv_b, dt_bias, a_log, d_skip, ssm_norm_w, w_ssm_out, lam_q1, lam_k1, lam_q2, lam_k2, attn_subln_w, w_attn_out, w_mix_out, norm_pre_mix, norm_post_mix, norm_pre_ffn, norm_post_ffn, w_ffn_gate, w_ffn_up, w_ffn_down):
    batch, seq, d_model = x.shape
    depth = w_in.shape[0]
    d_inner = w_ssm_out.shape[1]
    n_ssm_heads = dt_bias.shape[1]
    bc_w = 2 * SSM_N_GROUPS * SSM_D_STATE
    att_w = w_attn_out.shape[1]
    n_att_heads = att_w // ATT_V_DIM
    t = batch * seq

    z_end = d_inner
    xbc_end = z_end + d_inner + bc_w
    dt_end = xbc_end + n_ssm_heads
    q_end = dt_end + att_w
    k_end = q_end + att_w
    v_end = k_end + att_w

    col_z, col_xs = 0, d_inner
    col_bc = 2 * d_inner
    col_q = col_bc + bc_w
    col_k = col_q + att_w
    col_v = col_k + att_w

    inv_freq = ROPE_THETA ** (-jnp.arange(0, ROPE_DIM, 2, dtype=F32) / ROPE_DIM)
    lane = jnp.arange(LANES) % ATT_HEAD_DIM
    inv_freq_lane = jnp.where(lane < ROPE_DIM, inv_freq[lane % (ROPE_DIM // 2)], 0.0)[None, :].astype(F32)
    head_of_col = np.arange(d_inner) // SSM_HEAD_DIM
    assert n_ssm_heads <= LANES
    e_mat = jnp.asarray(np.arange(LANES)[:, None] == head_of_col[None, :], BF16)
    tri = np.arange(SSM_CHUNK)[None, :] <= np.arange(SSM_CHUNK)[:, None]
    tri = jnp.asarray(np.concatenate([tri, tri, tri], axis=1), BF16)
    L = SSM_CHUNK
    t_out = np.arange(L)[:, None, None]
    tap = np.arange(SSM_CONV)[None, :, None]
    src = t_out - (SSM_CONV - 1) + tap
    src_col = np.where(src >= 0, tap * L + src, SSM_CONV * L + tap * CONV_CARRY + CONV_CARRY + src)
    shift = jnp.asarray((np.arange((SSM_CONV + 1) * L)[None, None, :] == src_col).any(axis=1), BF16)

    def pad_heads(v):
        return jnp.pad(v.astype(F32), (0, LANES - n_ssm_heads))[None, :]

    x2d = x.reshape(t, d_model)
    pos2d = positions.reshape(t, 1)
    for l in range(depth):
        lam_init = 0.8 - 0.6 * math.exp(-0.3 * l)
        wl = w_in[l]
        w_t = wl.T.astype(BF16)
        w_gate_t = w_t[v_end:]

        proj, dt = _in_proj(x2d, pos2d, norm_pre_mix[l][None, :], w_t, pad_heads(dt_bias[l]), inv_freq_lane,
                            n_out=col_v + att_w, row_dt=xbc_end, n_dt=n_ssm_heads, row_q=dt_end,
                            col_q=col_q, col_k=col_k, col_v=col_v)

        cw = jnp.repeat(conv_w[l].astype(BF16), CONV_CARRY, axis=0)
        y_ssm = _ssd(proj, dt,
                     cw[:, :d_inner], conv_b[l][None, :d_inner],
                     cw[:, d_inner:], conv_b[l][None, d_inner:],
                     pad_heads(a_log[l]), jnp.repeat(d_skip[l], SSM_HEAD_DIM)[None, :],
                     ssm_norm_w[l][None, :], e_mat, tri, shift,
                     batch=batch, seq=seq, d_inner=d_inner, col_xs=col_xs, col_z=col_z, col_bc=col_bc)

        y_att = _attention(proj, lam_q1[l][None, :], lam_k1[l][None, :], lam_q2[l][None, :],
                           lam_k2[l][None, :], attn_subln_w[l][:, None],
                           batch=batch, seq=seq, n_heads=n_att_heads,
                           col_q=col_q, col_k=col_k, col_v=col_v, lam_init=lam_init)

        x2d = _mix(x2d, y_ssm, y_att, norm_pre_mix[l][None, :], norm_post_mix[l][None, :],
                   w_gate_t, w_ssm_out[l].astype(BF16), w_attn_out[l].astype(BF16),
                   w_mix_out[l].astype(BF16))
        x2d = _ffn(x2d, norm_pre_ffn[l][None, :], norm_post_ffn[l][None, :],
                   w_ffn_gate[l].astype(BF16), w_ffn_up[l].astype(BF16), w_ffn_down[l].astype(BF16))
    return x2d.reshape(batch, seq, d_model)
```

```python
import functools
import math

import jax
import jax.numpy as jnp
import numpy as np
from jax import lax
from jax.experimental import pallas as pl
from jax.experimental.pallas import tpu as pltpu

F32 = jnp.float32
BF16 = jnp.bfloat16

LANES = 128

SSM_HEAD_DIM = 64
SSM_N_GROUPS = 4
SSM_D_STATE = 128
SSM_CONV = 4
SSM_CHUNK = 128
CONV_CARRY = 16
ATT_HEAD_DIM = 64
ATT_V_DIM = 2 * ATT_HEAD_DIM
ROPE_THETA = 500000.0
ROPE_DIM = ATT_HEAD_DIM // 4
RMS_EPS = 1e-6
Q_SCALE = ATT_HEAD_DIM ** -0.5 * math.log2(math.e)

MIB = 1024 * 1024


def _rms(x, w):
    return x * lax.rsqrt(jnp.mean(x * x, axis=-1, keepdims=True) + RMS_EPS) * w


def _silu(x):
    hx = 0.5 * x
    return hx + hx * jnp.tanh(hx)


def _dot(a, b):
    return jnp.dot(a, b, preferred_element_type=F32)


def _dot_nt(a, b):
    return lax.dot_general(a, b, (((1,), (1,)), ((), ())), preferred_element_type=F32)


def _split3(x):
    hi = x.astype(BF16)
    r = x - hi.astype(F32)
    mid = r.astype(BF16)
    lo = (r - mid.astype(F32)).astype(BF16)
    return hi, mid, lo


def _const_spec(shape):
    nd = len(shape)
    return pl.BlockSpec(shape, lambda *_: (0,) * nd, pipeline_mode=pl.Buffered(1))


def _inproj_kernel(x_ref, pos_ref, nw_ref, wt_ref, dtb_ref, invf_ref, rope_e_ref, rope_c_ref,
                   out_ref, dt_ref, h_scr, cos_scr, s1_scr, s2_scr,
                   *, row_dt, n_dt, row_q, col_q, col_k, col_v, sub):
    hb = _rms(x_ref[...], nw_ref[...]).astype(BF16)
    h_scr[...] = hb
    dt_lane = lax.broadcasted_iota(jnp.int32, dt_ref.shape, 1)
    dt = jax.nn.softplus(_dot_nt(hb, wt_ref[row_dt:row_dt + LANES, :]) + dtb_ref[...])
    dt_ref[...] = jnp.where(dt_lane < n_dt, dt, 0.0)
    half = ROPE_DIM // 2
    ang = invf_ref[...] * pos_ref[...].astype(F32)
    terms = [t.astype(F32) for t in _split3(jnp.cos(ang)) + _split3(jnp.sin(ang))]
    fill = jnp.zeros((LANES - len(terms) * half, ang.shape[1]), F32)
    grid_t = jnp.concatenate(terms + [fill], axis=0).T.astype(BF16)
    tables = _dot(grid_t, rope_e_ref[...])
    cos_scr[...] = tables[:, 0:LANES] + rope_c_ref[...]
    s1_scr[...] = tables[:, LANES:2 * LANES]
    s2_scr[...] = tables[:, 2 * LANES:3 * LANES]

    for c0 in range(0, out_ref.shape[1], sub):
        r0 = c0 if c0 < col_q else c0 - col_q + row_q
        acc = _dot_nt(h_scr[...], wt_ref[r0:r0 + sub, :])
        if col_q <= c0 < col_v:
            scale = Q_SCALE if c0 < col_k else 1.0
            for c in range(c0, c0 + sub, LANES):
                a = acc[:, c - c0:c - c0 + LANES]
                r = (a * cos_scr[...] + pltpu.roll(a, half, 1) * s1_scr[...]
                     + pltpu.roll(a, LANES - half, 1) * s2_scr[...])
                out_ref[:, c:c + LANES] = (r * scale).astype(out_ref.dtype)
        else:
            out_ref[:, c0:c0 + sub] = acc.astype(out_ref.dtype)


def _in_proj(x2d, positions, norm_w, w_t, dt_bias, inv_freq, *, n_out, row_dt, n_dt, row_q,
             col_q, col_k, col_v, tm=512, sub=512):
    t, d = x2d.shape
    assert all(c % sub == 0 for c in (col_q, col_k, col_v, n_out))
    half = ROPE_DIM // 2
    lane = np.arange(LANES) % ATT_HEAD_DIM
    freq_row = np.arange(LANES) % half
    is_cos = np.arange(LANES) < 3 * half
    is_sin = (np.arange(LANES) >= 3 * half) & (np.arange(LANES) < 6 * half)
    same_f = freq_row[:, None] == (lane % half)[None, :]
    rope_e = np.concatenate([
        (is_cos[:, None] & same_f & (lane < ROPE_DIM)[None, :]) * 1.0,
        (is_sin[:, None] & same_f & ((lane >= half) & (lane < ROPE_DIM))[None, :]) * 1.0,
        (is_sin[:, None] & same_f & (lane < half)[None, :]) * -1.0], axis=1)
    rope_c = (lane >= ROPE_DIM)[None, :] * 1.0
    kern = functools.partial(_inproj_kernel, row_dt=row_dt, n_dt=n_dt, row_q=row_q,
                             col_q=col_q, col_k=col_k, col_v=col_v, sub=sub)
    return pl.pallas_call(
        kern,
        grid=(t // tm,),
        in_specs=[
            pl.BlockSpec((tm, d), lambda i: (i, 0)),
            pl.BlockSpec((None, 1, tm), lambda i: (i, 0, 0)),
            _const_spec((1, d)),
            _const_spec(w_t.shape),
            _const_spec((1, LANES)),
            _const_spec((half, 1)),
            _const_spec(rope_e.shape),
            _const_spec(rope_c.shape),
        ],
        out_specs=[
            pl.BlockSpec((tm, n_out), lambda i: (i, 0)),
            pl.BlockSpec((tm, LANES), lambda i: (i, 0)),
        ],
        out_shape=[
            jax.ShapeDtypeStruct((t, n_out), BF16),
            jax.ShapeDtypeStruct((t, LANES), F32),
        ],
        scratch_shapes=[
            pltpu.VMEM((tm, d), BF16),
            pltpu.VMEM((tm, LANES), F32),
            pltpu.VMEM((tm, LANES), F32),
            pltpu.VMEM((tm, LANES), F32),
        ],
        compiler_params=pltpu.CompilerParams(
            dimension_semantics=("parallel",),
            vmem_limit_bytes=56 * MIB),
        name="in_proj",
    )(x2d, positions.reshape(t // tm, 1, tm), norm_w, w_t, dt_bias, inv_freq[:, None],
      jnp.asarray(rope_e, BF16), jnp.asarray(rope_c, F32))


def _ssd_kernel(xs_ref, z_ref, bc_ref, dt_ref, cwx_ref, cbx_ref, cwbc_ref, cbbc_ref,
                alog_ref, dskip_ref, nw_ref, e_ref, tri_ref, shift_ref,
                y_ref, xprev, bcprev, hstate, *, n_groups, d_state, head_dim, chunks):
    L = SSM_CHUNK
    d_inner = xs_ref.shape[1]
    gw = d_inner // n_groups
    heads_per_group = gw // head_dim
    pairs_per_group = heads_per_group // 2

    @pl.when(pl.program_id(1) == 0)
    def _():
        xprev[...] = jnp.zeros(xprev.shape, BF16)
        bcprev[...] = jnp.zeros(bcprev.shape, BF16)
        hstate[...] = jnp.zeros(hstate.shape, F32)

    a_neg = -jnp.exp(alog_ref[...]) * math.log2(math.e)
    row = lax.broadcasted_iota(jnp.int32, (L, L), 0)
    col = lax.broadcasted_iota(jnp.int32, (L, L), 1)
    causal = col <= row
    lane = lax.broadcasted_iota(jnp.int32, (L, LANES), 1)

    def conv_silu(cur_ref, prev, w_ref, b_ref, r0):
        cur = cur_ref[r0:r0 + L, :]
        old = prev[...] if r0 == 0 else cur_ref[r0 - CONV_CARRY:r0, :]
        wk = [w_ref[k * CONV_CARRY:(k + 1) * CONV_CARRY, :] for k in range(SSM_CONV)]
        taps = [cur * jnp.concatenate([wk[k]] * (L // CONV_CARRY), axis=0) for k in range(SSM_CONV)]
        carry = [old * wk[k] for k in range(SSM_CONV - 1)]
        pad = jnp.zeros((L - len(carry) * CONV_CARRY, cur.shape[1]), BF16)
        acc = b_ref[...] + _dot(shift_ref[...], jnp.concatenate(taps + carry + [pad], axis=0))
        return _silu(acc)

    h = [hstate[g] for g in range(n_groups)]
    for ci in range(chunks):
        r0 = ci * L
        xs = conv_silu(xs_ref, xprev, cwx_ref, cbx_ref, r0)
        bc = conv_silu(bc_ref, bcprev, cwbc_ref, cbbc_ref, r0).astype(BF16)

        dt = dt_ref[r0:r0 + L, :]
        a_cs = _dot(tri_ref[...], jnp.concatenate(_split3(dt * a_neg), axis=0))
        a_cs_t = a_cs.T
        ea = jnp.exp2(a_cs)
        dte = jnp.exp2(a_cs[L - 1:L, :] - a_cs)
        stacked = jnp.concatenate([dt, ea, dte], axis=0).astype(BF16)
        expanded = _dot(stacked, e_ref[...])
        dt_e = expanded[0:L]
        ea_e = expanded[L:2 * L]
        dte_e = expanded[2 * L:3 * L]

        xdt = xs * dt_e
        xdt_b = xdt.astype(BF16)
        xw_b = (xdt * dte_e).astype(BF16)
        gate = _silu(z_ref[r0:r0 + L, :].astype(F32))

        for g in range(n_groups):
            b_g = bc[:, g * d_state:(g + 1) * d_state]
            c_g = bc[:, (n_groups + g) * d_state:(n_groups + g + 1) * d_state]
            cb = _dot_nt(c_g, b_g)
            y_diag = []
            for p2 in range(pairs_per_group):
                pair = g * pairs_per_group + p2
                ms = []
                for hh in (2 * pair, 2 * pair + 1):
                    seg = a_cs[:, hh:hh + 1] - a_cs_t[hh:hh + 1, :]
                    dec = jnp.exp2(jnp.where(causal, seg, -jnp.inf))
                    ms.append((cb * dec).astype(BF16))
                lhs = jnp.concatenate(ms, axis=1)
                xp = xdt_b[:, pair * LANES:(pair + 1) * LANES]
                zero = jnp.zeros_like(xp)
                rhs = jnp.concatenate([jnp.where(lane < head_dim, xp, zero),
                                       jnp.where(lane >= head_dim, xp, zero)], axis=0)
                y_diag.append(_dot(lhs, rhs))
            sl = slice(g * gw, (g + 1) * gw)
            ea_g = ea_e[:, sl]
            y_off = _dot(c_g, h[g].astype(BF16)) * ea_g
            h[g] = h[g] * ea_g[L - 1:L, :] + _dot(b_g.T, xw_b[:, sl])
            yg = (jnp.concatenate(y_diag, axis=1) + y_off + xs[:, sl] * dskip_ref[:, sl]) * gate[:, sl]
            y_ref[r0:r0 + L, sl] = _rms(yg, nw_ref[:, sl]).astype(y_ref.dtype)

    for g in range(n_groups):
        hstate[g] = h[g]
    xprev[...] = xs_ref[chunks * L - CONV_CARRY:chunks * L, :]
    bcprev[...] = bc_ref[chunks * L - CONV_CARRY:chunks * L, :]


def _ssd(proj, dt, cw_x, cb_x, cw_bc, cb_bc, a_log, d_skip_e, norm_w, e_mat, tri, shift,
         *, batch, seq, d_inner, col_xs, col_z, col_bc, chunks=2):
    L = SSM_CHUNK
    rows = chunks * L
    nc = seq // rows
    t = batch * seq
    bc_w = 2 * SSM_N_GROUPS * SSM_D_STATE
    row = lambda b, c: b * nc + c
    kern = functools.partial(_ssd_kernel, n_groups=SSM_N_GROUPS, d_state=SSM_D_STATE,
                             head_dim=SSM_HEAD_DIM, chunks=chunks)
    return pl.pallas_call(
        kern,
        grid=(batch, nc),
        in_specs=[
            pl.BlockSpec((rows, d_inner), lambda b, c: (row(b, c), col_xs // d_inner)),
            pl.BlockSpec((rows, d_inner), lambda b, c: (row(b, c), col_z // d_inner)),
            pl.BlockSpec((rows, bc_w), lambda b, c: (row(b, c), col_bc // bc_w)),
            pl.BlockSpec((rows, LANES), lambda b, c: (row(b, c), 0)),
            pl.BlockSpec((SSM_CONV * CONV_CARRY, d_inner), lambda b, c: (0, 0)),
            pl.BlockSpec((1, d_inner), lambda b, c: (0, 0)),
            pl.BlockSpec((SSM_CONV * CONV_CARRY, bc_w), lambda b, c: (0, 0)),
            pl.BlockSpec((1, bc_w), lambda b, c: (0, 0)),
            pl.BlockSpec((1, LANES), lambda b, c: (0, 0)),
            pl.BlockSpec((1, d_inner), lambda b, c: (0, 0)),
            pl.BlockSpec((1, d_inner), lambda b, c: (0, 0)),
            pl.BlockSpec((LANES, d_inner), lambda b, c: (0, 0)),
            pl.BlockSpec((L, 3 * L), lambda b, c: (0, 0)),
            pl.BlockSpec(shift.shape, lambda b, c: (0, 0)),
        ],
        out_specs=pl.BlockSpec((rows, d_inner), lambda b, c: (row(b, c), 0)),
        out_shape=jax.ShapeDtypeStruct((t, d_inner), BF16),
        scratch_shapes=[
            pltpu.VMEM((CONV_CARRY, d_inner), BF16),
            pltpu.VMEM((CONV_CARRY, bc_w), BF16),
            pltpu.VMEM((SSM_N_GROUPS, SSM_D_STATE, d_inner // SSM_N_GROUPS), F32),
        ],
        compiler_params=pltpu.CompilerParams(
            dimension_semantics=("parallel", "arbitrary"),
            vmem_limit_bytes=40 * MIB),
        name="ssd",
    )(proj, proj, proj, dt, cw_x, cb_x, cw_bc, cb_bc, a_log, d_skip_e, norm_w, e_mat, tri, shift)


def _attn_kernel(q_ref, k_ref, v_ref, lq1_ref, lk1_ref, lq2_ref, lk2_ref, subw_ref,
                 o_ref, vt_scr, *, tq, tk, lam_init):
    seq = q_ref.shape[0]
    for jb in range(seq // tk):
        vt_scr[jb] = v_ref[jb * tk:(jb + 1) * tk, :].T

    lam = (jnp.exp(jnp.sum(lq1_ref[...] * lk1_ref[...]))
           - jnp.exp(jnp.sum(lq2_ref[...] * lk2_ref[...])) + lam_init)
    subw = subw_ref[...] * (1.0 - lam_init)

    def scores(qi, j):
        q = q_ref[qi * tq:(qi + 1) * tq, :]
        lane = lax.broadcasted_iota(jnp.int32, q.shape, 1)
        zero = jnp.zeros_like(q)
        q2 = jnp.concatenate([jnp.where(lane < ATT_HEAD_DIM, q, zero),
                              jnp.where(lane >= ATT_HEAD_DIM, q, zero)], axis=0)
        return _dot_nt(k_ref[j * tk:(j + 1) * tk, :], q2)

    steps = [(qi, j) for qi in range(seq // tq) for j in range((qi + 1) * tq // tk)]
    s_next = scores(*steps[0])
    m = l = acc = None
    for t, (qi, j) in enumerate(steps):
        s = s_next
        if t + 1 < len(steps):
            s_next = scores(*steps[t + 1])
        if (j + 1) * tk > qi * tq:
            kv = lax.broadcasted_iota(jnp.int32, s.shape, 0) + (j * tk - qi * tq)
            r = lax.broadcasted_iota(jnp.int32, s.shape, 1)
            r = jnp.where(r >= tq, r - tq, r)
            s = jnp.where(kv <= r, s, -jnp.inf)
        s_max = jnp.max(s, axis=0, keepdims=True)
        if j == 0:
            m = s_max
            p = jnp.exp2(s - m)
            l = jnp.sum(p, axis=0, keepdims=True)
            acc = _dot(vt_scr[j], p.astype(BF16))
        else:
            m_new = jnp.maximum(m, s_max)
            alpha = jnp.exp2(m - m_new)
            p = jnp.exp2(s - m_new)
            l = alpha * l + jnp.sum(p, axis=0, keepdims=True)
            acc = alpha * acc + _dot(vt_scr[j], p.astype(BF16))
            m = m_new
        if (j + 1) * tk == (qi + 1) * tq:
            o = acc / l
            od = o[:, 0:tq] - lam * o[:, tq:2 * tq]
            od = od * lax.rsqrt(jnp.mean(od * od, axis=0, keepdims=True) + RMS_EPS)
            o_ref[qi * tq:(qi + 1) * tq, :] = (od * subw).T.astype(o_ref.dtype)


def _attention(proj, lq1, lk1, lq2, lk2, subw, *, batch, seq, n_heads, col_q, col_k, col_v,
               lam_init, tq=512, tk=512):
    t = batch * seq
    hd = ATT_V_DIM
    kern = functools.partial(_attn_kernel, tq=tq, tk=tk, lam_init=lam_init)
    small = pl.BlockSpec((1, ATT_HEAD_DIM), lambda b, h: (0, 0))
    return pl.pallas_call(
        kern,
        grid=(batch, n_heads),
        in_specs=[
            pl.BlockSpec((seq, hd), lambda b, h: (b, col_q // hd + h)),
            pl.BlockSpec((seq, hd), lambda b, h: (b, col_k // hd + h)),
            pl.BlockSpec((seq, hd), lambda b, h: (b, col_v // hd + h)),
            small, small, small, small,
            pl.BlockSpec((hd, 1), lambda b, h: (0, 0)),
        ],
        out_specs=pl.BlockSpec((seq, hd), lambda b, h: (b, h)),
        out_shape=jax.ShapeDtypeStruct((t, n_heads * hd), BF16),
        scratch_shapes=[pltpu.VMEM((seq // tk, hd, tk), BF16)],
        compiler_params=pltpu.CompilerParams(
            dimension_semantics=("parallel", "parallel"),
            vmem_limit_bytes=32 * MIB),
        name="diff_attn",
    )(proj, proj, proj, lq1, lk1, lq2, lk2, subw)


def _mix_kernel(x_ref, ys_ref, ya_ref, npre_ref, npost_ref, wg_ref, wso_ref, wao_ref, wmix_ref, o_ref):
    x = x_ref[...]
    d = x.shape[1]
    h = _rms(x, npre_ref[...]).astype(BF16)
    gates = jax.nn.sigmoid(_dot_nt(h, wg_ref[...]))
    y_ssm = _dot(ys_ref[...], wso_ref[...])
    y_att = _dot(ya_ref[...], wao_ref[...])
    blend = gates[:, 0:d] * y_ssm + gates[:, d:2 * d] * y_att
    mixed = _dot(blend.astype(BF16), wmix_ref[...])
    o_ref[...] = x + _rms(mixed, npost_ref[...])


def _mix(x2d, y_ssm, y_att, n_pre, n_post, w_gate, w_so, w_ao, w_mix, *, tm=512):
    t, d = x2d.shape
    rows = lambda w: pl.BlockSpec((tm, w), lambda i: (i, 0))
    return pl.pallas_call(
        _mix_kernel,
        grid=(t // tm,),
        in_specs=[rows(d), rows(y_ssm.shape[1]), rows(y_att.shape[1]),
                  _const_spec(n_pre.shape), _const_spec(n_post.shape),
                  _const_spec(w_gate.shape), _const_spec(w_so.shape),
                  _const_spec(w_ao.shape), _const_spec(w_mix.shape)],
        out_specs=rows(d),
        out_shape=jax.ShapeDtypeStruct((t, d), F32),
        compiler_params=pltpu.CompilerParams(
            dimension_semantics=("parallel",), vmem_limit_bytes=48 * MIB),
        name="mix",
    )(x2d, y_ssm, y_att, n_pre, n_post, w_gate, w_so, w_ao, w_mix)


def _ffn_kernel(x_ref, npre_ref, npost_ref, wg_ref, wu_ref, wd_ref, o_ref):
    x = x_ref[...]
    h = _rms(x, npre_ref[...]).astype(BF16)
    act = (_silu(_dot(h, wg_ref[...])) * _dot(h, wu_ref[...])).astype(BF16)
    f = _dot(act, wd_ref[...])
    o_ref[...] = x + _rms(f, npost_ref[...])


def _ffn(x2d, n_pre, n_post, w_g, w_u, w_d, *, tm=512):
    t, d = x2d.shape
    rows = pl.BlockSpec((tm, d), lambda i: (i, 0))
    return pl.pallas_call(
        _ffn_kernel,
        grid=(t // tm,),
        in_specs=[rows, _const_spec(n_pre.shape), _const_spec(n_post.shape),
                  _const_spec(w_g.shape), _const_spec(w_u.shape), _const_spec(w_d.shape)],
        out_specs=rows,
        out_shape=jax.ShapeDtypeStruct((t, d), F32),
        compiler_params=pltpu.CompilerParams(
            dimension_semantics=("parallel",), vmem_limit_bytes=56 * MIB),
        name="ffn",
    )(x2d, n_pre, n_post, w_g, w_u, w_d)


def kernel(x, positions, w_in, conv_w, conv_b, dt_bias, a_log, d_skip, ssm_norm_w, w_ssm_out, lam_q1, lam_k1, lam_q2, lam_k2, attn_subln_w, w_attn_out, w_mix_out, norm_pre_mix, norm_post_mix, norm_pre_ffn, norm_post_ffn, w_ffn_gate, w_ffn_up, w_ffn_down):
    batch, seq, d_model = x.shape
    depth = w_in.shape[0]
    d_inner = w_ssm_out.shape[1]
    n_ssm_heads = dt_bias.shape[1]
    bc_w = 2 * SSM_N_GROUPS * SSM_D_STATE
    att_w = w_attn_out.shape[1]
    n_att_heads = att_w // ATT_V_DIM
    t = batch * seq

    z_end = d_inner
    xbc_end = z_end + d_inner + bc_w
    dt_end = xbc_end + n_ssm_heads
    q_end = dt_end + att_w
    k_end = q_end + att_w
    v_end = k_end + att_w

    col_z, col_xs = 0, d_inner
    col_bc = 2 * d_inner
    col_q = col_bc + bc_w
    col_k = col_q + att_w
    col_v = col_k + att_w

    inv_freq = ROPE_THETA ** (-jnp.arange(0, ROPE_DIM, 2, dtype=F32) / ROPE_DIM)
    head_of_col = np.arange(d_inner) // SSM_HEAD_DIM
    assert n_ssm_heads <= LANES
    e_mat = jnp.asarray(np.arange(LANES)[:, None] == head_of_col[None, :], BF16)
    tri = np.arange(SSM_CHUNK)[None, :] <= np.arange(SSM_CHUNK)[:, None]
    tri = jnp.asarray(np.concatenate([tri, tri, tri], axis=1), BF16)
    L = SSM_CHUNK
    t_out = np.arange(L)[:, None, None]
    tap = np.arange(SSM_CONV)[None, :, None]
    src = t_out - (SSM_CONV - 1) + tap
    src_col = np.where(src >= 0, tap * L + src, SSM_CONV * L + tap * CONV_CARRY + CONV_CARRY + src)
    shift = jnp.asarray((np.arange((SSM_CONV + 1) * L)[None, None, :] == src_col).any(axis=1), BF16)

    def pad_heads(v):
        return jnp.pad(v.astype(F32), (0, LANES - n_ssm_heads))[None, :]

    x2d = x.reshape(t, d_model)
    for l in range(depth):
        lam_init = 0.8 - 0.6 * math.exp(-0.3 * l)
        wl = w_in[l]
        w_t = wl.T.astype(BF16)
        w_gate_t = w_t[v_end:]

        proj, dt = _in_proj(x2d, positions, norm_pre_mix[l][None, :], w_t, pad_heads(dt_bias[l]), inv_freq,
                            n_out=col_v + att_w, row_dt=xbc_end, n_dt=n_ssm_heads, row_q=dt_end,
                            col_q=col_q, col_k=col_k, col_v=col_v)

        cw = jnp.repeat(conv_w[l].astype(BF16), CONV_CARRY, axis=0)
        y_ssm = _ssd(proj, dt,
                     cw[:, :d_inner], conv_b[l][None, :d_inner],
                     cw[:, d_inner:], conv_b[l][None, d_inner:],
                     pad_heads(a_log[l]), jnp.repeat(d_skip[l], SSM_HEAD_DIM)[None, :],
                     ssm_norm_w[l][None, :], e_mat, tri, shift,
                     batch=batch, seq=seq, d_inner=d_inner, col_xs=col_xs, col_z=col_z, col_bc=col_bc)

        y_att = _attention(proj, lam_q1[l][None, :], lam_k1[l][None, :], lam_q2[l][None, :],
                           lam_k2[l][None, :], attn_subln_w[l][:, None],
                           batch=batch, seq=seq, n_heads=n_att_heads,
                           col_q=col_q, col_k=col_k, col_v=col_v, lam_init=lam_init)

        x2d = _mix(x2d, y_ssm, y_att, norm_pre_mix[l][None, :], norm_post_mix[l][None, :],
                   w_gate_t, w_ssm_out[l].astype(BF16), w_attn_out[l].astype(BF16),
                   w_mix_out[l].astype(BF16))
        x2d = _ffn(x2d, norm_pre_ffn[l][None, :], norm_post_ffn[l][None, :],
                   w_ffn_gate[l].astype(BF16), w_ffn_up[l].astype(BF16), w_ffn_down[l].astype(BF16))
    return x2d.reshape(batch, seq, d_model)
```

```python
import functools
import math

import jax
import jax.numpy as jnp
import numpy as np
from jax import lax
from jax.experimental import pallas as pl
from jax.experimental.pallas import tpu as pltpu

F32 = jnp.float32
BF16 = jnp.bfloat16

LANES = 128

SSM_HEAD_DIM = 64
SSM_N_GROUPS = 4
SSM_D_STATE = 128
SSM_CONV = 4
SSM_CHUNK = 128
CONV_CARRY = 16
ATT_HEAD_DIM = 64
ATT_V_DIM = 2 * ATT_HEAD_DIM
ROPE_THETA = 500000.0
ROPE_DIM = ATT_HEAD_DIM // 4
RMS_EPS = 1e-6
Q_SCALE = ATT_HEAD_DIM ** -0.5 * math.log2(math.e)

MIB = 1024 * 1024


def _rms(x, w):
    return x * lax.rsqrt(jnp.mean(x * x, axis=-1, keepdims=True) + RMS_EPS) * w


def _silu(x):
    hx = 0.5 * x
    return hx + hx * jnp.tanh(hx)


def _dot(a, b):
    return jnp.dot(a, b, preferred_element_type=F32)


def _dot_nt(a, b):
    return lax.dot_general(a, b, (((1,), (1,)), ((), ())), preferred_element_type=F32)


def _split3(x):
    hi = x.astype(BF16)
    r = x - hi.astype(F32)
    mid = r.astype(BF16)
    lo = (r - mid.astype(F32)).astype(BF16)
    return hi, mid, lo


def _const_spec(shape):
    nd = len(shape)
    return pl.BlockSpec(shape, lambda *_: (0,) * nd, pipeline_mode=pl.Buffered(1))


def _inproj_kernel(x_ref, pos_ref, nw_ref, wt_ref, dtb_ref, invf_ref, rope_e_ref, rope_c_ref,
                   out_ref, dt_ref, h_scr, cos_scr, s1_scr, s2_scr,
                   *, row_dt, n_dt, row_q, col_q, col_k, col_v, sub):
    hb = _rms(x_ref[...], nw_ref[...]).astype(BF16)
    h_scr[...] = hb
    dt_lane = lax.broadcasted_iota(jnp.int32, dt_ref.shape, 1)
    dt = jax.nn.softplus(_dot_nt(hb, wt_ref[row_dt:row_dt + LANES, :]) + dtb_ref[...])
    dt_ref[...] = jnp.where(dt_lane < n_dt, dt, 0.0)
    half = ROPE_DIM // 2
    ang = invf_ref[...] * pos_ref[...].astype(F32)
    terms = [t.astype(F32) for t in _split3(jnp.cos(ang)) + _split3(jnp.sin(ang))]
    fill = jnp.zeros((LANES - len(terms) * half, ang.shape[1]), F32)
    grid_t = jnp.concatenate(terms + [fill], axis=0).T.astype(BF16)
    tables = _dot(grid_t, rope_e_ref[...])
    cos_scr[...] = tables[:, 0:LANES] + rope_c_ref[...]
    s1_scr[...] = tables[:, LANES:2 * LANES]
    s2_scr[...] = tables[:, 2 * LANES:3 * LANES]

    for c0 in range(0, out_ref.shape[1], sub):
        r0 = c0 if c0 < col_q else c0 - col_q + row_q
        acc = _dot_nt(h_scr[...], wt_ref[r0:r0 + sub, :])
        if col_q <= c0 < col_v:
            scale = Q_SCALE if c0 < col_k else 1.0
            for c in range(c0, c0 + sub, LANES):
                a = acc[:, c - c0:c - c0 + LANES]
                r = (a * cos_scr[...] + pltpu.roll(a, half, 1) * s1_scr[...]
                     + pltpu.roll(a, LANES - half, 1) * s2_scr[...])
                out_ref[:, c:c + LANES] = (r * scale).astype(out_ref.dtype)
        else:
            out_ref[:, c0:c0 + sub] = acc.astype(out_ref.dtype)


def _in_proj(x2d, positions, norm_w, w_t, dt_bias, inv_freq, *, n_out, row_dt, n_dt, row_q,
             col_q, col_k, col_v, tm=512, sub=512):
    t, d = x2d.shape
    assert all(c % sub == 0 for c in (col_q, col_k, col_v, n_out))
    half = ROPE_DIM // 2
    lane = np.arange(LANES) % ATT_HEAD_DIM
    freq_row = np.arange(LANES) % half
    is_cos = np.arange(LANES) < 3 * half
    is_sin = (np.arange(LANES) >= 3 * half) & (np.arange(LANES) < 6 * half)
    same_f = freq_row[:, None] == (lane % half)[None, :]
    rope_e = np.concatenate([
        (is_cos[:, None] & same_f & (lane < ROPE_DIM)[None, :]) * 1.0,
        (is_sin[:, None] & same_f & ((lane >= half) & (lane < ROPE_DIM))[None, :]) * 1.0,
        (is_sin[:, None] & same_f & (lane < half)[None, :]) * -1.0], axis=1)
    rope_c = (lane >= ROPE_DIM)[None, :] * 1.0
    kern = functools.partial(_inproj_kernel, row_dt=row_dt, n_dt=n_dt, row_q=row_q,
                             col_q=col_q, col_k=col_k, col_v=col_v, sub=sub)
    return pl.pallas_call(
        kern,
        grid=(t // tm,),
        in_specs=[
            pl.BlockSpec((tm, d), lambda i: (i, 0)),
            pl.BlockSpec((None, 1, tm), lambda i: (i, 0, 0)),
            _const_spec((1, d)),
            _const_spec(w_t.shape),
            _const_spec((1, LANES)),
            _const_spec((half, 1)),
            _const_spec(rope_e.shape),
            _const_spec(rope_c.shape),
        ],
        out_specs=[
            pl.BlockSpec((tm, n_out), lambda i: (i, 0)),
            pl.BlockSpec((tm, LANES), lambda i: (i, 0)),
        ],
        out_shape=[
            jax.ShapeDtypeStruct((t, n_out), BF16),
            jax.ShapeDtypeStruct((t, LANES), F32),
        ],
        scratch_shapes=[
            pltpu.VMEM((tm, d), BF16),
            pltpu.VMEM((tm, LANES), F32),
            pltpu.VMEM((tm, LANES), F32),
            pltpu.VMEM((tm, LANES), F32),
        ],
        compiler_params=pltpu.CompilerParams(
            dimension_semantics=("parallel",),
            vmem_limit_bytes=56 * MIB),
        name="in_proj",
    )(x2d, positions.reshape(t // tm, 1, tm), norm_w, w_t, dt_bias, inv_freq[:, None],
      jnp.asarray(rope_e, BF16), jnp.asarray(rope_c, F32))


def _ssd_kernel(xs_ref, z_ref, bc_ref, dt_ref, cwx_ref, cbx_ref, cwbc_ref, cbbc_ref,
                alog_ref, dskip_ref, nw_ref, e_ref, tri_ref, shift_ref,
                y_ref, xprev, bcprev, hstate, *, n_groups, d_state, head_dim, chunks):
    L = SSM_CHUNK
    d_inner = xs_ref.shape[1]
    gw = d_inner // n_groups
    heads_per_group = gw // head_dim
    pairs_per_group = heads_per_group // 2

    @pl.when(pl.program_id(1) == 0)
    def _():
        xprev[...] = jnp.zeros(xprev.shape, BF16)
        bcprev[...] = jnp.zeros(bcprev.shape, BF16)
        hstate[...] = jnp.zeros(hstate.shape, F32)

    a_neg = -jnp.exp(alog_ref[...]) * math.log2(math.e)
    row = lax.broadcasted_iota(jnp.int32, (L, L), 0)
    col = lax.broadcasted_iota(jnp.int32, (L, L), 1)
    causal = col <= row
    lane = lax.broadcasted_iota(jnp.int32, (L, LANES), 1)

    def conv_silu(cur_ref, prev, w_ref, b_ref, r0):
        cur = cur_ref[r0:r0 + L, :]
        old = prev[...] if r0 == 0 else cur_ref[r0 - CONV_CARRY:r0, :]
        wk = [w_ref[k * CONV_CARRY:(k + 1) * CONV_CARRY, :] for k in range(SSM_CONV)]
        taps = [cur * jnp.concatenate([wk[k]] * (L // CONV_CARRY), axis=0) for k in range(SSM_CONV)]
        carry = [old * wk[k] for k in range(SSM_CONV - 1)]
        pad = jnp.zeros((L - len(carry) * CONV_CARRY, cur.shape[1]), BF16)
        acc = b_ref[...] + _dot(shift_ref[...], jnp.concatenate(taps + carry + [pad], axis=0))
        return _silu(acc)

    h = [hstate[g] for g in range(n_groups)]
    for ci in range(chunks):
        r0 = ci * L
        xs = conv_silu(xs_ref, xprev, cwx_ref, cbx_ref, r0)
        bc = conv_silu(bc_ref, bcprev, cwbc_ref, cbbc_ref, r0).astype(BF16)

        dt = dt_ref[r0:r0 + L, :]
        a_cs = _dot(tri_ref[...], jnp.concatenate(_split3(dt * a_neg), axis=0))
        a_cs_t = a_cs.T
        ea = jnp.exp2(a_cs)
        dte = jnp.exp2(a_cs[L - 1:L, :] - a_cs)
        stacked = jnp.concatenate([dt, ea, dte], axis=0).astype(BF16)
        expanded = _dot(stacked, e_ref[...])
        dt_e = expanded[0:L]
        ea_e = expanded[L:2 * L]
        dte_e = expanded[2 * L:3 * L]

        xdt = xs * dt_e
        xdt_b = xdt.astype(BF16)
        xw_b = (xdt * dte_e).astype(BF16)
        gate = _silu(z_ref[r0:r0 + L, :].astype(F32))

        for g in range(n_groups):
            b_g = bc[:, g * d_state:(g + 1) * d_state]
            c_g = bc[:, (n_groups + g) * d_state:(n_groups + g + 1) * d_state]
            cb = _dot_nt(c_g, b_g)
            y_diag = []
            for p2 in range(pairs_per_group):
                pair = g * pairs_per_group + p2
                ms = []
                for hh in (2 * pair, 2 * pair + 1):
                    seg = a_cs[:, hh:hh + 1] - a_cs_t[hh:hh + 1, :]
                    dec = jnp.exp2(jnp.where(causal, seg, -jnp.inf))
                    ms.append((cb * dec).astype(BF16))
                lhs = jnp.concatenate(ms, axis=1)
                xp = xdt_b[:, pair * LANES:(pair + 1) * LANES]
                zero = jnp.zeros_like(xp)
                rhs = jnp.concatenate([jnp.where(lane < head_dim, xp, zero),
                                       jnp.where(lane >= head_dim, xp, zero)], axis=0)
                y_diag.append(_dot(lhs, rhs))
            sl = slice(g * gw, (g + 1) * gw)
            ea_g = ea_e[:, sl]
            y_off = _dot(c_g, h[g].astype(BF16)) * ea_g
            h[g] = h[g] * ea_g[L - 1:L, :] + _dot(b_g.T, xw_b[:, sl])
            yg = (jnp.concatenate(y_diag, axis=1) + y_off + xs[:, sl] * dskip_ref[:, sl]) * gate[:, sl]
            y_ref[r0:r0 + L, sl] = _rms(yg, nw_ref[:, sl]).astype(y_ref.dtype)

    for g in range(n_groups):
        hstate[g] = h[g]
    xprev[...] = xs_ref[chunks * L - CONV_CARRY:chunks * L, :]
    bcprev[...] = bc_ref[chunks * L - CONV_CARRY:chunks * L, :]


def _ssd(proj, dt, cw_x, cb_x, cw_bc, cb_bc, a_log, d_skip_e, norm_w, e_mat, tri, shift,
         *, batch, seq, d_inner, col_xs, col_z, col_bc, chunks=4):
    L = SSM_CHUNK
    rows = chunks * L
    nc = seq // rows
    t = batch * seq
    bc_w = 2 * SSM_N_GROUPS * SSM_D_STATE
    row = lambda b, c: b * nc + c
    kern = functools.partial(_ssd_kernel, n_groups=SSM_N_GROUPS, d_state=SSM_D_STATE,
                             head_dim=SSM_HEAD_DIM, chunks=chunks)
    return pl.pallas_call(
        kern,
        grid=(batch, nc),
        in_specs=[
            pl.BlockSpec((rows, d_inner), lambda b, c: (row(b, c), col_xs // d_inner)),
            pl.BlockSpec((rows, d_inner), lambda b, c: (row(b, c), col_z // d_inner)),
            pl.BlockSpec((rows, bc_w), lambda b, c: (row(b, c), col_bc // bc_w)),
            pl.BlockSpec((rows, LANES), lambda b, c: (row(b, c), 0)),
            pl.BlockSpec((SSM_CONV * CONV_CARRY, d_inner), lambda b, c: (0, 0)),
            pl.BlockSpec((1, d_inner), lambda b, c: (0, 0)),
            pl.BlockSpec((SSM_CONV * CONV_CARRY, bc_w), lambda b, c: (0, 0)),
            pl.BlockSpec((1, bc_w), lambda b, c: (0, 0)),
            pl.BlockSpec((1, LANES), lambda b, c: (0, 0)),
            pl.BlockSpec((1, d_inner), lambda b, c: (0, 0)),
            pl.BlockSpec((1, d_inner), lambda b, c: (0, 0)),
            pl.BlockSpec((LANES, d_inner), lambda b, c: (0, 0)),
            pl.BlockSpec((L, 3 * L), lambda b, c: (0, 0)),
            pl.BlockSpec(shift.shape, lambda b, c: (0, 0)),
        ],
        out_specs=pl.BlockSpec((rows, d_inner), lambda b, c: (row(b, c), 0)),
        out_shape=jax.ShapeDtypeStruct((t, d_inner), BF16),
        scratch_shapes=[
            pltpu.VMEM((CONV_CARRY, d_inner), BF16),
            pltpu.VMEM((CONV_CARRY, bc_w), BF16),
            pltpu.VMEM((SSM_N_GROUPS, SSM_D_STATE, d_inner // SSM_N_GROUPS), F32),
        ],
        compiler_params=pltpu.CompilerParams(
            dimension_semantics=("parallel", "arbitrary"),
            vmem_limit_bytes=40 * MIB),
        name="ssd",
    )(proj, proj, proj, dt, cw_x, cb_x, cw_bc, cb_bc, a_log, d_skip_e, norm_w, e_mat, tri, shift)


def _attn_kernel(q_ref, k_ref, v_ref, lq1_ref, lk1_ref, lq2_ref, lk2_ref, subw_ref,
                 o_ref, vt_scr, *, tq, tk, lam_init):
    seq = q_ref.shape[0]
    hd = v_ref.shape[1]
    for jb in range(seq // tk):
        vt_scr[jb, 0:hd, :] = v_ref[jb * tk:(jb + 1) * tk, :].T
        vt_scr[jb, hd:, :] = jnp.ones((vt_scr.shape[1] - hd, tk), BF16)

    lam = (jnp.exp(jnp.sum(lq1_ref[...] * lk1_ref[...]))
           - jnp.exp(jnp.sum(lq2_ref[...] * lk2_ref[...])) + lam_init)
    subw = subw_ref[...] * (1.0 - lam_init)

    def scores(qi, j):
        q = q_ref[qi * tq:(qi + 1) * tq, :]
        lane = lax.broadcasted_iota(jnp.int32, q.shape, 1)
        zero = jnp.zeros_like(q)
        q2 = jnp.concatenate([jnp.where(lane < ATT_HEAD_DIM, q, zero),
                              jnp.where(lane >= ATT_HEAD_DIM, q, zero)], axis=0)
        return _dot_nt(k_ref[j * tk:(j + 1) * tk, :], q2)

    steps = [(qi, j) for qi in range(seq // tq) for j in range((qi + 1) * tq // tk)]
    s_next = scores(*steps[0])
    m = acc = None
    for t, (qi, j) in enumerate(steps):
        s = s_next
        if t + 1 < len(steps):
            s_next = scores(*steps[t + 1])
        if (j + 1) * tk > qi * tq:
            kv = lax.broadcasted_iota(jnp.int32, s.shape, 0) + (j * tk - qi * tq)
            r = lax.broadcasted_iota(jnp.int32, s.shape, 1)
            r = jnp.where(r >= tq, r - tq, r)
            s = jnp.where(kv <= r, s, -jnp.inf)
        s_max = jnp.max(s, axis=0, keepdims=True)
        if j == 0:
            m = s_max
            acc = _dot(vt_scr[j], jnp.exp2(s - m).astype(BF16))
        else:
            m_new = jnp.maximum(m, s_max)
            alpha = jnp.exp2(m - m_new)
            acc = alpha * acc + _dot(vt_scr[j], jnp.exp2(s - m_new).astype(BF16))
            m = m_new
        if (j + 1) * tk == (qi + 1) * tq:
            o = acc[0:hd] / acc[hd:hd + 1]
            od = o[:, 0:tq] - lam * o[:, tq:2 * tq]
            od = od * lax.rsqrt(jnp.mean(od * od, axis=0, keepdims=True) + RMS_EPS)
            o_ref[qi * tq:(qi + 1) * tq, :] = (od * subw).T.astype(o_ref.dtype)


def _attention(proj, lq1, lk1, lq2, lk2, subw, *, batch, seq, n_heads, col_q, col_k, col_v,
               lam_init, tq=512, tk=512):
    t = batch * seq
    hd = ATT_V_DIM
    kern = functools.partial(_attn_kernel, tq=tq, tk=tk, lam_init=lam_init)
    small = pl.BlockSpec((1, ATT_HEAD_DIM), lambda b, h: (0, 0))
    return pl.pallas_call(
        kern,
        grid=(batch, n_heads),
        in_specs=[
            pl.BlockSpec((seq, hd), lambda b, h: (b, col_q // hd + h)),
            pl.BlockSpec((seq, hd), lambda b, h: (b, col_k // hd + h)),
            pl.BlockSpec((seq, hd), lambda b, h: (b, col_v // hd + h)),
            small, small, small, small,
            pl.BlockSpec((hd, 1), lambda b, h: (0, 0)),
        ],
        out_specs=pl.BlockSpec((seq, hd), lambda b, h: (b, h)),
        out_shape=jax.ShapeDtypeStruct((t, n_heads * hd), BF16),
        scratch_shapes=[pltpu.VMEM((seq // tk, hd + 16, tk), BF16)],
        compiler_params=pltpu.CompilerParams(
            dimension_semantics=("parallel", "parallel"),
            vmem_limit_bytes=32 * MIB),
        name="diff_attn",
    )(proj, proj, proj, lq1, lk1, lq2, lk2, subw)


def _mix_kernel(x_ref, ys_ref, ya_ref, npre_ref, npost_ref, wg_ref, wso_ref, wao_ref, wmix_ref, o_ref,
                *, parts):
    d = x_ref.shape[1]
    rows = x_ref.shape[0] // parts

    def blend(i):
        r = slice(i * rows, (i + 1) * rows)
        h = _rms(x_ref[r, :], npre_ref[...]).astype(BF16)
        gates = jax.nn.sigmoid(_dot_nt(h, wg_ref[...]))
        y_ssm = _dot(ys_ref[r, :], wso_ref[...])
        y_att = _dot(ya_ref[r, :], wao_ref[...])
        return (gates[:, 0:d] * y_ssm + gates[:, d:2 * d] * y_att).astype(BF16)

    blends = [blend(i) for i in range(parts)]
    for i in range(parts):
        r = slice(i * rows, (i + 1) * rows)
        mixed = _dot(blends[i], wmix_ref[...])
        o_ref[r, :] = x_ref[r, :] + _rms(mixed, npost_ref[...])


def _mix(x2d, y_ssm, y_att, n_pre, n_post, w_gate, w_so, w_ao, w_mix, *, tm=1024, parts=2):
    t, d = x2d.shape
    rows = lambda w: pl.BlockSpec((tm, w), lambda i: (i, 0))
    return pl.pallas_call(
        functools.partial(_mix_kernel, parts=parts),
        grid=(t // tm,),
        in_specs=[rows(d), rows(y_ssm.shape[1]), rows(y_att.shape[1]),
                  _const_spec(n_pre.shape), _const_spec(n_post.shape),
                  _const_spec(w_gate.shape), _const_spec(w_so.shape),
                  _const_spec(w_ao.shape), _const_spec(w_mix.shape)],
        out_specs=rows(d),
        out_shape=jax.ShapeDtypeStruct((t, d), F32),
        compiler_params=pltpu.CompilerParams(
            dimension_semantics=("parallel",), vmem_limit_bytes=56 * MIB),
        name="mix",
    )(x2d, y_ssm, y_att, n_pre, n_post, w_gate, w_so, w_ao, w_mix)


def _ffn_kernel(x_ref, npre_ref, npost_ref, wg_ref, wu_ref, wd_ref, o_ref, *, parts):
    rows = x_ref.shape[0] // parts

    def hidden(i):
        h = _rms(x_ref[i * rows:(i + 1) * rows, :], npre_ref[...]).astype(BF16)
        return (_silu(_dot(h, wg_ref[...])) * _dot(h, wu_ref[...])).astype(BF16)

    acts = [hidden(i) for i in range(parts)]
    for i in range(parts):
        r = slice(i * rows, (i + 1) * rows)
        f = _dot(acts[i], wd_ref[...])
        o_ref[r, :] = x_ref[r, :] + _rms(f, npost_ref[...])


def _ffn(x2d, n_pre, n_post, w_g, w_u, w_d, *, tm=1024, parts=2):
    t, d = x2d.shape
    rows = pl.BlockSpec((tm, d), lambda i: (i, 0))
    return pl.pallas_call(
        functools.partial(_ffn_kernel, parts=parts),
        grid=(t // tm,),
        in_specs=[rows, _const_spec(n_pre.shape), _const_spec(n_post.shape),
                  _const_spec(w_g.shape), _const_spec(w_u.shape), _const_spec(w_d.shape)],
        out_specs=rows,
        out_shape=jax.ShapeDtypeStruct((t, d), F32),
        compiler_params=pltpu.CompilerParams(
            dimension_semantics=("parallel",), vmem_limit_bytes=56 * MIB),
        name="ffn",
    )(x2d, n_pre, n_post, w_g, w_u, w_d)


def kernel(x, positions, w_in, conv_w, conv_b, dt_bias, a_log, d_skip, ssm_norm_w, w_ssm_out, lam_q1, lam_k1, lam_q2, lam_k2, attn_subln_w, w_attn_out, w_mix_out, norm_pre_mix, norm_post_mix, norm_pre_ffn, norm_post_ffn, w_ffn_gate, w_ffn_up, w_ffn_down):
    batch, seq, d_model = x.shape
    depth = w_in.shape[0]
    d_inner = w_ssm_out.shape[1]
    n_ssm_heads = dt_bias.shape[1]
    bc_w = 2 * SSM_N_GROUPS * SSM_D_STATE
    att_w = w_attn_out.shape[1]
    n_att_heads = att_w // ATT_V_DIM
    t = batch * seq

    z_end = d_inner
    xbc_end = z_end + d_inner + bc_w
    dt_end = xbc_end + n_ssm_heads
    q_end = dt_end + att_w
    k_end = q_end + att_w
    v_end = k_end + att_w

    col_z, col_xs = 0, d_inner
    col_bc = 2 * d_inner
    col_q = col_bc + bc_w
    col_k = col_q + att_w
    col_v = col_k + att_w

    inv_freq = ROPE_THETA ** (-jnp.arange(0, ROPE_DIM, 2, dtype=F32) / ROPE_DIM)
    head_of_col = np.arange(d_inner) // SSM_HEAD_DIM
    assert n_ssm_heads <= LANES
    e_mat = jnp.asarray(np.arange(LANES)[:, None] == head_of_col[None, :], BF16)
    tri = np.arange(SSM_CHUNK)[None, :] <= np.arange(SSM_CHUNK)[:, None]
    tri = jnp.asarray(np.concatenate([tri, tri, tri], axis=1), BF16)
    L = SSM_CHUNK
    t_out = np.arange(L)[:, None, None]
    tap = np.arange(SSM_CONV)[None, :, None]
    src = t_out - (SSM_CONV - 1) + tap
    src_col = np.where(src >= 0, tap * L + src, SSM_CONV * L + tap * CONV_CARRY + CONV_CARRY + src)
    shift = jnp.asarray((np.arange((SSM_CONV + 1) * L)[None, None, :] == src_col).any(axis=1), BF16)

    def pad_heads(v):
        return jnp.pad(v.astype(F32), (0, LANES - n_ssm_heads))[None, :]

    x2d = x.reshape(t, d_model)
    for l in range(depth):
        lam_init = 0.8 - 0.6 * math.exp(-0.3 * l)
        wl = w_in[l]
        w_t = wl.T.astype(BF16)
        w_gate_t = w_t[v_end:]

        proj, dt = _in_proj(x2d, positions, norm_pre_mix[l][None, :], w_t, pad_heads(dt_bias[l]), inv_freq,
                            n_out=col_v + att_w, row_dt=xbc_end, n_dt=n_ssm_heads, row_q=dt_end,
                            col_q=col_q, col_k=col_k, col_v=col_v)

        cw = jnp.repeat(conv_w[l].astype(BF16), CONV_CARRY, axis=0)
        y_ssm = _ssd(proj, dt,
                     cw[:, :d_inner], conv_b[l][None, :d_inner],
                     cw[:, d_inner:], conv_b[l][None, d_inner:],
                     pad_heads(a_log[l]), jnp.repeat(d_skip[l], SSM_HEAD_DIM)[None, :],
                     ssm_norm_w[l][None, :], e_mat, tri, shift,
                     batch=batch, seq=seq, d_inner=d_inner, col_xs=col_xs, col_z=col_z, col_bc=col_bc)

        y_att = _attention(proj, lam_q1[l][None, :], lam_k1[l][None, :], lam_q2[l][None, :],
                           lam_k2[l][None, :], attn_subln_w[l][:, None],
                           batch=batch, seq=seq, n_heads=n_att_heads,
                           col_q=col_q, col_k=col_k, col_v=col_v, lam_init=lam_init)

        x2d = _mix(x2d, y_ssm, y_att, norm_pre_mix[l][None, :], norm_post_mix[l][None, :],
                   w_gate_t, w_ssm_out[l].astype(BF16), w_attn_out[l].astype(BF16),
                   w_mix_out[l].astype(BF16))
        x2d = _ffn(x2d, norm_pre_ffn[l][None, :], norm_post_ffn[l][None, :],
                   w_ffn_gate[l].astype(BF16), w_ffn_up[l].astype(BF16), w_ffn_down[l].astype(BF16))
    return x2d.reshape(batch, seq, d_model)
```

```python
import functools
import math

import jax
import jax.numpy as jnp
import numpy as np
from jax import lax
from jax.experimental import pallas as pl
from jax.experimental.pallas import tpu as pltpu

F32 = jnp.float32
BF16 = jnp.bfloat16

LANES = 128

SSM_HEAD_DIM = 64
SSM_N_GROUPS = 4
SSM_D_STATE = 128
SSM_CONV = 4
SSM_CHUNK = 128
CONV_CARRY = 16
ATT_HEAD_DIM = 64
ATT_V_DIM = 2 * ATT_HEAD_DIM
ROPE_THETA = 500000.0
ROPE_DIM = ATT_HEAD_DIM // 4
RMS_EPS = 1e-6
Q_SCALE = ATT_HEAD_DIM ** -0.5 * math.log2(math.e)

MIB = 1024 * 1024


def _rms(x, w):
    return x * lax.rsqrt(jnp.mean(x * x, axis=-1, keepdims=True) + RMS_EPS) * w


def _silu(x):
    hx = 0.5 * x
    return hx + hx * jnp.tanh(hx)


def _dot(a, b):
    return jnp.dot(a, b, preferred_element_type=F32)


def _dot_nt(a, b):
    return lax.dot_general(a, b, (((1,), (1,)), ((), ())), preferred_element_type=F32)


def _split3(x):
    hi = x.astype(BF16)
    r = x - hi.astype(F32)
    mid = r.astype(BF16)
    lo = (r - mid.astype(F32)).astype(BF16)
    return hi, mid, lo


def _const_spec(shape):
    nd = len(shape)
    return pl.BlockSpec(shape, lambda *_: (0,) * nd, pipeline_mode=pl.Buffered(1))


def _inproj_kernel(x_ref, pos_ref, nw_ref, wt_ref, dtb_ref, invf_ref, rope_e_ref, rope_c_ref,
                   *refs, n_cast, row_dt, n_dt, row_q, row_tail, col_q, col_k, col_v, sub):
    cast_in = refs[:n_cast]
    out_ref, dt_ref, tail_ref = refs[n_cast:n_cast + 3]
    cast_out = refs[n_cast + 3:2 * n_cast + 3]
    h_scr, cos_scr, s1_scr, s2_scr = refs[2 * n_cast + 3:]
    for src, dst in zip(cast_in, cast_out):
        dst[...] = src[...].astype(dst.dtype)
    tail_rows = tail_ref.shape[0]
    tail_start = pl.multiple_of(row_tail + pl.program_id(0) * tail_rows, tail_rows)
    tail_ref[...] = wt_ref[pl.ds(tail_start, tail_rows), :]

    hb = _rms(x_ref[...], nw_ref[...]).astype(BF16)
    h_scr[...] = hb
    dt_lane = lax.broadcasted_iota(jnp.int32, dt_ref.shape, 1)
    dt = jax.nn.softplus(_dot_nt(hb, wt_ref[row_dt:row_dt + LANES, :]) + dtb_ref[...])
    dt_ref[...] = jnp.where(dt_lane < n_dt, dt, 0.0)
    half = ROPE_DIM // 2
    ang = invf_ref[...] * pos_ref[...].astype(F32)
    terms = [t.astype(F32) for t in _split3(jnp.cos(ang)) + _split3(jnp.sin(ang))]
    fill = jnp.zeros((LANES - len(terms) * half, ang.shape[1]), F32)
    grid_t = jnp.concatenate(terms + [fill], axis=0).T.astype(BF16)
    tables = _dot(grid_t, rope_e_ref[...])
    cos_scr[...] = tables[:, 0:LANES] + rope_c_ref[...]
    s1_scr[...] = tables[:, LANES:2 * LANES]
    s2_scr[...] = tables[:, 2 * LANES:3 * LANES]

    for c0 in range(0, out_ref.shape[1], sub):
        r0 = c0 if c0 < col_q else c0 - col_q + row_q
        acc = _dot_nt(h_scr[...], wt_ref[r0:r0 + sub, :])
        if col_q <= c0 < col_v:
            scale = Q_SCALE if c0 < col_k else 1.0
            for c in range(c0, c0 + sub, LANES):
                a = acc[:, c - c0:c - c0 + LANES]
                r = (a * cos_scr[...] + pltpu.roll(a, half, 1) * s1_scr[...]
                     + pltpu.roll(a, LANES - half, 1) * s2_scr[...])
                out_ref[:, c:c + LANES] = (r * scale).astype(out_ref.dtype)
        else:
            out_ref[:, c0:c0 + sub] = acc.astype(out_ref.dtype)


def _in_proj(x2d, positions, norm_w, w_t, dt_bias, inv_freq, casts, *, n_out, row_dt, n_dt, row_q,
             row_tail, col_q, col_k, col_v, tm=512, sub=512):
    t, d = x2d.shape
    steps = t // tm
    assert all(c % sub == 0 for c in (col_q, col_k, col_v, n_out))
    tail_rows = (w_t.shape[0] - row_tail) // steps
    assert tail_rows * steps == w_t.shape[0] - row_tail and tail_rows % 16 == 0 and row_tail % 16 == 0

    def cast_spec(w):
        hold = next(h for h in (1, 2, 4, 8) if (w.shape[0] * h) % (16 * steps) == 0)
        return pl.BlockSpec((w.shape[0] * hold // steps, w.shape[1]), lambda i: (i // hold, 0))

    cast_specs = [cast_spec(w) for w in casts]
    half = ROPE_DIM // 2
    lane = np.arange(LANES) % ATT_HEAD_DIM
    freq_row = np.arange(LANES) % half
    is_cos = np.arange(LANES) < 3 * half
    is_sin = (np.arange(LANES) >= 3 * half) & (np.arange(LANES) < 6 * half)
    same_f = freq_row[:, None] == (lane % half)[None, :]
    rope_e = np.concatenate([
        (is_cos[:, None] & same_f & (lane < ROPE_DIM)[None, :]) * 1.0,
        (is_sin[:, None] & same_f & ((lane >= half) & (lane < ROPE_DIM))[None, :]) * 1.0,
        (is_sin[:, None] & same_f & (lane < half)[None, :]) * -1.0], axis=1)
    rope_c = (lane >= ROPE_DIM)[None, :] * 1.0
    kern = functools.partial(_inproj_kernel, n_cast=len(casts), row_dt=row_dt, n_dt=n_dt, row_q=row_q,
                             row_tail=row_tail, col_q=col_q, col_k=col_k, col_v=col_v, sub=sub)
    return pl.pallas_call(
        kern,
        grid=(t // tm,),
        in_specs=[
            pl.BlockSpec((tm, d), lambda i: (i, 0)),
            pl.BlockSpec((None, 1, tm), lambda i: (i, 0, 0)),
            _const_spec((1, d)),
            _const_spec(w_t.shape),
            _const_spec((1, LANES)),
            _const_spec((half, 1)),
            _const_spec(rope_e.shape),
            _const_spec(rope_c.shape),
        ] + cast_specs,
        out_specs=[
            pl.BlockSpec((tm, n_out), lambda i: (i, 0)),
            pl.BlockSpec((tm, LANES), lambda i: (i, 0)),
            pl.BlockSpec((tail_rows, d), lambda i: (i, 0)),
        ] + cast_specs,
        out_shape=[
            jax.ShapeDtypeStruct((t, n_out), BF16),
            jax.ShapeDtypeStruct((t, LANES), F32),
            jax.ShapeDtypeStruct((w_t.shape[0] - row_tail, d), BF16),
        ] + [jax.ShapeDtypeStruct(w.shape, BF16) for w in casts],
        scratch_shapes=[
            pltpu.VMEM((tm, d), BF16),
            pltpu.VMEM((tm, LANES), F32),
            pltpu.VMEM((tm, LANES), F32),
            pltpu.VMEM((tm, LANES), F32),
        ],
        compiler_params=pltpu.CompilerParams(
            dimension_semantics=("parallel",),
            vmem_limit_bytes=56 * MIB),
        name="in_proj",
    )(x2d, positions.reshape(t // tm, 1, tm), norm_w, w_t, dt_bias, inv_freq[:, None],
      jnp.asarray(rope_e, BF16), jnp.asarray(rope_c, F32), *casts)


def _ssd_kernel(xs_ref, z_ref, bc_ref, dt_ref, cwx_ref, cbx_ref, cwbc_ref, cbbc_ref,
                alog_ref, dskip_ref, nw_ref, e_ref, tri_ref, shift_ref,
                y_ref, xprev, bcprev, hstate, *, n_groups, d_state, head_dim, chunks):
    L = SSM_CHUNK
    d_inner = xs_ref.shape[1]
    gw = d_inner // n_groups
    heads_per_group = gw // head_dim
    pairs_per_group = heads_per_group // 2

    @pl.when(pl.program_id(1) == 0)
    def _():
        xprev[...] = jnp.zeros(xprev.shape, BF16)
        bcprev[...] = jnp.zeros(bcprev.shape, BF16)
        hstate[...] = jnp.zeros(hstate.shape, F32)

    a_neg = -jnp.exp(alog_ref[...]) * math.log2(math.e)
    row = lax.broadcasted_iota(jnp.int32, (L, L), 0)
    col = lax.broadcasted_iota(jnp.int32, (L, L), 1)
    causal = col <= row
    lane = lax.broadcasted_iota(jnp.int32, (L, LANES), 1)

    def conv_silu(cur_ref, prev, w_ref, b_ref, r0):
        cur = cur_ref[r0:r0 + L, :]
        old = prev[...] if r0 == 0 else cur_ref[r0 - CONV_CARRY:r0, :]
        wk = [w_ref[k * CONV_CARRY:(k + 1) * CONV_CARRY, :] for k in range(SSM_CONV)]
        taps = [cur * jnp.concatenate([wk[k]] * (L // CONV_CARRY), axis=0) for k in range(SSM_CONV)]
        carry = [old * wk[k] for k in range(SSM_CONV - 1)]
        pad = jnp.zeros((L - len(carry) * CONV_CARRY, cur.shape[1]), BF16)
        acc = b_ref[...] + _dot(shift_ref[...], jnp.concatenate(taps + carry + [pad], axis=0))
        return _silu(acc)

    h = [hstate[g] for g in range(n_groups)]
    for ci in range(chunks):
        r0 = ci * L
        xs = conv_silu(xs_ref, xprev, cwx_ref, cbx_ref, r0)
        bc = conv_silu(bc_ref, bcprev, cwbc_ref, cbbc_ref, r0).astype(BF16)

        dt = dt_ref[r0:r0 + L, :]
        a_cs = _dot(tri_ref[...], jnp.concatenate(_split3(dt * a_neg), axis=0))
        a_cs_t = a_cs.T
        ea = jnp.exp2(a_cs)
        dte = jnp.exp2(a_cs[L - 1:L, :] - a_cs)
        stacked = jnp.concatenate([dt, ea, dte], axis=0).astype(BF16)
        expanded = _dot(stacked, e_ref[...])
        dt_e = expanded[0:L]
        ea_e = expanded[L:2 * L]
        dte_e = expanded[2 * L:3 * L]

        xdt = xs * dt_e
        xdt_b = xdt.astype(BF16)
        xw_b = (xdt * dte_e).astype(BF16)
        gate = _silu(z_ref[r0:r0 + L, :].astype(F32))

        for g in range(n_groups):
            b_g = bc[:, g * d_state:(g + 1) * d_state]
            c_g = bc[:, (n_groups + g) * d_state:(n_groups + g + 1) * d_state]
            cb = _dot_nt(c_g, b_g)
            y_diag = []
            for p2 in range(pairs_per_group):
                pair = g * pairs_per_group + p2
                ms = []
                for hh in (2 * pair, 2 * pair + 1):
                    seg = a_cs[:, hh:hh + 1] - a_cs_t[hh:hh + 1, :]
                    dec = jnp.exp2(jnp.where(causal, seg, -jnp.inf))
                    ms.append((cb * dec).astype(BF16))
                lhs = jnp.concatenate(ms, axis=1)
                xp = xdt_b[:, pair * LANES:(pair + 1) * LANES]
                zero = jnp.zeros_like(xp)
                rhs = jnp.concatenate([jnp.where(lane < head_dim, xp, zero),
                                       jnp.where(lane >= head_dim, xp, zero)], axis=0)
                y_diag.append(_dot(lhs, rhs))
            sl = slice(g * gw, (g + 1) * gw)
            ea_g = ea_e[:, sl]
            y_off = _dot(c_g, h[g].astype(BF16)) * ea_g
            h[g] = h[g] * ea_g[L - 1:L, :] + _dot(b_g.T, xw_b[:, sl])
            yg = (jnp.concatenate(y_diag, axis=1) + y_off + xs[:, sl] * dskip_ref[:, sl]) * gate[:, sl]
            y_ref[r0:r0 + L, sl] = _rms(yg, nw_ref[:, sl]).astype(y_ref.dtype)

    for g in range(n_groups):
        hstate[g] = h[g]
    xprev[...] = xs_ref[chunks * L - CONV_CARRY:chunks * L, :]
    bcprev[...] = bc_ref[chunks * L - CONV_CARRY:chunks * L, :]


def _ssd(proj, dt, cw_x, cb_x, cw_bc, cb_bc, a_log, d_skip_e, norm_w, e_mat, tri, shift,
         *, batch, seq, d_inner, col_xs, col_z, col_bc, chunks=2):
    L = SSM_CHUNK
    rows = chunks * L
    nc = seq // rows
    t = batch * seq
    bc_w = 2 * SSM_N_GROUPS * SSM_D_STATE
    row = lambda b, c: b * nc + c
    kern = functools.partial(_ssd_kernel, n_groups=SSM_N_GROUPS, d_state=SSM_D_STATE,
                             head_dim=SSM_HEAD_DIM, chunks=chunks)
    return pl.pallas_call(
        kern,
        grid=(batch, nc),
        in_specs=[
            pl.BlockSpec((rows, d_inner), lambda b, c: (row(b, c), col_xs // d_inner)),
            pl.BlockSpec((rows, d_inner), lambda b, c: (row(b, c), col_z // d_inner)),
            pl.BlockSpec((rows, bc_w), lambda b, c: (row(b, c), col_bc // bc_w)),
            pl.BlockSpec((rows, LANES), lambda b, c: (row(b, c), 0)),
            pl.BlockSpec((SSM_CONV * CONV_CARRY, d_inner), lambda b, c: (0, 0)),
            pl.BlockSpec((1, d_inner), lambda b, c: (0, 0)),
            pl.BlockSpec((SSM_CONV * CONV_CARRY, bc_w), lambda b, c: (0, 0)),
            pl.BlockSpec((1, bc_w), lambda b, c: (0, 0)),
            pl.BlockSpec((1, LANES), lambda b, c: (0, 0)),
            pl.BlockSpec((1, d_inner), lambda b, c: (0, 0)),
            pl.BlockSpec((1, d_inner), lambda b, c: (0, 0)),
            pl.BlockSpec((LANES, d_inner), lambda b, c: (0, 0)),
            pl.BlockSpec((L, 3 * L), lambda b, c: (0, 0)),
            pl.BlockSpec(shift.shape, lambda b, c: (0, 0)),
        ],
        out_specs=pl.BlockSpec((rows, d_inner), lambda b, c: (row(b, c), 0)),
        out_shape=jax.ShapeDtypeStruct((t, d_inner), BF16),
        scratch_shapes=[
            pltpu.VMEM((CONV_CARRY, d_inner), BF16),
            pltpu.VMEM((CONV_CARRY, bc_w), BF16),
            pltpu.VMEM((SSM_N_GROUPS, SSM_D_STATE, d_inner // SSM_N_GROUPS), F32),
        ],
        compiler_params=pltpu.CompilerParams(
            dimension_semantics=("parallel", "arbitrary"),
            vmem_limit_bytes=40 * MIB),
        name="ssd",
    )(proj, proj, proj, dt, cw_x, cb_x, cw_bc, cb_bc, a_log, d_skip_e, norm_w, e_mat, tri, shift)


def _attn_kernel(q_ref, k_ref, v_ref, lq1_ref, lk1_ref, lq2_ref, lk2_ref, subw_ref,
                 o_ref, vt_scr, *, tq, tk, lam_init):
    seq = q_ref.shape[0]
    for jb in range(seq // tk):
        vt_scr[jb] = v_ref[jb * tk:(jb + 1) * tk, :].T

    lam = (jnp.exp(jnp.sum(lq1_ref[...] * lk1_ref[...]))
           - jnp.exp(jnp.sum(lq2_ref[...] * lk2_ref[...])) + lam_init)
    subw = subw_ref[...] * (1.0 - lam_init)

    def scores(qi, j):
        q = q_ref[qi * tq:(qi + 1) * tq, :]
        lane = lax.broadcasted_iota(jnp.int32, q.shape, 1)
        zero = jnp.zeros_like(q)
        q2 = jnp.concatenate([jnp.where(lane < ATT_HEAD_DIM, q, zero),
                              jnp.where(lane >= ATT_HEAD_DIM, q, zero)], axis=0)
        return _dot_nt(k_ref[j * tk:(j + 1) * tk, :], q2)

    steps = [(qi, j) for qi in range(seq // tq) for j in range((qi + 1) * tq // tk)]
    s_next = scores(*steps[0])
    m = l = acc = None
    for t, (qi, j) in enumerate(steps):
        s = s_next
        if t + 1 < len(steps):
            s_next = scores(*steps[t + 1])
        if (j + 1) * tk > qi * tq:
            kv = lax.broadcasted_iota(jnp.int32, s.shape, 0) + (j * tk - qi * tq)
            r = lax.broadcasted_iota(jnp.int32, s.shape, 1)
            r = jnp.where(r >= tq, r - tq, r)
            s = jnp.where(kv <= r, s, -jnp.inf)
        s_max = jnp.max(s, axis=0, keepdims=True)
        if j == 0:
            m = s_max
            p = jnp.exp2(s - m)
            l = jnp.sum(p, axis=0, keepdims=True)
            acc = _dot(vt_scr[j], p.astype(BF16))
        else:
            m_new = jnp.maximum(m, s_max)
            alpha = jnp.exp2(m - m_new)
            p = jnp.exp2(s - m_new)
            l = alpha * l + jnp.sum(p, axis=0, keepdims=True)
            acc = alpha * acc + _dot(vt_scr[j], p.astype(BF16))
            m = m_new
        if (j + 1) * tk == (qi + 1) * tq:
            o = acc / l
            od = o[:, 0:tq] - lam * o[:, tq:2 * tq]
            od = od * lax.rsqrt(jnp.mean(od * od, axis=0, keepdims=True) + RMS_EPS)
            o_ref[qi * tq:(qi + 1) * tq, :] = (od * subw).T.astype(o_ref.dtype)


def _attention(proj, lq1, lk1, lq2, lk2, subw, *, batch, seq, n_heads, col_q, col_k, col_v,
               lam_init, tq=512, tk=512):
    t = batch * seq
    hd = ATT_V_DIM
    kern = functools.partial(_attn_kernel, tq=tq, tk=tk, lam_init=lam_init)
    small = pl.BlockSpec((1, ATT_HEAD_DIM), lambda b, h: (0, 0))
    return pl.pallas_call(
        kern,
        grid=(batch, n_heads),
        in_specs=[
            pl.BlockSpec((seq, hd), lambda b, h: (b, col_q // hd + h)),
            pl.BlockSpec((seq, hd), lambda b, h: (b, col_k // hd + h)),
            pl.BlockSpec((seq, hd), lambda b, h: (b, col_v // hd + h)),
            small, small, small, small,
            pl.BlockSpec((hd, 1), lambda b, h: (0, 0)),
        ],
        out_specs=pl.BlockSpec((seq, hd), lambda b, h: (b, h)),
        out_shape=jax.ShapeDtypeStruct((t, n_heads * hd), BF16),
        scratch_shapes=[pltpu.VMEM((seq // tk, hd, tk), BF16)],
        compiler_params=pltpu.CompilerParams(
            dimension_semantics=("parallel", "parallel"),
            vmem_limit_bytes=32 * MIB),
        name="diff_attn",
    )(proj, proj, proj, lq1, lk1, lq2, lk2, subw)


def _mix_kernel(x_ref, ys_ref, ya_ref, npre_ref, npost_ref, wg_ref, wso_ref, wao_ref, wmix_ref, o_ref):
    x = x_ref[...]
    d = x.shape[1]
    h = _rms(x, npre_ref[...]).astype(BF16)
    gates = jax.nn.sigmoid(_dot_nt(h, wg_ref[...]))
    y_ssm = _dot(ys_ref[...], wso_ref[...])
    y_att = _dot(ya_ref[...], wao_ref[...])
    blend = gates[:, 0:d] * y_ssm + gates[:, d:2 * d] * y_att
    mixed = _dot(blend.astype(BF16), wmix_ref[...])
    o_ref[...] = x + _rms(mixed, npost_ref[...])


def _mix(x2d, y_ssm, y_att, n_pre, n_post, w_gate, w_so, w_ao, w_mix, *, tm=512):
    t, d = x2d.shape
    rows = lambda w: pl.BlockSpec((tm, w), lambda i: (i, 0))
    return pl.pallas_call(
        _mix_kernel,
        grid=(t // tm,),
        in_specs=[rows(d), rows(y_ssm.shape[1]), rows(y_att.shape[1]),
                  _const_spec(n_pre.shape), _const_spec(n_post.shape),
                  _const_spec(w_gate.shape), _const_spec(w_so.shape),
                  _const_spec(w_ao.shape), _const_spec(w_mix.shape)],
        out_specs=rows(d),
        out_shape=jax.ShapeDtypeStruct((t, d), F32),
        compiler_params=pltpu.CompilerParams(
            dimension_semantics=("parallel",), vmem_limit_bytes=48 * MIB),
        name="mix",
    )(x2d, y_ssm, y_att, n_pre, n_post, w_gate, w_so, w_ao, w_mix)


def _ffn_kernel(x_ref, npre_ref, npost_ref, wg_ref, wu_ref, wd_ref, o_ref):
    x = x_ref[...]
    h = _rms(x, npre_ref[...]).astype(BF16)
    act = (_silu(_dot(h, wg_ref[...])) * _dot(h, wu_ref[...])).astype(BF16)
    f = _dot(act, wd_ref[...])
    o_ref[...] = x + _rms(f, npost_ref[...])


def _ffn(x2d, n_pre, n_post, w_g, w_u, w_d, *, tm=512):
    t, d = x2d.shape
    rows = pl.BlockSpec((tm, d), lambda i: (i, 0))
    return pl.pallas_call(
        _ffn_kernel,
        grid=(t // tm,),
        in_specs=[rows, _const_spec(n_pre.shape), _const_spec(n_post.shape),
                  _const_spec(w_g.shape), _const_spec(w_u.shape), _const_spec(w_d.shape)],
        out_specs=rows,
        out_shape=jax.ShapeDtypeStruct((t, d), F32),
        compiler_params=pltpu.CompilerParams(
            dimension_semantics=("parallel",), vmem_limit_bytes=56 * MIB),
        name="ffn",
    )(x2d, n_pre, n_post, w_g, w_u, w_d)


def kernel(x, positions, w_in, conv_w, conv_b, dt_bias, a_log, d_skip, ssm_norm_w, w_ssm_out, lam_q1, lam_k1, lam_q2, lam_k2, attn_subln_w, w_attn_out, w_mix_out, norm_pre_mix, norm_post_mix, norm_pre_ffn, norm_post_ffn, w_ffn_gate, w_ffn_up, w_ffn_down):
    batch, seq, d_model = x.shape
    depth = w_in.shape[0]
    d_inner = w_ssm_out.shape[1]
    n_ssm_heads = dt_bias.shape[1]
    bc_w = 2 * SSM_N_GROUPS * SSM_D_STATE
    att_w = w_attn_out.shape[1]
    n_att_heads = att_w // ATT_V_DIM
    t = batch * seq

    z_end = d_inner
    xbc_end = z_end + d_inner + bc_w
    dt_end = xbc_end + n_ssm_heads
    q_end = dt_end + att_w
    k_end = q_end + att_w
    v_end = k_end + att_w

    col_z, col_xs = 0, d_inner
    col_bc = 2 * d_inner
    col_q = col_bc + bc_w
    col_k = col_q + att_w
    col_v = col_k + att_w

    inv_freq = ROPE_THETA ** (-jnp.arange(0, ROPE_DIM, 2, dtype=F32) / ROPE_DIM)
    head_of_col = np.arange(d_inner) // SSM_HEAD_DIM
    assert n_ssm_heads <= LANES
    e_mat = jnp.asarray(np.arange(LANES)[:, None] == head_of_col[None, :], BF16)
    tri = np.arange(SSM_CHUNK)[None, :] <= np.arange(SSM_CHUNK)[:, None]
    tri = jnp.asarray(np.concatenate([tri, tri, tri], axis=1), BF16)
    L = SSM_CHUNK
    t_out = np.arange(L)[:, None, None]
    tap = np.arange(SSM_CONV)[None, :, None]
    src = t_out - (SSM_CONV - 1) + tap
    src_col = np.where(src >= 0, tap * L + src, SSM_CONV * L + tap * CONV_CARRY + CONV_CARRY + src)
    shift = jnp.asarray((np.arange((SSM_CONV + 1) * L)[None, None, :] == src_col).any(axis=1), BF16)

    def pad_heads(v):
        return jnp.pad(v.astype(F32), (0, LANES - n_ssm_heads))[None, :]

    x2d = x.reshape(t, d_model)
    for l in range(depth):
        lam_init = 0.8 - 0.6 * math.exp(-0.3 * l)
        wl = w_in[l]
        w_t = wl.T.astype(BF16)
        later = [w_ssm_out[l], w_attn_out[l], w_mix_out[l], w_ffn_gate[l], w_ffn_up[l], w_ffn_down[l]]
        proj, dt, w_gate_t, w_so, w_ao, w_mix, w_fg, w_fu, w_fd = _in_proj(
            x2d, positions, norm_pre_mix[l][None, :], w_t, pad_heads(dt_bias[l]), inv_freq, later,
            n_out=col_v + att_w, row_dt=xbc_end, n_dt=n_ssm_heads, row_q=dt_end, row_tail=v_end,
            col_q=col_q, col_k=col_k, col_v=col_v)

        cw = jnp.repeat(conv_w[l].astype(BF16), CONV_CARRY, axis=0)
        y_ssm = _ssd(proj, dt,
                     cw[:, :d_inner], conv_b[l][None, :d_inner],
                     cw[:, d_inner:], conv_b[l][None, d_inner:],
                     pad_heads(a_log[l]), jnp.repeat(d_skip[l], SSM_HEAD_DIM)[None, :],
                     ssm_norm_w[l][None, :], e_mat, tri, shift,
                     batch=batch, seq=seq, d_inner=d_inner, col_xs=col_xs, col_z=col_z, col_bc=col_bc)

        y_att = _attention(proj, lam_q1[l][None, :], lam_k1[l][None, :], lam_q2[l][None, :],
                           lam_k2[l][None, :], attn_subln_w[l][:, None],
                           batch=batch, seq=seq, n_heads=n_att_heads,
                           col_q=col_q, col_k=col_k, col_v=col_v, lam_init=lam_init)

        x2d = _mix(x2d, y_ssm, y_att, norm_pre_mix[l][None, :], norm_post_mix[l][None, :],
                   w_gate_t, w_so, w_ao, w_mix)
        x2d = _ffn(x2d, norm_pre_ffn[l][None, :], norm_post_ffn[l][None, :], w_fg, w_fu, w_fd)
    return x2d.reshape(batch, seq, d_model)
```

```python
import functools
import math

import jax
import jax.numpy as jnp
import numpy as np
from jax import lax
from jax.experimental import pallas as pl
from jax.experimental.pallas import tpu as pltpu

F32 = jnp.float32
BF16 = jnp.bfloat16

LANES = 128

SSM_HEAD_DIM = 64
SSM_N_GROUPS = 4
SSM_D_STATE = 128
SSM_CONV = 4
SSM_CHUNK = 128
CONV_CARRY = 16
ATT_HEAD_DIM = 64
ATT_V_DIM = 2 * ATT_HEAD_DIM
ROPE_THETA = 500000.0
ROPE_DIM = ATT_HEAD_DIM // 4
RMS_EPS = 1e-6
Q_SCALE = ATT_HEAD_DIM ** -0.5 * math.log2(math.e)

MIB = 1024 * 1024
V7X_VMEM_BYTES = 64 * MIB
VMEM_RESERVE = 8 * MIB


def _nbytes(shape, dtype):
    return math.prod(shape) * jnp.dtype(dtype).itemsize


def _vmem_limit(pipelined, resident, live):
    need = 2 * pipelined + resident + live
    assert need <= V7X_VMEM_BYTES - VMEM_RESERVE, need
    return -(-need // MIB) * MIB


def _rms(x, w):
    return x * lax.rsqrt(jnp.mean(x * x, axis=-1, keepdims=True) + RMS_EPS) * w


def _silu(x):
    hx = 0.5 * x
    return hx + hx * jnp.tanh(hx)


def _dot(a, b):
    return jnp.dot(a, b, preferred_element_type=F32)


def _dot_nt(a, b):
    return lax.dot_general(a, b, (((1,), (1,)), ((), ())), preferred_element_type=F32)


def _split3(x):
    hi = x.astype(BF16)
    r = x - hi.astype(F32)
    mid = r.astype(BF16)
    lo = (r - mid.astype(F32)).astype(BF16)
    return hi, mid, lo


def _const_spec(shape):
    nd = len(shape)
    return pl.BlockSpec(shape, lambda *_: (0,) * nd, pipeline_mode=pl.Buffered(1))


def _inproj_kernel(x_ref, pos_ref, nw_ref, wt_ref, dtb_ref, invf_ref, rope_e_ref, rope_c_ref,
                   *refs, n_cast, row_dt, n_dt, row_q, row_tail, col_q, col_k, col_v, sub):
    cast_in = refs[:n_cast]
    out_ref, dt_ref, tail_ref = refs[n_cast:n_cast + 3]
    cast_out = refs[n_cast + 3:2 * n_cast + 3]
    h_scr, cos_scr, s1_scr, s2_scr = refs[2 * n_cast + 3:]
    for src, dst in zip(cast_in, cast_out):
        dst[...] = src[...].astype(dst.dtype)
    tail_rows = tail_ref.shape[0]
    tail_start = pl.multiple_of(row_tail + pl.program_id(0) * tail_rows, tail_rows)
    tail_ref[...] = wt_ref[pl.ds(tail_start, tail_rows), :]

    hb = _rms(x_ref[...], nw_ref[...]).astype(BF16)
    h_scr[...] = hb
    dt_lane = lax.broadcasted_iota(jnp.int32, dt_ref.shape, 1)
    dt = jax.nn.softplus(_dot_nt(hb, wt_ref[row_dt:row_dt + LANES, :]) + dtb_ref[...])
    dt_ref[...] = jnp.where(dt_lane < n_dt, dt, 0.0)
    half = ROPE_DIM // 2
    ang = invf_ref[...] * pos_ref[...].astype(F32)
    terms = [t.astype(F32) for t in _split3(jnp.cos(ang)) + _split3(jnp.sin(ang))]
    fill = jnp.zeros((LANES - len(terms) * half, ang.shape[1]), F32)
    grid_t = jnp.concatenate(terms + [fill], axis=0).T.astype(BF16)
    tables = _dot(grid_t, rope_e_ref[...])
    cos_scr[...] = tables[:, 0:LANES] + rope_c_ref[...]
    s1_scr[...] = tables[:, LANES:2 * LANES]
    s2_scr[...] = tables[:, 2 * LANES:3 * LANES]

    for c0 in range(0, out_ref.shape[1], sub):
        r0 = c0 if c0 < col_q else c0 - col_q + row_q
        acc = _dot_nt(h_scr[...], wt_ref[r0:r0 + sub, :])
        if col_q <= c0 < col_v:
            scale = Q_SCALE if c0 < col_k else 1.0
            for c in range(c0, c0 + sub, LANES):
                a = acc[:, c - c0:c - c0 + LANES]
                r = (a * cos_scr[...] + pltpu.roll(a, half, 1) * s1_scr[...]
                     + pltpu.roll(a, LANES - half, 1) * s2_scr[...])
                out_ref[:, c:c + LANES] = (r * scale).astype(out_ref.dtype)
        else:
            out_ref[:, c0:c0 + sub] = acc.astype(out_ref.dtype)


def _in_proj(x2d, positions, norm_w, w_t, dt_bias, inv_freq, casts, *, n_out, row_dt, n_dt, row_q,
             row_tail, col_q, col_k, col_v, tm=512, sub=512):
    t, d = x2d.shape
    steps = t // tm
    assert all(c % sub == 0 for c in (col_q, col_k, col_v, n_out))
    tail_rows = (w_t.shape[0] - row_tail) // steps
    assert tail_rows * steps == w_t.shape[0] - row_tail and tail_rows % 16 == 0 and row_tail % 16 == 0

    def cast_spec(w):
        hold = next(h for h in (1, 2, 4, 8) if (w.shape[0] * h) % (16 * steps) == 0)
        return pl.BlockSpec((w.shape[0] * hold // steps, w.shape[1]), lambda i: (i // hold, 0))

    cast_specs = [cast_spec(w) for w in casts]
    half = ROPE_DIM // 2
    lane = np.arange(LANES) % ATT_HEAD_DIM
    freq_row = np.arange(LANES) % half
    is_cos = np.arange(LANES) < 3 * half
    is_sin = (np.arange(LANES) >= 3 * half) & (np.arange(LANES) < 6 * half)
    same_f = freq_row[:, None] == (lane % half)[None, :]
    rope_e = np.concatenate([
        (is_cos[:, None] & same_f & (lane < ROPE_DIM)[None, :]) * 1.0,
        (is_sin[:, None] & same_f & ((lane >= half) & (lane < ROPE_DIM))[None, :]) * 1.0,
        (is_sin[:, None] & same_f & (lane < half)[None, :]) * -1.0], axis=1)
    rope_c = (lane >= ROPE_DIM)[None, :] * 1.0
    kern = functools.partial(_inproj_kernel, n_cast=len(casts), row_dt=row_dt, n_dt=n_dt, row_q=row_q,
                             row_tail=row_tail, col_q=col_q, col_k=col_k, col_v=col_v, sub=sub)
    return pl.pallas_call(
        kern,
        grid=(t // tm,),
        in_specs=[
            pl.BlockSpec((tm, d), lambda i: (i, 0)),
            pl.BlockSpec((None, 1, tm), lambda i: (i, 0, 0)),
            _const_spec((1, d)),
            _const_spec(w_t.shape),
            _const_spec((1, LANES)),
            _const_spec((half, 1)),
            _const_spec(rope_e.shape),
            _const_spec(rope_c.shape),
        ] + cast_specs,
        out_specs=[
            pl.BlockSpec((tm, n_out), lambda i: (i, 0)),
            pl.BlockSpec((tm, LANES), lambda i: (i, 0)),
            pl.BlockSpec((tail_rows, d), lambda i: (i, 0)),
        ] + cast_specs,
        out_shape=[
            jax.ShapeDtypeStruct((t, n_out), BF16),
            jax.ShapeDtypeStruct((t, LANES), F32),
            jax.ShapeDtypeStruct((w_t.shape[0] - row_tail, d), BF16),
        ] + [jax.ShapeDtypeStruct(w.shape, BF16) for w in casts],
        scratch_shapes=[
            pltpu.VMEM((tm, d), BF16),
            pltpu.VMEM((tm, LANES), F32),
            pltpu.VMEM((tm, LANES), F32),
            pltpu.VMEM((tm, LANES), F32),
        ],
        compiler_params=pltpu.CompilerParams(
            dimension_semantics=("arbitrary",),
            vmem_limit_bytes=_vmem_limit(
                pipelined=(_nbytes((tm, d), F32) + _nbytes((tm, n_out), BF16) + _nbytes((tm, LANES), F32)
                           + _nbytes((tail_rows, d), BF16)
                           + sum(_nbytes(s.block_shape, F32) + _nbytes(s.block_shape, BF16) for s in cast_specs)),
                resident=_nbytes(w_t.shape, BF16) + _nbytes((tm, d), BF16) + 3 * _nbytes((tm, LANES), F32),
                live=3 * _nbytes((tm, sub), F32))),
        name="in_proj",
    )(x2d, positions.reshape(t // tm, 1, tm), norm_w, w_t, dt_bias, inv_freq[:, None],
      jnp.asarray(rope_e, BF16), jnp.asarray(rope_c, F32), *casts)


def _ssd_kernel(xs_ref, z_ref, bc_ref, dt_ref, cwx_ref, cbx_ref, cwbc_ref, cbbc_ref,
                alog_ref, dskip_ref, nw_ref, e_ref, tri_ref, shift_ref,
                y_ref, xprev, bcprev, hstate, *, n_groups, d_state, head_dim, chunks):
    L = SSM_CHUNK
    d_inner = xs_ref.shape[1]
    gw = d_inner // n_groups
    heads_per_group = gw // head_dim
    pairs_per_group = heads_per_group // 2

    @pl.when(pl.program_id(1) == 0)
    def _():
        xprev[...] = jnp.zeros(xprev.shape, BF16)
        bcprev[...] = jnp.zeros(bcprev.shape, BF16)
        hstate[...] = jnp.zeros(hstate.shape, F32)

    a_neg = -jnp.exp(alog_ref[...]) * math.log2(math.e)
    row = lax.broadcasted_iota(jnp.int32, (L, L), 0)
    col = lax.broadcasted_iota(jnp.int32, (L, L), 1)
    causal = col <= row
    lane = lax.broadcasted_iota(jnp.int32, (L, LANES), 1)

    def conv_silu(cur_ref, prev, w_ref, b_ref, r0):
        cur = cur_ref[r0:r0 + L, :]
        old = prev[...] if r0 == 0 else cur_ref[r0 - CONV_CARRY:r0, :]
        wk = [w_ref[k * CONV_CARRY:(k + 1) * CONV_CARRY, :] for k in range(SSM_CONV)]
        taps = [cur * jnp.concatenate([wk[k]] * (L // CONV_CARRY), axis=0) for k in range(SSM_CONV)]
        carry = [old * wk[k] for k in range(SSM_CONV - 1)]
        pad = jnp.zeros((L - len(carry) * CONV_CARRY, cur.shape[1]), BF16)
        acc = b_ref[...] + _dot(shift_ref[...], jnp.concatenate(taps + carry + [pad], axis=0))
        return _silu(acc)

    h = [hstate[g] for g in range(n_groups)]
    for ci in range(chunks):
        r0 = ci * L
        xs = conv_silu(xs_ref, xprev, cwx_ref, cbx_ref, r0)
        bc = conv_silu(bc_ref, bcprev, cwbc_ref, cbbc_ref, r0).astype(BF16)

        dt = dt_ref[r0:r0 + L, :]
        a_cs = _dot(tri_ref[...], jnp.concatenate(_split3(dt * a_neg), axis=0))
        a_cs_t = a_cs.T
        ea = jnp.exp2(a_cs)
        dte = jnp.exp2(a_cs[L - 1:L, :] - a_cs)
        stacked = jnp.concatenate([dt, ea, dte], axis=0).astype(BF16)
        expanded = _dot(stacked, e_ref[...])
        dt_e = expanded[0:L]
        ea_e = expanded[L:2 * L]
        dte_e = expanded[2 * L:3 * L]

        xdt = xs * dt_e
        xdt_b = xdt.astype(BF16)
        xw_b = (xdt * dte_e).astype(BF16)
        gate = _silu(z_ref[r0:r0 + L, :].astype(F32))

        for g in range(n_groups):
            b_g = bc[:, g * d_state:(g + 1) * d_state]
            c_g = bc[:, (n_groups + g) * d_state:(n_groups + g + 1) * d_state]
            cb = _dot_nt(c_g, b_g)
            y_diag = []
            for p2 in range(pairs_per_group):
                pair = g * pairs_per_group + p2
                ms = []
                for hh in (2 * pair, 2 * pair + 1):
                    seg = a_cs[:, hh:hh + 1] - a_cs_t[hh:hh + 1, :]
                    dec = jnp.exp2(jnp.where(causal, seg, -jnp.inf))
                    ms.append((cb * dec).astype(BF16))
                lhs = jnp.concatenate(ms, axis=1)
                xp = xdt_b[:, pair * LANES:(pair + 1) * LANES]
                zero = jnp.zeros_like(xp)
                rhs = jnp.concatenate([jnp.where(lane < head_dim, xp, zero),
                                       jnp.where(lane >= head_dim, xp, zero)], axis=0)
                y_diag.append(_dot(lhs, rhs))
            sl = slice(g * gw, (g + 1) * gw)
            ea_g = ea_e[:, sl]
            y_off = _dot(c_g, h[g].astype(BF16)) * ea_g
            h[g] = h[g] * ea_g[L - 1:L, :] + _dot(b_g.T, xw_b[:, sl])
            yg = (jnp.concatenate(y_diag, axis=1) + y_off + xs[:, sl] * dskip_ref[:, sl]) * gate[:, sl]
            y_ref[r0:r0 + L, sl] = _rms(yg, nw_ref[:, sl]).astype(y_ref.dtype)

    for g in range(n_groups):
        hstate[g] = h[g]
    xprev[...] = xs_ref[chunks * L - CONV_CARRY:chunks * L, :]
    bcprev[...] = bc_ref[chunks * L - CONV_CARRY:chunks * L, :]


def _ssd(proj, dt, cw_x, cb_x, cw_bc, cb_bc, a_log, d_skip_e, norm_w, e_mat, tri, shift,
         *, batch, seq, d_inner, col_xs, col_z, col_bc, chunks=2):
    L = SSM_CHUNK
    rows = chunks * L
    nc = seq // rows
    t = batch * seq
    bc_w = 2 * SSM_N_GROUPS * SSM_D_STATE
    row = lambda b, c: b * nc + c
    kern = functools.partial(_ssd_kernel, n_groups=SSM_N_GROUPS, d_state=SSM_D_STATE,
                             head_dim=SSM_HEAD_DIM, chunks=chunks)
    return pl.pallas_call(
        kern,
        grid=(batch, nc),
        in_specs=[
            pl.BlockSpec((rows, d_inner), lambda b, c: (row(b, c), col_xs // d_inner)),
            pl.BlockSpec((rows, d_inner), lambda b, c: (row(b, c), col_z // d_inner)),
            pl.BlockSpec((rows, bc_w), lambda b, c: (row(b, c), col_bc // bc_w)),
            pl.BlockSpec((rows, LANES), lambda b, c: (row(b, c), 0)),
            pl.BlockSpec((SSM_CONV * CONV_CARRY, d_inner), lambda b, c: (0, 0)),
            pl.BlockSpec((1, d_inner), lambda b, c: (0, 0)),
            pl.BlockSpec((SSM_CONV * CONV_CARRY, bc_w), lambda b, c: (0, 0)),
            pl.BlockSpec((1, bc_w), lambda b, c: (0, 0)),
            pl.BlockSpec((1, LANES), lambda b, c: (0, 0)),
            pl.BlockSpec((1, d_inner), lambda b, c: (0, 0)),
            pl.BlockSpec((1, d_inner), lambda b, c: (0, 0)),
            pl.BlockSpec((LANES, d_inner), lambda b, c: (0, 0)),
            pl.BlockSpec((L, 3 * L), lambda b, c: (0, 0)),
            pl.BlockSpec(shift.shape, lambda b, c: (0, 0)),
        ],
        out_specs=pl.BlockSpec((rows, d_inner), lambda b, c: (row(b, c), 0)),
        out_shape=jax.ShapeDtypeStruct((t, d_inner), BF16),
        scratch_shapes=[
            pltpu.VMEM((CONV_CARRY, d_inner), BF16),
            pltpu.VMEM((CONV_CARRY, bc_w), BF16),
            pltpu.VMEM((SSM_N_GROUPS, SSM_D_STATE, d_inner // SSM_N_GROUPS), F32),
        ],
        compiler_params=pltpu.CompilerParams(
            dimension_semantics=("parallel", "arbitrary"),
            vmem_limit_bytes=_vmem_limit(
                pipelined=(3 * _nbytes((rows, d_inner), BF16) + _nbytes((rows, bc_w), BF16)
                           + _nbytes((rows, LANES), F32) + _nbytes(cw_x.shape, BF16) + _nbytes(cw_bc.shape, BF16)
                           + _nbytes(e_mat.shape, BF16) + _nbytes(tri.shape, BF16) + _nbytes(shift.shape, BF16)),
                resident=_nbytes((SSM_D_STATE, d_inner), F32),
                live=chunks * (_nbytes((3 * L, d_inner), F32) + _nbytes((L, d_inner + bc_w), F32)
                               + 2 * _nbytes((L, d_inner), F32)))),
        name="ssd",
    )(proj, proj, proj, dt, cw_x, cb_x, cw_bc, cb_bc, a_log, d_skip_e, norm_w, e_mat, tri, shift)


def _attn_kernel(q_ref, k_ref, v_ref, lq1_ref, lk1_ref, lq2_ref, lk2_ref, subw_ref,
                 o_ref, vt_scr, *, tq, tk, lam_init):
    seq = q_ref.shape[0]
    for jb in range(seq // tk):
        vt_scr[jb] = v_ref[jb * tk:(jb + 1) * tk, :].T

    lam = (jnp.exp(jnp.sum(lq1_ref[...] * lk1_ref[...]))
           - jnp.exp(jnp.sum(lq2_ref[...] * lk2_ref[...])) + lam_init)
    subw = subw_ref[...] * (1.0 - lam_init)

    def scores(qi, j):
        q = q_ref[qi * tq:(qi + 1) * tq, :]
        lane = lax.broadcasted_iota(jnp.int32, q.shape, 1)
        zero = jnp.zeros_like(q)
        q2 = jnp.concatenate([jnp.where(lane < ATT_HEAD_DIM, q, zero),
                              jnp.where(lane >= ATT_HEAD_DIM, q, zero)], axis=0)
        return _dot_nt(k_ref[j * tk:(j + 1) * tk, :], q2)

    steps = [(qi, j) for qi in range(seq // tq) for j in range((qi + 1) * tq // tk)]
    s_next = scores(*steps[0])
    m = l = acc = None
    for t, (qi, j) in enumerate(steps):
        s = s_next
        if t + 1 < len(steps):
            s_next = scores(*steps[t + 1])
        if (j + 1) * tk > qi * tq:
            kv = lax.broadcasted_iota(jnp.int32, s.shape, 0) + (j * tk - qi * tq)
            r = lax.broadcasted_iota(jnp.int32, s.shape, 1)
            r = jnp.where(r >= tq, r - tq, r)
            s = jnp.where(kv <= r, s, -jnp.inf)
        s_max = jnp.max(s, axis=0, keepdims=True)
        if j == 0:
            m = s_max
            p = jnp.exp2(s - m)
            l = jnp.sum(p, axis=0, keepdims=True)
            acc = _dot(vt_scr[j], p.astype(BF16))
        else:
            m_new = jnp.maximum(m, s_max)
            alpha = jnp.exp2(m - m_new)
            p = jnp.exp2(s - m_new)
            l = alpha * l + jnp.sum(p, axis=0, keepdims=True)
            acc = alpha * acc + _dot(vt_scr[j], p.astype(BF16))
            m = m_new
        if (j + 1) * tk == (qi + 1) * tq:
            o = acc / l
            od = o[:, 0:tq] - lam * o[:, tq:2 * tq]
            od = od * lax.rsqrt(jnp.mean(od * od, axis=0, keepdims=True) + RMS_EPS)
            o_ref[qi * tq:(qi + 1) * tq, :] = (od * subw).T.astype(o_ref.dtype)


def _attention(proj, lq1, lk1, lq2, lk2, subw, *, batch, seq, n_heads, col_q, col_k, col_v,
               lam_init, tq=512, tk=512):
    t = batch * seq
    hd = ATT_V_DIM
    kern = functools.partial(_attn_kernel, tq=tq, tk=tk, lam_init=lam_init)
    small = pl.BlockSpec((1, ATT_HEAD_DIM), lambda b, h: (0, 0))
    return pl.pallas_call(
        kern,
        grid=(batch, n_heads),
        in_specs=[
            pl.BlockSpec((seq, hd), lambda b, h: (b, col_q // hd + h)),
            pl.BlockSpec((seq, hd), lambda b, h: (b, col_k // hd + h)),
            pl.BlockSpec((seq, hd), lambda b, h: (b, col_v // hd + h)),
            small, small, small, small,
            pl.BlockSpec((hd, 1), lambda b, h: (0, 0)),
        ],
        out_specs=pl.BlockSpec((seq, hd), lambda b, h: (b, h)),
        out_shape=jax.ShapeDtypeStruct((t, n_heads * hd), BF16),
        scratch_shapes=[pltpu.VMEM((seq // tk, hd, tk), BF16)],
        compiler_params=pltpu.CompilerParams(
            dimension_semantics=("parallel", "parallel"),
            vmem_limit_bytes=_vmem_limit(
                pipelined=4 * _nbytes((seq, hd), BF16),
                resident=_nbytes((seq, hd), BF16),
                live=(4 * _nbytes((tk, 2 * tq), F32) + _nbytes((tk, 2 * tq), BF16)
                      + 2 * _nbytes((hd, 2 * tq), F32)))),
        name="diff_attn",
    )(proj, proj, proj, lq1, lk1, lq2, lk2, subw)


def _mix_kernel(x_ref, ys_ref, ya_ref, npre_ref, npost_ref, wg_ref, wso_ref, wao_ref, wmix_ref, o_ref):
    x = x_ref[...]
    d = x.shape[1]
    h = _rms(x, npre_ref[...]).astype(BF16)
    gates = jax.nn.sigmoid(_dot_nt(h, wg_ref[...]))
    y_ssm = _dot(ys_ref[...], wso_ref[...])
    y_att = _dot(ya_ref[...], wao_ref[...])
    blend = gates[:, 0:d] * y_ssm + gates[:, d:2 * d] * y_att
    mixed = _dot(blend.astype(BF16), wmix_ref[...])
    o_ref[...] = x + _rms(mixed, npost_ref[...])


def _mix(x2d, y_ssm, y_att, n_pre, n_post, w_gate, w_so, w_ao, w_mix, *, tm=512):
    t, d = x2d.shape
    rows = lambda w: pl.BlockSpec((tm, w), lambda i: (i, 0))
    return pl.pallas_call(
        _mix_kernel,
        grid=(t // tm,),
        in_specs=[rows(d), rows(y_ssm.shape[1]), rows(y_att.shape[1]),
                  _const_spec(n_pre.shape), _const_spec(n_post.shape),
                  _const_spec(w_gate.shape), _const_spec(w_so.shape),
                  _const_spec(w_ao.shape), _const_spec(w_mix.shape)],
        out_specs=rows(d),
        out_shape=jax.ShapeDtypeStruct((t, d), F32),
        compiler_params=pltpu.CompilerParams(
            dimension_semantics=("parallel",),
            vmem_limit_bytes=_vmem_limit(
                pipelined=(2 * _nbytes((tm, d), F32) + _nbytes((tm, y_ssm.shape[1]), BF16)
                           + _nbytes((tm, y_att.shape[1]), BF16)),
                resident=sum(_nbytes(w.shape, BF16) for w in (w_gate, w_so, w_ao, w_mix)),
                live=_nbytes((tm, 2 * d), F32) + 4 * _nbytes((tm, d), F32))),
        name="mix",
    )(x2d, y_ssm, y_att, n_pre, n_post, w_gate, w_so, w_ao, w_mix)


def _ffn_kernel(x_ref, npre_ref, npost_ref, wg_ref, wu_ref, wd_ref, o_ref):
    x = x_ref[...]
    h = _rms(x, npre_ref[...]).astype(BF16)
    act = (_silu(_dot(h, wg_ref[...])) * _dot(h, wu_ref[...])).astype(BF16)
    f = _dot(act, wd_ref[...])
    o_ref[...] = x + _rms(f, npost_ref[...])


def _ffn(x2d, n_pre, n_post, w_g, w_u, w_d, *, tm=512):
    t, d = x2d.shape
    rows = pl.BlockSpec((tm, d), lambda i: (i, 0))
    return pl.pallas_call(
        _ffn_kernel,
        grid=(t // tm,),
        in_specs=[rows, _const_spec(n_pre.shape), _const_spec(n_post.shape),
                  _const_spec(w_g.shape), _const_spec(w_u.shape), _const_spec(w_d.shape)],
        out_specs=rows,
        out_shape=jax.ShapeDtypeStruct((t, d), F32),
        compiler_params=pltpu.CompilerParams(
            dimension_semantics=("parallel",),
            vmem_limit_bytes=_vmem_limit(
                pipelined=2 * _nbytes((tm, d), F32),
                resident=sum(_nbytes(w.shape, BF16) for w in (w_g, w_u, w_d)),
                live=2 * _nbytes((tm, w_g.shape[1]), F32) + _nbytes((tm, w_g.shape[1]), BF16))),
        name="ffn",
    )(x2d, n_pre, n_post, w_g, w_u, w_d)


def kernel(x, positions, w_in, conv_w, conv_b, dt_bias, a_log, d_skip, ssm_norm_w, w_ssm_out, lam_q1, lam_k1, lam_q2, lam_k2, attn_subln_w, w_attn_out, w_mix_out, norm_pre_mix, norm_post_mix, norm_pre_ffn, norm_post_ffn, w_ffn_gate, w_ffn_up, w_ffn_down):
    batch, seq, d_model = x.shape
    depth = w_in.shape[0]
    d_inner = w_ssm_out.shape[1]
    n_ssm_heads = dt_bias.shape[1]
    bc_w = 2 * SSM_N_GROUPS * SSM_D_STATE
    att_w = w_attn_out.shape[1]
    n_att_heads = att_w // ATT_V_DIM
    t = batch * seq

    z_end = d_inner
    xbc_end = z_end + d_inner + bc_w
    dt_end = xbc_end + n_ssm_heads
    q_end = dt_end + att_w
    k_end = q_end + att_w
    v_end = k_end + att_w

    col_z, col_xs = 0, d_inner
    col_bc = 2 * d_inner
    col_q = col_bc + bc_w
    col_k = col_q + att_w
    col_v = col_k + att_w

    inv_freq = ROPE_THETA ** (-jnp.arange(0, ROPE_DIM, 2, dtype=F32) / ROPE_DIM)
    head_of_col = np.arange(d_inner) // SSM_HEAD_DIM
    assert n_ssm_heads <= LANES
    e_mat = jnp.asarray(np.arange(LANES)[:, None] == head_of_col[None, :], BF16)
    tri = np.arange(SSM_CHUNK)[None, :] <= np.arange(SSM_CHUNK)[:, None]
    tri = jnp.asarray(np.concatenate([tri, tri, tri], axis=1), BF16)
    L = SSM_CHUNK
    t_out = np.arange(L)[:, None, None]
    tap = np.arange(SSM_CONV)[None, :, None]
    src = t_out - (SSM_CONV - 1) + tap
    src_col = np.where(src >= 0, tap * L + src, SSM_CONV * L + tap * CONV_CARRY + CONV_CARRY + src)
    shift = jnp.asarray((np.arange((SSM_CONV + 1) * L)[None, None, :] == src_col).any(axis=1), BF16)

    def pad_heads(v):
        return jnp.pad(v.astype(F32), (0, LANES - n_ssm_heads))[None, :]

    x2d = x.reshape(t, d_model)
    for l in range(depth):
        lam_init = 0.8 - 0.6 * math.exp(-0.3 * l)
        wl = w_in[l]
        w_t = wl.T.astype(BF16)
        later = [w_ssm_out[l], w_attn_out[l], w_mix_out[l], w_ffn_gate[l], w_ffn_up[l], w_ffn_down[l]]
        proj, dt, w_gate_t, w_so, w_ao, w_mix, w_fg, w_fu, w_fd = _in_proj(
            x2d, positions, norm_pre_mix[l][None, :], w_t, pad_heads(dt_bias[l]), inv_freq, later,
            n_out=col_v + att_w, row_dt=xbc_end, n_dt=n_ssm_heads, row_q=dt_end, row_tail=v_end,
            col_q=col_q, col_k=col_k, col_v=col_v)

        cw = jnp.repeat(conv_w[l].astype(BF16), CONV_CARRY, axis=0)
        y_ssm = _ssd(proj, dt,
                     cw[:, :d_inner], conv_b[l][None, :d_inner],
                     cw[:, d_inner:], conv_b[l][None, d_inner:],
                     pad_heads(a_log[l]), jnp.repeat(d_skip[l], SSM_HEAD_DIM)[None, :],
                     ssm_norm_w[l][None, :], e_mat, tri, shift,
                     batch=batch, seq=seq, d_inner=d_inner, col_xs=col_xs, col_z=col_z, col_bc=col_bc)

        y_att = _attention(proj, lam_q1[l][None, :], lam_k1[l][None, :], lam_q2[l][None, :],
                           lam_k2[l][None, :], attn_subln_w[l][:, None],
                           batch=batch, seq=seq, n_heads=n_att_heads,
                           col_q=col_q, col_k=col_k, col_v=col_v, lam_init=lam_init)

        x2d = _mix(x2d, y_ssm, y_att, norm_pre_mix[l][None, :], norm_post_mix[l][None, :],
                   w_gate_t, w_so, w_ao, w_mix)
        x2d = _ffn(x2d, norm_pre_ffn[l][None, :], norm_post_ffn[l][None, :], w_fg, w_fu, w_fd)
    return x2d.reshape(batch, seq, d_model)
```

```python
import functools
import math

import jax
import jax.numpy as jnp
import numpy as np
from jax import lax
from jax.experimental import pallas as pl
from jax.experimental.pallas import tpu as pltpu

F32 = jnp.float32
BF16 = jnp.bfloat16

LANES = 128

SSM_HEAD_DIM = 64
SSM_N_GROUPS = 4
SSM_D_STATE = 128
SSM_CONV = 4
SSM_CHUNK = 128
CONV_CARRY = 16
ATT_HEAD_DIM = 64
ATT_V_DIM = 2 * ATT_HEAD_DIM
ROPE_THETA = 500000.0
ROPE_DIM = ATT_HEAD_DIM // 4
RMS_EPS = 1e-6
Q_SCALE = ATT_HEAD_DIM ** -0.5 * math.log2(math.e)

MIB = 1024 * 1024
V7X_VMEM_BYTES = 64 * MIB
VMEM_RESERVE = 8 * MIB


def _nbytes(shape, dtype):
    return math.prod(shape) * jnp.dtype(dtype).itemsize


def _vmem_limit(pipelined, resident, live):
    budget = V7X_VMEM_BYTES - VMEM_RESERVE
    need = 2 * pipelined + resident + live
    assert need <= budget, need
    return budget


def _rms(x, w):
    return x * lax.rsqrt(jnp.mean(x * x, axis=-1, keepdims=True) + RMS_EPS) * w


def _silu(x):
    hx = 0.5 * x
    return hx + hx * jnp.tanh(hx)


def _dot(a, b):
    return jnp.dot(a, b, preferred_element_type=F32)


def _dot_nt(a, b):
    return lax.dot_general(a, b, (((1,), (1,)), ((), ())), preferred_element_type=F32)


def _split3(x):
    hi = x.astype(BF16)
    r = x - hi.astype(F32)
    mid = r.astype(BF16)
    lo = (r - mid.astype(F32)).astype(BF16)
    return hi, mid, lo


def _const_spec(shape):
    nd = len(shape)
    return pl.BlockSpec(shape, lambda *_: (0,) * nd, pipeline_mode=pl.Buffered(1))


def _inproj_kernel(x_ref, pos_ref, nw_ref, wt_ref, dtb_ref, invf_ref, rope_e_ref, rope_c_ref,
                   *refs, n_cast, row_dt, n_dt, row_q, row_tail, col_q, col_k, col_v, sub):
    cast_in = refs[:n_cast]
    out_ref, dt_ref, tail_ref = refs[n_cast:n_cast + 3]
    cast_out = refs[n_cast + 3:2 * n_cast + 3]
    h_scr, cos_scr, s1_scr, s2_scr = refs[2 * n_cast + 3:]
    for src, dst in zip(cast_in, cast_out):
        dst[...] = src[...].astype(dst.dtype)
    tail_rows = tail_ref.shape[0]
    tail_start = pl.multiple_of(row_tail + pl.program_id(0) * tail_rows, tail_rows)
    tail_ref[...] = wt_ref[pl.ds(tail_start, tail_rows), :]

    hb = _rms(x_ref[...], nw_ref[...]).astype(BF16)
    h_scr[...] = hb
    dt_lane = lax.broadcasted_iota(jnp.int32, dt_ref.shape, 1)
    dt = jax.nn.softplus(_dot_nt(hb, wt_ref[row_dt:row_dt + LANES, :]) + dtb_ref[...])
    dt_ref[...] = jnp.where(dt_lane < n_dt, dt, 0.0)
    half = ROPE_DIM // 2
    ang = invf_ref[...] * pos_ref[...].astype(F32)
    terms = [t.astype(F32) for t in _split3(jnp.cos(ang)) + _split3(jnp.sin(ang))]
    fill = jnp.zeros((LANES - len(terms) * half, ang.shape[1]), F32)
    grid_t = jnp.concatenate(terms + [fill], axis=0).T.astype(BF16)
    tables = _dot(grid_t, rope_e_ref[...])
    cos_scr[...] = tables[:, 0:LANES] + rope_c_ref[...]
    s1_scr[...] = tables[:, LANES:2 * LANES]
    s2_scr[...] = tables[:, 2 * LANES:3 * LANES]

    for c0 in range(0, out_ref.shape[1], sub):
        r0 = c0 if c0 < col_q else c0 - col_q + row_q
        acc = _dot_nt(h_scr[...], wt_ref[r0:r0 + sub, :])
        if col_q <= c0 < col_v:
            scale = Q_SCALE if c0 < col_k else 1.0
            for c in range(c0, c0 + sub, LANES):
                a = acc[:, c - c0:c - c0 + LANES]
                r = (a * cos_scr[...] + pltpu.roll(a, half, 1) * s1_scr[...]
                     + pltpu.roll(a, LANES - half, 1) * s2_scr[...])
                out_ref[:, c:c + LANES] = (r * scale).astype(out_ref.dtype)
        else:
            out_ref[:, c0:c0 + sub] = acc.astype(out_ref.dtype)


def _in_proj(x2d, positions, norm_w, w_t, dt_bias, inv_freq, casts, *, n_out, row_dt, n_dt, row_q,
             row_tail, col_q, col_k, col_v, tm=512, sub=512):
    t, d = x2d.shape
    steps = t // tm
    assert all(c % sub == 0 for c in (col_q, col_k, col_v, n_out))
    tail_rows = (w_t.shape[0] - row_tail) // steps
    assert tail_rows * steps == w_t.shape[0] - row_tail and tail_rows % 16 == 0 and row_tail % 16 == 0

    def cast_spec(w):
        hold = next(h for h in (1, 2, 4, 8) if (w.shape[0] * h) % (16 * steps) == 0)
        return pl.BlockSpec((w.shape[0] * hold // steps, w.shape[1]), lambda i: (i // hold, 0))

    cast_specs = [cast_spec(w) for w in casts]
    half = ROPE_DIM // 2
    lane = np.arange(LANES) % ATT_HEAD_DIM
    freq_row = np.arange(LANES) % half
    is_cos = np.arange(LANES) < 3 * half
    is_sin = (np.arange(LANES) >= 3 * half) & (np.arange(LANES) < 6 * half)
    same_f = freq_row[:, None] == (lane % half)[None, :]
    rope_e = np.concatenate([
        (is_cos[:, None] & same_f & (lane < ROPE_DIM)[None, :]) * 1.0,
        (is_sin[:, None] & same_f & ((lane >= half) & (lane < ROPE_DIM))[None, :]) * 1.0,
        (is_sin[:, None] & same_f & (lane < half)[None, :]) * -1.0], axis=1)
    rope_c = (lane >= ROPE_DIM)[None, :] * 1.0
    kern = functools.partial(_inproj_kernel, n_cast=len(casts), row_dt=row_dt, n_dt=n_dt, row_q=row_q,
                             row_tail=row_tail, col_q=col_q, col_k=col_k, col_v=col_v, sub=sub)
    return pl.pallas_call(
        kern,
        grid=(t // tm,),
        in_specs=[
            pl.BlockSpec((tm, d), lambda i: (i, 0)),
            pl.BlockSpec((None, 1, tm), lambda i: (i, 0, 0)),
            _const_spec((1, d)),
            _const_spec(w_t.shape),
            _const_spec((1, LANES)),
            _const_spec((half, 1)),
            _const_spec(rope_e.shape),
            _const_spec(rope_c.shape),
        ] + cast_specs,
        out_specs=[
            pl.BlockSpec((tm, n_out), lambda i: (i, 0)),
            pl.BlockSpec((tm, LANES), lambda i: (i, 0)),
            pl.BlockSpec((tail_rows, d), lambda i: (i, 0)),
        ] + cast_specs,
        out_shape=[
            jax.ShapeDtypeStruct((t, n_out), BF16),
            jax.ShapeDtypeStruct((t, LANES), F32),
            jax.ShapeDtypeStruct((w_t.shape[0] - row_tail, d), BF16),
        ] + [jax.ShapeDtypeStruct(w.shape, BF16) for w in casts],
        scratch_shapes=[
            pltpu.VMEM((tm, d), BF16),
            pltpu.VMEM((tm, LANES), F32),
            pltpu.VMEM((tm, LANES), F32),
            pltpu.VMEM((tm, LANES), F32),
        ],
        compiler_params=pltpu.CompilerParams(
            dimension_semantics=("arbitrary",),
            vmem_limit_bytes=_vmem_limit(
                pipelined=(_nbytes((tm, d), F32) + _nbytes((tm, n_out), BF16) + _nbytes((tm, LANES), F32)
                           + _nbytes((tail_rows, d), BF16)
                           + sum(_nbytes(s.block_shape, F32) + _nbytes(s.block_shape, BF16) for s in cast_specs)),
                resident=_nbytes(w_t.shape, BF16) + _nbytes((tm, d), BF16) + 3 * _nbytes((tm, LANES), F32),
                live=3 * _nbytes((tm, sub), F32))),
        name="in_proj",
    )(x2d, positions.reshape(t // tm, 1, tm), norm_w, w_t, dt_bias, inv_freq[:, None],
      jnp.asarray(rope_e, BF16), jnp.asarray(rope_c, F32), *casts)


def _ssd_kernel(xs_ref, z_ref, bc_ref, dt_ref, cwx_ref, cbx_ref, cwbc_ref, cbbc_ref,
                alog_ref, dskip_ref, nw_ref, e_ref, tri_ref, shift_ref,
                y_ref, xprev, bcprev, hstate, *, n_groups, d_state, head_dim, chunks):
    L = SSM_CHUNK
    d_inner = xs_ref.shape[1]
    gw = d_inner // n_groups
    heads_per_group = gw // head_dim
    pairs_per_group = heads_per_group // 2

    @pl.when(pl.program_id(1) == 0)
    def _():
        xprev[...] = jnp.zeros(xprev.shape, BF16)
        bcprev[...] = jnp.zeros(bcprev.shape, BF16)
        hstate[...] = jnp.zeros(hstate.shape, F32)

    a_neg = -jnp.exp(alog_ref[...]) * math.log2(math.e)
    row = lax.broadcasted_iota(jnp.int32, (L, L), 0)
    col = lax.broadcasted_iota(jnp.int32, (L, L), 1)
    causal = col <= row
    lane = lax.broadcasted_iota(jnp.int32, (L, LANES), 1)

    def conv_silu(cur_ref, prev, w_ref, b_ref, r0):
        cur = cur_ref[r0:r0 + L, :]
        old = prev[...] if r0 == 0 else cur_ref[r0 - CONV_CARRY:r0, :]
        wk = [w_ref[k * CONV_CARRY:(k + 1) * CONV_CARRY, :] for k in range(SSM_CONV)]
        taps = [cur * jnp.concatenate([wk[k]] * (L // CONV_CARRY), axis=0) for k in range(SSM_CONV)]
        carry = [old * wk[k] for k in range(SSM_CONV - 1)]
        pad = jnp.zeros((L - len(carry) * CONV_CARRY, cur.shape[1]), BF16)
        acc = b_ref[...] + _dot(shift_ref[...], jnp.concatenate(taps + carry + [pad], axis=0))
        return _silu(acc)

    h = [hstate[g] for g in range(n_groups)]
    for ci in range(chunks):
        r0 = ci * L
        xs = conv_silu(xs_ref, xprev, cwx_ref, cbx_ref, r0)
        bc = conv_silu(bc_ref, bcprev, cwbc_ref, cbbc_ref, r0).astype(BF16)

        dt = dt_ref[r0:r0 + L, :]
        a_cs = _dot(tri_ref[...], jnp.concatenate(_split3(dt * a_neg), axis=0))
        a_cs_t = a_cs.T
        ea = jnp.exp2(a_cs)
        dte = jnp.exp2(a_cs[L - 1:L, :] - a_cs)
        stacked = jnp.concatenate([dt, ea, dte], axis=0).astype(BF16)
        expanded = _dot(stacked, e_ref[...])
        dt_e = expanded[0:L]
        ea_e = expanded[L:2 * L]
        dte_e = expanded[2 * L:3 * L]

        xdt = xs * dt_e
        xdt_b = xdt.astype(BF16)
        xw_b = (xdt * dte_e).astype(BF16)
        gate = _silu(z_ref[r0:r0 + L, :].astype(F32))

        for g in range(n_groups):
            b_g = bc[:, g * d_state:(g + 1) * d_state]
            c_g = bc[:, (n_groups + g) * d_state:(n_groups + g + 1) * d_state]
            cb = _dot_nt(c_g, b_g)
            y_diag = []
            for p2 in range(pairs_per_group):
                pair = g * pairs_per_group + p2
                ms = []
                for hh in (2 * pair, 2 * pair + 1):
                    seg = a_cs[:, hh:hh + 1] - a_cs_t[hh:hh + 1, :]
                    dec = jnp.exp2(jnp.where(causal, seg, -jnp.inf))
                    ms.append((cb * dec).astype(BF16))
                lhs = jnp.concatenate(ms, axis=1)
                xp = xdt_b[:, pair * LANES:(pair + 1) * LANES]
                zero = jnp.zeros_like(xp)
                rhs = jnp.concatenate([jnp.where(lane < head_dim, xp, zero),
                                       jnp.where(lane >= head_dim, xp, zero)], axis=0)
                y_diag.append(_dot(lhs, rhs))
            sl = slice(g * gw, (g + 1) * gw)
            ea_g = ea_e[:, sl]
            y_off = _dot(c_g, h[g].astype(BF16)) * ea_g
            h[g] = h[g] * ea_g[L - 1:L, :] + _dot(b_g.T, xw_b[:, sl])
            yg = (jnp.concatenate(y_diag, axis=1) + y_off + xs[:, sl] * dskip_ref[:, sl]) * gate[:, sl]
            y_ref[r0:r0 + L, sl] = _rms(yg, nw_ref[:, sl]).astype(y_ref.dtype)

    for g in range(n_groups):
        hstate[g] = h[g]
    xprev[...] = xs_ref[chunks * L - CONV_CARRY:chunks * L, :]
    bcprev[...] = bc_ref[chunks * L - CONV_CARRY:chunks * L, :]


def _ssd(proj, dt, cw_x, cb_x, cw_bc, cb_bc, a_log, d_skip_e, norm_w, e_mat, tri, shift,
         *, batch, seq, d_inner, col_xs, col_z, col_bc, chunks=2):
    L = SSM_CHUNK
    rows = chunks * L
    nc = seq // rows
    t = batch * seq
    bc_w = 2 * SSM_N_GROUPS * SSM_D_STATE
    row = lambda b, c: b * nc + c
    kern = functools.partial(_ssd_kernel, n_groups=SSM_N_GROUPS, d_state=SSM_D_STATE,
                             head_dim=SSM_HEAD_DIM, chunks=chunks)
    return pl.pallas_call(
        kern,
        grid=(batch, nc),
        in_specs=[
            pl.BlockSpec((rows, d_inner), lambda b, c: (row(b, c), col_xs // d_inner)),
            pl.BlockSpec((rows, d_inner), lambda b, c: (row(b, c), col_z // d_inner)),
            pl.BlockSpec((rows, bc_w), lambda b, c: (row(b, c), col_bc // bc_w)),
            pl.BlockSpec((rows, LANES), lambda b, c: (row(b, c), 0)),
            pl.BlockSpec((SSM_CONV * CONV_CARRY, d_inner), lambda b, c: (0, 0)),
            pl.BlockSpec((1, d_inner), lambda b, c: (0, 0)),
            pl.BlockSpec((SSM_CONV * CONV_CARRY, bc_w), lambda b, c: (0, 0)),
            pl.BlockSpec((1, bc_w), lambda b, c: (0, 0)),
            pl.BlockSpec((1, LANES), lambda b, c: (0, 0)),
            pl.BlockSpec((1, d_inner), lambda b, c: (0, 0)),
            pl.BlockSpec((1, d_inner), lambda b, c: (0, 0)),
            pl.BlockSpec((LANES, d_inner), lambda b, c: (0, 0)),
            pl.BlockSpec((L, 3 * L), lambda b, c: (0, 0)),
            pl.BlockSpec(shift.shape, lambda b, c: (0, 0)),
        ],
        out_specs=pl.BlockSpec((rows, d_inner), lambda b, c: (row(b, c), 0)),
        out_shape=jax.ShapeDtypeStruct((t, d_inner), BF16),
        scratch_shapes=[
            pltpu.VMEM((CONV_CARRY, d_inner), BF16),
            pltpu.VMEM((CONV_CARRY, bc_w), BF16),
            pltpu.VMEM((SSM_N_GROUPS, SSM_D_STATE, d_inner // SSM_N_GROUPS), F32),
        ],
        compiler_params=pltpu.CompilerParams(
            dimension_semantics=("parallel", "arbitrary"),
            vmem_limit_bytes=_vmem_limit(
                pipelined=(3 * _nbytes((rows, d_inner), BF16) + _nbytes((rows, bc_w), BF16)
                           + _nbytes((rows, LANES), F32) + _nbytes(cw_x.shape, BF16) + _nbytes(cw_bc.shape, BF16)
                           + _nbytes(e_mat.shape, BF16) + _nbytes(tri.shape, BF16) + _nbytes(shift.shape, BF16)),
                resident=_nbytes((SSM_D_STATE, d_inner), F32),
                live=chunks * (_nbytes((3 * L, d_inner), F32) + _nbytes((L, d_inner + bc_w), F32)
                               + 2 * _nbytes((L, d_inner), F32)))),
        name="ssd",
    )(proj, proj, proj, dt, cw_x, cb_x, cw_bc, cb_bc, a_log, d_skip_e, norm_w, e_mat, tri, shift)


def _attn_kernel(q_ref, k_ref, v_ref, lq1_ref, lk1_ref, lq2_ref, lk2_ref, subw_ref,
                 o_ref, vt_scr, *, tq, tk, lam_init):
    seq = q_ref.shape[0]
    for jb in range(seq // tk):
        vt_scr[jb] = v_ref[jb * tk:(jb + 1) * tk, :].T

    lam = (jnp.exp(jnp.sum(lq1_ref[...] * lk1_ref[...]))
           - jnp.exp(jnp.sum(lq2_ref[...] * lk2_ref[...])) + lam_init)
    subw = subw_ref[...] * (1.0 - lam_init)

    def scores(qi, j):
        q = q_ref[qi * tq:(qi + 1) * tq, :]
        lane = lax.broadcasted_iota(jnp.int32, q.shape, 1)
        zero = jnp.zeros_like(q)
        q2 = jnp.concatenate([jnp.where(lane < ATT_HEAD_DIM, q, zero),
                              jnp.where(lane >= ATT_HEAD_DIM, q, zero)], axis=0)
        return _dot_nt(k_ref[j * tk:(j + 1) * tk, :], q2)

    steps = [(qi, j) for qi in range(seq // tq) for j in range((qi + 1) * tq // tk)]
    s_next = scores(*steps[0])
    m = l = acc = None
    for t, (qi, j) in enumerate(steps):
        s = s_next
        if t + 1 < len(steps):
            s_next = scores(*steps[t + 1])
        if (j + 1) * tk > qi * tq:
            kv = lax.broadcasted_iota(jnp.int32, s.shape, 0) + (j * tk - qi * tq)
            r = lax.broadcasted_iota(jnp.int32, s.shape, 1)
            r = jnp.where(r >= tq, r - tq, r)
            s = jnp.where(kv <= r, s, -jnp.inf)
        s_max = jnp.max(s, axis=0, keepdims=True)
        if j == 0:
            m = s_max
            p = jnp.exp2(s - m)
            l = jnp.sum(p, axis=0, keepdims=True)
            acc = _dot(vt_scr[j], p.astype(BF16))
        else:
            m_new = jnp.maximum(m, s_max)
            alpha = jnp.exp2(m - m_new)
            p = jnp.exp2(s - m_new)
            l = alpha * l + jnp.sum(p, axis=0, keepdims=True)
            acc = alpha * acc + _dot(vt_scr[j], p.astype(BF16))
            m = m_new
        if (j + 1) * tk == (qi + 1) * tq:
            o = acc / l
            od = o[:, 0:tq] - lam * o[:, tq:2 * tq]
            od = od * lax.rsqrt(jnp.mean(od * od, axis=0, keepdims=True) + RMS_EPS)
            o_ref[qi * tq:(qi + 1) * tq, :] = (od * subw).T.astype(o_ref.dtype)


def _attention(proj, lq1, lk1, lq2, lk2, subw, *, batch, seq, n_heads, col_q, col_k, col_v,
               lam_init, tq=512, tk=512):
    t = batch * seq
    hd = ATT_V_DIM
    kern = functools.partial(_attn_kernel, tq=tq, tk=tk, lam_init=lam_init)
    small = pl.BlockSpec((1, ATT_HEAD_DIM), lambda b, h: (0, 0))
    return pl.pallas_call(
        kern,
        grid=(batch, n_heads),
        in_specs=[
            pl.BlockSpec((seq, hd), lambda b, h: (b, col_q // hd + h)),
            pl.BlockSpec((seq, hd), lambda b, h: (b, col_k // hd + h)),
            pl.BlockSpec((seq, hd), lambda b, h: (b, col_v // hd + h)),
            small, small, small, small,
            pl.BlockSpec((hd, 1), lambda b, h: (0, 0)),
        ],
        out_specs=pl.BlockSpec((seq, hd), lambda b, h: (b, h)),
        out_shape=jax.ShapeDtypeStruct((t, n_heads * hd), BF16),
        scratch_shapes=[pltpu.VMEM((seq // tk, hd, tk), BF16)],
        compiler_params=pltpu.CompilerParams(
            dimension_semantics=("parallel", "parallel"),
            vmem_limit_bytes=_vmem_limit(
                pipelined=4 * _nbytes((seq, hd), BF16),
                resident=_nbytes((seq, hd), BF16),
                live=(4 * _nbytes((tk, 2 * tq), F32) + _nbytes((tk, 2 * tq), BF16)
                      + 2 * _nbytes((hd, 2 * tq), F32)))),
        name="diff_attn",
    )(proj, proj, proj, lq1, lk1, lq2, lk2, subw)


def _mix_kernel(x_ref, ys_ref, ya_ref, npre_ref, npost_ref, wg_ref, wso_ref, wao_ref, wmix_ref, o_ref):
    x = x_ref[...]
    d = x.shape[1]
    h = _rms(x, npre_ref[...]).astype(BF16)
    gates = jax.nn.sigmoid(_dot_nt(h, wg_ref[...]))
    y_ssm = _dot(ys_ref[...], wso_ref[...])
    y_att = _dot(ya_ref[...], wao_ref[...])
    blend = gates[:, 0:d] * y_ssm + gates[:, d:2 * d] * y_att
    mixed = _dot(blend.astype(BF16), wmix_ref[...])
    o_ref[...] = x + _rms(mixed, npost_ref[...])


def _mix(x2d, y_ssm, y_att, n_pre, n_post, w_gate, w_so, w_ao, w_mix, *, tm=512):
    t, d = x2d.shape
    rows = lambda w: pl.BlockSpec((tm, w), lambda i: (i, 0))
    return pl.pallas_call(
        _mix_kernel,
        grid=(t // tm,),
        in_specs=[rows(d), rows(y_ssm.shape[1]), rows(y_att.shape[1]),
                  _const_spec(n_pre.shape), _const_spec(n_post.shape),
                  _const_spec(w_gate.shape), _const_spec(w_so.shape),
                  _const_spec(w_ao.shape), _const_spec(w_mix.shape)],
        out_specs=rows(d),
        out_shape=jax.ShapeDtypeStruct((t, d), F32),
        compiler_params=pltpu.CompilerParams(
            dimension_semantics=("parallel",),
            vmem_limit_bytes=_vmem_limit(
                pipelined=(2 * _nbytes((tm, d), F32) + _nbytes((tm, y_ssm.shape[1]), BF16)
                           + _nbytes((tm, y_att.shape[1]), BF16)),
                resident=sum(_nbytes(w.shape, BF16) for w in (w_gate, w_so, w_ao, w_mix)),
                live=_nbytes((tm, 2 * d), F32) + 4 * _nbytes((tm, d), F32))),
        name="mix",
    )(x2d, y_ssm, y_att, n_pre, n_post, w_gate, w_so, w_ao, w_mix)


def _ffn_kernel(x_ref, npre_ref, npost_ref, wg_ref, wu_ref, wd_ref, o_ref):
    x = x_ref[...]
    h = _rms(x, npre_ref[...]).astype(BF16)
    act = (_silu(_dot(h, wg_ref[...])) * _dot(h, wu_ref[...])).astype(BF16)
    f = _dot(act, wd_ref[...])
    o_ref[...] = x + _rms(f, npost_ref[...])


def _ffn(x2d, n_pre, n_post, w_g, w_u, w_d, *, tm=512):
    t, d = x2d.shape
    rows = pl.BlockSpec((tm, d), lambda i: (i, 0))
    return pl.pallas_call(
        _ffn_kernel,
        grid=(t // tm,),
        in_specs=[rows, _const_spec(n_pre.shape), _const_spec(n_post.shape),
                  _const_spec(w_g.shape), _const_spec(w_u.shape), _const_spec(w_d.shape)],
        out_specs=rows,
        out_shape=jax.ShapeDtypeStruct((t, d), F32),
        compiler_params=pltpu.CompilerParams(
            dimension_semantics=("parallel",),
            vmem_limit_bytes=_vmem_limit(
                pipelined=2 * _nbytes((tm, d), F32),
                resident=sum(_nbytes(w.shape, BF16) for w in (w_g, w_u, w_d)),
                live=2 * _nbytes((tm, w_g.shape[1]), F32) + _nbytes((tm, w_g.shape[1]), BF16))),
        name="ffn",
    )(x2d, n_pre, n_post, w_g, w_u, w_d)


def kernel(x, positions, w_in, conv_w, conv_b, dt_bias, a_log, d_skip, ssm_norm_w, w_ssm_out, lam_q1, lam_k1, lam_q2, lam_k2, attn_subln_w, w_attn_out, w_mix_out, norm_pre_mix, norm_post_mix, norm_pre_ffn, norm_post_ffn, w_ffn_gate, w_ffn_up, w_ffn_down):
    batch, seq, d_model = x.shape
    depth = w_in.shape[0]
    d_inner = w_ssm_out.shape[1]
    n_ssm_heads = dt_bias.shape[1]
    bc_w = 2 * SSM_N_GROUPS * SSM_D_STATE
    att_w = w_attn_out.shape[1]
    n_att_heads = att_w // ATT_V_DIM
    t = batch * seq

    z_end = d_inner
    xbc_end = z_end + d_inner + bc_w
    dt_end = xbc_end + n_ssm_heads
    q_end = dt_end + att_w
    k_end = q_end + att_w
    v_end = k_end + att_w

    col_z, col_xs = 0, d_inner
    col_bc = 2 * d_inner
    col_q = col_bc + bc_w
    col_k = col_q + att_w
    col_v = col_k + att_w

    inv_freq = ROPE_THETA ** (-jnp.arange(0, ROPE_DIM, 2, dtype=F32) / ROPE_DIM)
    head_of_col = np.arange(d_inner) // SSM_HEAD_DIM
    assert n_ssm_heads <= LANES
    e_mat = jnp.asarray(np.arange(LANES)[:, None] == head_of_col[None, :], BF16)
    tri = np.arange(SSM_CHUNK)[None, :] <= np.arange(SSM_CHUNK)[:, None]
    tri = jnp.asarray(np.concatenate([tri, tri, tri], axis=1), BF16)
    L = SSM_CHUNK
    t_out = np.arange(L)[:, None, None]
    tap = np.arange(SSM_CONV)[None, :, None]
    src = t_out - (SSM_CONV - 1) + tap
    src_col = np.where(src >= 0, tap * L + src, SSM_CONV * L + tap * CONV_CARRY + CONV_CARRY + src)
    shift = jnp.asarray((np.arange((SSM_CONV + 1) * L)[None, None, :] == src_col).any(axis=1), BF16)

    def pad_heads(v):
        return jnp.pad(v.astype(F32), (0, LANES - n_ssm_heads))[None, :]

    x2d = x.reshape(t, d_model)
    for l in range(depth):
        lam_init = 0.8 - 0.6 * math.exp(-0.3 * l)
        wl = w_in[l]
        w_t = wl.T.astype(BF16)
        later = [w_ssm_out[l], w_attn_out[l], w_mix_out[l], w_ffn_gate[l], w_ffn_up[l], w_ffn_down[l]]
        proj, dt, w_gate_t, w_so, w_ao, w_mix, w_fg, w_fu, w_fd = _in_proj(
            x2d, positions, norm_pre_mix[l][None, :], w_t, pad_heads(dt_bias[l]), inv_freq, later,
            n_out=col_v + att_w, row_dt=xbc_end, n_dt=n_ssm_heads, row_q=dt_end, row_tail=v_end,
            col_q=col_q, col_k=col_k, col_v=col_v)

        cw = jnp.repeat(conv_w[l].astype(BF16), CONV_CARRY, axis=0)
        y_ssm = _ssd(proj, dt,
                     cw[:, :d_inner], conv_b[l][None, :d_inner],
                     cw[:, d_inner:], conv_b[l][None, d_inner:],
                     pad_heads(a_log[l]), jnp.repeat(d_skip[l], SSM_HEAD_DIM)[None, :],
                     ssm_norm_w[l][None, :], e_mat, tri, shift,
                     batch=batch, seq=seq, d_inner=d_inner, col_xs=col_xs, col_z=col_z, col_bc=col_bc)

        y_att = _attention(proj, lam_q1[l][None, :], lam_k1[l][None, :], lam_q2[l][None, :],
                           lam_k2[l][None, :], attn_subln_w[l][:, None],
                           batch=batch, seq=seq, n_heads=n_att_heads,
                           col_q=col_q, col_k=col_k, col_v=col_v, lam_init=lam_init)

        x2d = _mix(x2d, y_ssm, y_att, norm_pre_mix[l][None, :], norm_post_mix[l][None, :],
                   w_gate_t, w_so, w_ao, w_mix)
        x2d = _ffn(x2d, norm_pre_ffn[l][None, :], norm_post_ffn[l][None, :], w_fg, w_fu, w_fd)
    return x2d.reshape(batch, seq, d_model)
```

```python
import functools
import math

import jax
import jax.numpy as jnp
import numpy as np
from jax import lax
from jax.experimental import pallas as pl
from jax.experimental.pallas import tpu as pltpu

F32 = jnp.float32
BF16 = jnp.bfloat16

LANES = 128

SSM_HEAD_DIM = 64
SSM_N_GROUPS = 4
SSM_D_STATE = 128
SSM_CONV = 4
SSM_CHUNK = 128
CONV_CARRY = 16
ATT_HEAD_DIM = 64
ATT_V_DIM = 2 * ATT_HEAD_DIM
ROPE_THETA = 500000.0
ROPE_DIM = ATT_HEAD_DIM // 4
RMS_EPS = 1e-6
Q_SCALE = ATT_HEAD_DIM ** -0.5 * math.log2(math.e)

MIB = 1024 * 1024
V7X_VMEM_BYTES = 64 * MIB
VMEM_RESERVE = 8 * MIB


def _nbytes(shape, dtype):
    return math.prod(shape) * jnp.dtype(dtype).itemsize


def _vmem_limit(pipelined, resident, live):
    budget = V7X_VMEM_BYTES - VMEM_RESERVE
    need = 2 * pipelined + resident + live
    assert need <= budget, need
    return budget


def _rms(x, w):
    return x * lax.rsqrt(jnp.mean(x * x, axis=-1, keepdims=True) + RMS_EPS) * w


def _silu(x):
    hx = 0.5 * x
    return hx + hx * jnp.tanh(hx)


def _dot(a, b):
    return jnp.dot(a, b, preferred_element_type=F32)


def _dot_nt(a, b):
    return lax.dot_general(a, b, (((1,), (1,)), ((), ())), preferred_element_type=F32)


def _split3(x):
    hi = x.astype(BF16)
    r = x - hi.astype(F32)
    mid = r.astype(BF16)
    lo = (r - mid.astype(F32)).astype(BF16)
    return hi, mid, lo


def _const_spec(shape):
    nd = len(shape)
    return pl.BlockSpec(shape, lambda *_: (0,) * nd, pipeline_mode=pl.Buffered(1))


def _inproj_kernel(x_ref, pos_ref, nw_ref, wt_ref, dtb_ref, invf_ref, rope_e_ref, rope_c_ref,
                   *refs, n_cast, row_dt, n_dt, row_q, row_tail, col_q, col_k, col_v, sub):
    cast_in = refs[:n_cast]
    out_ref, dt_ref, tail_ref = refs[n_cast:n_cast + 3]
    cast_out = refs[n_cast + 3:2 * n_cast + 3]
    h_scr, cos_scr, s1_scr, s2_scr = refs[2 * n_cast + 3:]
    for src, dst in zip(cast_in, cast_out):
        dst[...] = src[...].astype(dst.dtype)
    tail_rows = tail_ref.shape[0]
    tail_start = pl.multiple_of(row_tail + pl.program_id(0) * tail_rows, tail_rows)
    tail_ref[...] = wt_ref[pl.ds(tail_start, tail_rows), :]

    hb = _rms(x_ref[...], nw_ref[...]).astype(BF16)
    h_scr[...] = hb
    dt_lane = lax.broadcasted_iota(jnp.int32, dt_ref.shape, 1)
    dt = jax.nn.softplus(_dot_nt(hb, wt_ref[row_dt:row_dt + LANES, :]) + dtb_ref[...])
    dt_ref[...] = jnp.where(dt_lane < n_dt, dt, 0.0)
    half = ROPE_DIM // 2
    ang = invf_ref[...] * pos_ref[...].astype(F32)
    terms = [t.astype(F32) for t in _split3(jnp.cos(ang)) + _split3(jnp.sin(ang))]
    fill = jnp.zeros((LANES - len(terms) * half, ang.shape[1]), F32)
    grid_t = jnp.concatenate(terms + [fill], axis=0).T.astype(BF16)
    tables = _dot(grid_t, rope_e_ref[...])
    cos_scr[...] = tables[:, 0:LANES] + rope_c_ref[...]
    s1_scr[...] = tables[:, LANES:2 * LANES]
    s2_scr[...] = tables[:, 2 * LANES:3 * LANES]

    for c0 in range(0, out_ref.shape[1], sub):
        r0 = c0 if c0 < col_q else c0 - col_q + row_q
        acc = _dot_nt(h_scr[...], wt_ref[r0:r0 + sub, :])
        if col_q <= c0 < col_v:
            scale = Q_SCALE if c0 < col_k else 1.0
            for c in range(c0, c0 + sub, LANES):
                a = acc[:, c - c0:c - c0 + LANES]
                r = (a * cos_scr[...] + pltpu.roll(a, half, 1) * s1_scr[...]
                     + pltpu.roll(a, LANES - half, 1) * s2_scr[...])
                out_ref[:, c:c + LANES] = (r * scale).astype(out_ref.dtype)
        else:
            out_ref[:, c0:c0 + sub] = acc.astype(out_ref.dtype)


def _in_proj(x2d, positions, norm_w, w_t, dt_bias, inv_freq, casts, *, n_out, row_dt, n_dt, row_q,
             row_tail, col_q, col_k, col_v, tm=512, sub=512):
    t, d = x2d.shape
    steps = t // tm
    assert all(c % sub == 0 for c in (col_q, col_k, col_v, n_out))
    tail_rows = (w_t.shape[0] - row_tail) // steps
    assert tail_rows * steps == w_t.shape[0] - row_tail and tail_rows % 16 == 0 and row_tail % 16 == 0

    def cast_spec(w):
        hold = next(h for h in (1, 2, 4, 8) if (w.shape[0] * h) % (16 * steps) == 0)
        return pl.BlockSpec((w.shape[0] * hold // steps, w.shape[1]), lambda i: (i // hold, 0))

    cast_specs = [cast_spec(w) for w in casts]
    half = ROPE_DIM // 2
    lane = np.arange(LANES) % ATT_HEAD_DIM
    freq_row = np.arange(LANES) % half
    is_cos = np.arange(LANES) < 3 * half
    is_sin = (np.arange(LANES) >= 3 * half) & (np.arange(LANES) < 6 * half)
    same_f = freq_row[:, None] == (lane % half)[None, :]
    rope_e = np.concatenate([
        (is_cos[:, None] & same_f & (lane < ROPE_DIM)[None, :]) * 1.0,
        (is_sin[:, None] & same_f & ((lane >= half) & (lane < ROPE_DIM))[None, :]) * 1.0,
        (is_sin[:, None] & same_f & (lane < half)[None, :]) * -1.0], axis=1)
    rope_c = (lane >= ROPE_DIM)[None, :] * 1.0
    kern = functools.partial(_inproj_kernel, n_cast=len(casts), row_dt=row_dt, n_dt=n_dt, row_q=row_q,
                             row_tail=row_tail, col_q=col_q, col_k=col_k, col_v=col_v, sub=sub)
    return pl.pallas_call(
        kern,
        grid=(t // tm,),
        in_specs=[
            pl.BlockSpec((tm, d), lambda i: (i, 0)),
            pl.BlockSpec((None, 1, tm), lambda i: (i, 0, 0)),
            _const_spec((1, d)),
            _const_spec(w_t.shape),
            _const_spec((1, LANES)),
            _const_spec((half, 1)),
            _const_spec(rope_e.shape),
            _const_spec(rope_c.shape),
        ] + cast_specs,
        out_specs=[
            pl.BlockSpec((tm, n_out), lambda i: (i, 0)),
            pl.BlockSpec((tm, LANES), lambda i: (i, 0)),
            pl.BlockSpec((tail_rows, d), lambda i: (i, 0)),
        ] + cast_specs,
        out_shape=[
            jax.ShapeDtypeStruct((t, n_out), BF16),
            jax.ShapeDtypeStruct((t, LANES), F32),
            jax.ShapeDtypeStruct((w_t.shape[0] - row_tail, d), BF16),
        ] + [jax.ShapeDtypeStruct(w.shape, BF16) for w in casts],
        scratch_shapes=[
            pltpu.VMEM((tm, d), BF16),
            pltpu.VMEM((tm, LANES), F32),
            pltpu.VMEM((tm, LANES), F32),
            pltpu.VMEM((tm, LANES), F32),
        ],
        compiler_params=pltpu.CompilerParams(
            dimension_semantics=("arbitrary",),
            vmem_limit_bytes=_vmem_limit(
                pipelined=(_nbytes((tm, d), F32) + _nbytes((tm, n_out), BF16) + _nbytes((tm, LANES), F32)
                           + _nbytes((tail_rows, d), BF16)
                           + sum(_nbytes(s.block_shape, F32) + _nbytes(s.block_shape, BF16) for s in cast_specs)),
                resident=_nbytes(w_t.shape, BF16) + _nbytes((tm, d), BF16) + 3 * _nbytes((tm, LANES), F32),
                live=3 * _nbytes((tm, sub), F32))),
        name="in_proj",
    )(x2d, positions.reshape(t // tm, 1, tm), norm_w, w_t, dt_bias, inv_freq[:, None],
      jnp.asarray(rope_e, BF16), jnp.asarray(rope_c, F32), *casts)


def _ssd_kernel(xs_ref, z_ref, bc_ref, dt_ref, cwx_ref, cbx_ref, cwbc_ref, cbbc_ref,
                alog_ref, dskip_ref, nw_ref, e_ref, tri_ref, shift_ref,
                y_ref, xprev, bcprev, hstate, *, n_groups, d_state, head_dim, chunks):
    L = SSM_CHUNK
    d_inner = xs_ref.shape[1]
    gw = d_inner // n_groups
    heads_per_group = gw // head_dim
    pairs_per_group = heads_per_group // 2

    @pl.when(pl.program_id(1) == 0)
    def _():
        xprev[...] = jnp.zeros(xprev.shape, BF16)
        bcprev[...] = jnp.zeros(bcprev.shape, BF16)
        hstate[...] = jnp.zeros(hstate.shape, F32)

    a_neg = -jnp.exp(alog_ref[...]) * math.log2(math.e)
    row = lax.broadcasted_iota(jnp.int32, (L, L), 0)
    col = lax.broadcasted_iota(jnp.int32, (L, L), 1)
    causal = col <= row
    lane = lax.broadcasted_iota(jnp.int32, (L, LANES), 1)

    def conv_silu(cur_ref, prev, w_ref, b_ref, r0):
        cur = cur_ref[r0:r0 + L, :]
        old = prev[...] if r0 == 0 else cur_ref[r0 - CONV_CARRY:r0, :]
        wk = [w_ref[k * CONV_CARRY:(k + 1) * CONV_CARRY, :] for k in range(SSM_CONV)]
        taps = [cur * jnp.concatenate([wk[k]] * (L // CONV_CARRY), axis=0) for k in range(SSM_CONV)]
        carry = [old * wk[k] for k in range(SSM_CONV - 1)]
        pad = jnp.zeros((L - len(carry) * CONV_CARRY, cur.shape[1]), BF16)
        acc = b_ref[...] + _dot(shift_ref[...], jnp.concatenate(taps + carry + [pad], axis=0))
        return _silu(acc)

    h = [hstate[g] for g in range(n_groups)]
    for ci in range(chunks):
        r0 = ci * L
        xs = conv_silu(xs_ref, xprev, cwx_ref, cbx_ref, r0)
        bc = conv_silu(bc_ref, bcprev, cwbc_ref, cbbc_ref, r0).astype(BF16)

        dt = dt_ref[r0:r0 + L, :]
        a_cs = _dot(tri_ref[...], jnp.concatenate(_split3(dt * a_neg), axis=0))
        a_cs_t = a_cs.T
        ea = jnp.exp2(a_cs)
        dte = jnp.exp2(a_cs[L - 1:L, :] - a_cs)
        stacked = jnp.concatenate([dt, ea, dte], axis=0).astype(BF16)
        expanded = _dot(stacked, e_ref[...])
        dt_e = expanded[0:L]
        ea_e = expanded[L:2 * L]
        dte_e = expanded[2 * L:3 * L]

        xdt = xs * dt_e
        xdt_b = xdt.astype(BF16)
        xw_b = (xdt * dte_e).astype(BF16)
        gate = _silu(z_ref[r0:r0 + L, :].astype(F32))

        for g in range(n_groups):
            b_g = bc[:, g * d_state:(g + 1) * d_state]
            c_g = bc[:, (n_groups + g) * d_state:(n_groups + g + 1) * d_state]
            cb = _dot_nt(c_g, b_g)
            y_diag = []
            for p2 in range(pairs_per_group):
                pair = g * pairs_per_group + p2
                ms = []
                for hh in (2 * pair, 2 * pair + 1):
                    seg = a_cs[:, hh:hh + 1] - a_cs_t[hh:hh + 1, :]
                    dec = jnp.exp2(jnp.where(causal, seg, -jnp.inf))
                    ms.append((cb * dec).astype(BF16))
                lhs = jnp.concatenate(ms, axis=1)
                xp = xdt_b[:, pair * LANES:(pair + 1) * LANES]
                zero = jnp.zeros_like(xp)
                rhs = jnp.concatenate([jnp.where(lane < head_dim, xp, zero),
                                       jnp.where(lane >= head_dim, xp, zero)], axis=0)
                y_diag.append(_dot(lhs, rhs))
            sl = slice(g * gw, (g + 1) * gw)
            ea_g = ea_e[:, sl]
            y_off = _dot(c_g, h[g].astype(BF16)) * ea_g
            h[g] = h[g] * ea_g[L - 1:L, :] + _dot(b_g.T, xw_b[:, sl])
            yg = (jnp.concatenate(y_diag, axis=1) + y_off + xs[:, sl] * dskip_ref[:, sl]) * gate[:, sl]
            y_ref[r0:r0 + L, sl] = _rms(yg, nw_ref[:, sl]).astype(y_ref.dtype)

    for g in range(n_groups):
        hstate[g] = h[g]
    xprev[...] = xs_ref[chunks * L - CONV_CARRY:chunks * L, :]
    bcprev[...] = bc_ref[chunks * L - CONV_CARRY:chunks * L, :]


def _ssd(proj, dt, cw_x, cb_x, cw_bc, cb_bc, a_log, d_skip_e, norm_w, e_mat, tri, shift,
         *, batch, seq, d_inner, col_xs, col_z, col_bc, chunks=2):
    L = SSM_CHUNK
    rows = chunks * L
    nc = seq // rows
    t = batch * seq
    bc_w = 2 * SSM_N_GROUPS * SSM_D_STATE
    row = lambda b, c: b * nc + c
    kern = functools.partial(_ssd_kernel, n_groups=SSM_N_GROUPS, d_state=SSM_D_STATE,
                             head_dim=SSM_HEAD_DIM, chunks=chunks)
    return pl.pallas_call(
        kern,
        grid=(batch, nc),
        in_specs=[
            pl.BlockSpec((rows, d_inner), lambda b, c: (row(b, c), col_xs // d_inner)),
            pl.BlockSpec((rows, d_inner), lambda b, c: (row(b, c), col_z // d_inner)),
            pl.BlockSpec((rows, bc_w), lambda b, c: (row(b, c), col_bc // bc_w)),
            pl.BlockSpec((rows, LANES), lambda b, c: (row(b, c), 0)),
            pl.BlockSpec((SSM_CONV * CONV_CARRY, d_inner), lambda b, c: (0, 0)),
            pl.BlockSpec((1, d_inner), lambda b, c: (0, 0)),
            pl.BlockSpec((SSM_CONV * CONV_CARRY, bc_w), lambda b, c: (0, 0)),
            pl.BlockSpec((1, bc_w), lambda b, c: (0, 0)),
            pl.BlockSpec((1, LANES), lambda b, c: (0, 0)),
            pl.BlockSpec((1, d_inner), lambda b, c: (0, 0)),
            pl.BlockSpec((1, d_inner), lambda b, c: (0, 0)),
            pl.BlockSpec((LANES, d_inner), lambda b, c: (0, 0)),
            pl.BlockSpec((L, 3 * L), lambda b, c: (0, 0)),
            pl.BlockSpec(shift.shape, lambda b, c: (0, 0)),
        ],
        out_specs=pl.BlockSpec((rows, d_inner), lambda b, c: (row(b, c), 0)),
        out_shape=jax.ShapeDtypeStruct((t, d_inner), BF16),
        scratch_shapes=[
            pltpu.VMEM((CONV_CARRY, d_inner), BF16),
            pltpu.VMEM((CONV_CARRY, bc_w), BF16),
            pltpu.VMEM((SSM_N_GROUPS, SSM_D_STATE, d_inner // SSM_N_GROUPS), F32),
        ],
        compiler_params=pltpu.CompilerParams(
            dimension_semantics=("parallel", "arbitrary"),
            vmem_limit_bytes=_vmem_limit(
                pipelined=(3 * _nbytes((rows, d_inner), BF16) + _nbytes((rows, bc_w), BF16)
                           + _nbytes((rows, LANES), F32) + _nbytes(cw_x.shape, BF16) + _nbytes(cw_bc.shape, BF16)
                           + _nbytes(e_mat.shape, BF16) + _nbytes(tri.shape, BF16) + _nbytes(shift.shape, BF16)),
                resident=_nbytes((SSM_D_STATE, d_inner), F32),
                live=chunks * (_nbytes((3 * L, d_inner), F32) + _nbytes((L, d_inner + bc_w), F32)
                               + 2 * _nbytes((L, d_inner), F32)))),
        name="ssd",
    )(proj, proj, proj, dt, cw_x, cb_x, cw_bc, cb_bc, a_log, d_skip_e, norm_w, e_mat, tri, shift)


def _attn_kernel(q_ref, k_ref, v_ref, lq1_ref, lk1_ref, lq2_ref, lk2_ref, subw_ref,
                 o_ref, vt_scr, *, tq, tk, lam_init):
    seq = q_ref.shape[0]
    for jb in range(seq // tk):
        vt_scr[jb] = v_ref[jb * tk:(jb + 1) * tk, :].T

    lam = (jnp.exp(jnp.sum(lq1_ref[...] * lk1_ref[...]))
           - jnp.exp(jnp.sum(lq2_ref[...] * lk2_ref[...])) + lam_init)
    subw = subw_ref[...] * (1.0 - lam_init)

    stacked_q = {}

    def scores(qi, j):
        if qi not in stacked_q:
            q = q_ref[qi * tq:(qi + 1) * tq, :]
            lane = lax.broadcasted_iota(jnp.int32, q.shape, 1)
            zero = jnp.zeros_like(q)
            stacked_q[qi] = jnp.concatenate([jnp.where(lane < ATT_HEAD_DIM, q, zero),
                                             jnp.where(lane >= ATT_HEAD_DIM, q, zero)], axis=0)
        return _dot_nt(k_ref[j * tk:(j + 1) * tk, :], stacked_q[qi])

    steps = [(qi, j) for qi in range(seq // tq) for j in range((qi + 1) * tq // tk)]
    s_next = scores(*steps[0])
    m = l = acc = None
    for t, (qi, j) in enumerate(steps):
        s = s_next
        if t + 1 < len(steps):
            s_next = scores(*steps[t + 1])
        if (j + 1) * tk > qi * tq:
            kv = lax.broadcasted_iota(jnp.int32, s.shape, 0) + (j * tk - qi * tq)
            r = lax.broadcasted_iota(jnp.int32, s.shape, 1)
            r = jnp.where(r >= tq, r - tq, r)
            s = jnp.where(kv <= r, s, -jnp.inf)
        s_max = jnp.max(s, axis=0, keepdims=True)
        if j == 0:
            m = s_max
            p = jnp.exp2(s - m)
            l = jnp.sum(p, axis=0, keepdims=True)
            acc = _dot(vt_scr[j], p.astype(BF16))
        else:
            m_new = jnp.maximum(m, s_max)
            alpha = jnp.exp2(m - m_new)
            p = jnp.exp2(s - m_new)
            l = alpha * l + jnp.sum(p, axis=0, keepdims=True)
            acc = alpha * acc + _dot(vt_scr[j], p.astype(BF16))
            m = m_new
        if (j + 1) * tk == (qi + 1) * tq:
            o = acc / l
            od = o[:, 0:tq] - lam * o[:, tq:2 * tq]
            od = od * lax.rsqrt(jnp.mean(od * od, axis=0, keepdims=True) + RMS_EPS)
            o_ref[qi * tq:(qi + 1) * tq, :] = (od * subw).T.astype(o_ref.dtype)


def _attention(proj, lq1, lk1, lq2, lk2, subw, *, batch, seq, n_heads, col_q, col_k, col_v,
               lam_init, tq=512, tk=512):
    t = batch * seq
    hd = ATT_V_DIM
    kern = functools.partial(_attn_kernel, tq=tq, tk=tk, lam_init=lam_init)
    small = pl.BlockSpec((1, ATT_HEAD_DIM), lambda b, h: (0, 0))
    return pl.pallas_call(
        kern,
        grid=(batch, n_heads),
        in_specs=[
            pl.BlockSpec((seq, hd), lambda b, h: (b, col_q // hd + h)),
            pl.BlockSpec((seq, hd), lambda b, h: (b, col_k // hd + h)),
            pl.BlockSpec((seq, hd), lambda b, h: (b, col_v // hd + h)),
            small, small, small, small,
            pl.BlockSpec((hd, 1), lambda b, h: (0, 0)),
        ],
        out_specs=pl.BlockSpec((seq, hd), lambda b, h: (b, h)),
        out_shape=jax.ShapeDtypeStruct((t, n_heads * hd), BF16),
        scratch_shapes=[pltpu.VMEM((seq // tk, hd, tk), BF16)],
        compiler_params=pltpu.CompilerParams(
            dimension_semantics=("parallel", "parallel"),
            vmem_limit_bytes=_vmem_limit(
                pipelined=4 * _nbytes((seq, hd), BF16),
                resident=_nbytes((seq, hd), BF16),
                live=(4 * _nbytes((tk, 2 * tq), F32) + _nbytes((tk, 2 * tq), BF16)
                      + 2 * _nbytes((hd, 2 * tq), F32)))),
        name="diff_attn",
    )(proj, proj, proj, lq1, lk1, lq2, lk2, subw)


def _mix_kernel(x_ref, ys_ref, ya_ref, npre_ref, npost_ref, wg_ref, wso_ref, wao_ref, wmix_ref, o_ref):
    x = x_ref[...]
    d = x.shape[1]
    h = _rms(x, npre_ref[...]).astype(BF16)
    gates = jax.nn.sigmoid(_dot_nt(h, wg_ref[...]))
    y_ssm = _dot(ys_ref[...], wso_ref[...])
    y_att = _dot(ya_ref[...], wao_ref[...])
    blend = gates[:, 0:d] * y_ssm + gates[:, d:2 * d] * y_att
    mixed = _dot(blend.astype(BF16), wmix_ref[...])
    o_ref[...] = x + _rms(mixed, npost_ref[...])


def _mix(x2d, y_ssm, y_att, n_pre, n_post, w_gate, w_so, w_ao, w_mix, *, tm=512):
    t, d = x2d.shape
    rows = lambda w: pl.BlockSpec((tm, w), lambda i: (i, 0))
    return pl.pallas_call(
        _mix_kernel,
        grid=(t // tm,),
        in_specs=[rows(d), rows(y_ssm.shape[1]), rows(y_att.shape[1]),
                  _const_spec(n_pre.shape), _const_spec(n_post.shape),
                  _const_spec(w_gate.shape), _const_spec(w_so.shape),
                  _const_spec(w_ao.shape), _const_spec(w_mix.shape)],
        out_specs=rows(d),
        out_shape=jax.ShapeDtypeStruct((t, d), F32),
        compiler_params=pltpu.CompilerParams(
            dimension_semantics=("parallel",),
            vmem_limit_bytes=_vmem_limit(
                pipelined=(2 * _nbytes((tm, d), F32) + _nbytes((tm, y_ssm.shape[1]), BF16)
                           + _nbytes((tm, y_att.shape[1]), BF16)),
                resident=sum(_nbytes(w.shape, BF16) for w in (w_gate, w_so, w_ao, w_mix)),
                live=_nbytes((tm, 2 * d), F32) + 4 * _nbytes((tm, d), F32))),
        name="mix",
    )(x2d, y_ssm, y_att, n_pre, n_post, w_gate, w_so, w_ao, w_mix)


def _ffn_kernel(x_ref, npre_ref, npost_ref, wg_ref, wu_ref, wd_ref, o_ref):
    x = x_ref[...]
    h = _rms(x, npre_ref[...]).astype(BF16)
    act = (_silu(_dot(h, wg_ref[...])) * _dot(h, wu_ref[...])).astype(BF16)
    f = _dot(act, wd_ref[...])
    o_ref[...] = x + _rms(f, npost_ref[...])


def _ffn(x2d, n_pre, n_post, w_g, w_u, w_d, *, tm=512):
    t, d = x2d.shape
    rows = pl.BlockSpec((tm, d), lambda i: (i, 0))
    return pl.pallas_call(
        _ffn_kernel,
        grid=(t // tm,),
        in_specs=[rows, _const_spec(n_pre.shape), _const_spec(n_post.shape),
                  _const_spec(w_g.shape), _const_spec(w_u.shape), _const_spec(w_d.shape)],
        out_specs=rows,
        out_shape=jax.ShapeDtypeStruct((t, d), F32),
        compiler_params=pltpu.CompilerParams(
            dimension_semantics=("parallel",),
            vmem_limit_bytes=_vmem_limit(
                pipelined=2 * _nbytes((tm, d), F32),
                resident=sum(_nbytes(w.shape, BF16) for w in (w_g, w_u, w_d)),
                live=2 * _nbytes((tm, w_g.shape[1]), F32) + _nbytes((tm, w_g.shape[1]), BF16))),
        name="ffn",
    )(x2d, n_pre, n_post, w_g, w_u, w_d)


def kernel(x, positions, w_in, conv_w, conv_b, dt_bias, a_log, d_skip, ssm_norm_w, w_ssm_out, lam_q1, lam_k1, lam_q2, lam_k2, attn_subln_w, w_attn_out, w_mix_out, norm_pre_mix, norm_post_mix, norm_pre_ffn, norm_post_ffn, w_ffn_gate, w_ffn_up, w_ffn_down):
    batch, seq, d_model = x.shape
    depth = w_in.shape[0]
    d_inner = w_ssm_out.shape[1]
    n_ssm_heads = dt_bias.shape[1]
    bc_w = 2 * SSM_N_GROUPS * SSM_D_STATE
    att_w = w_attn_out.shape[1]
    n_att_heads = att_w // ATT_V_DIM
    t = batch * seq

    z_end = d_inner
    xbc_end = z_end + d_inner + bc_w
    dt_end = xbc_end + n_ssm_heads
    q_end = dt_end + att_w
    k_end = q_end + att_w
    v_end = k_end + att_w

    col_z, col_xs = 0, d_inner
    col_bc = 2 * d_inner
    col_q = col_bc + bc_w
    col_k = col_q + att_w
    col_v = col_k + att_w

    inv_freq = ROPE_THETA ** (-jnp.arange(0, ROPE_DIM, 2, dtype=F32) / ROPE_DIM)
    head_of_col = np.arange(d_inner) // SSM_HEAD_DIM
    assert n_ssm_heads <= LANES
    e_mat = jnp.asarray(np.arange(LANES)[:, None] == head_of_col[None, :], BF16)
    tri = np.arange(SSM_CHUNK)[None, :] <= np.arange(SSM_CHUNK)[:, None]
    tri = jnp.asarray(np.concatenate([tri, tri, tri], axis=1), BF16)
    L = SSM_CHUNK
    t_out = np.arange(L)[:, None, None]
    tap = np.arange(SSM_CONV)[None, :, None]
    src = t_out - (SSM_CONV - 1) + tap
    src_col = np.where(src >= 0, tap * L + src, SSM_CONV * L + tap * CONV_CARRY + CONV_CARRY + src)
    shift = jnp.asarray((np.arange((SSM_CONV + 1) * L)[None, None, :] == src_col).any(axis=1), BF16)

    def pad_heads(v):
        return jnp.pad(v.astype(F32), (0, LANES - n_ssm_heads))[None, :]

    x2d = x.reshape(t, d_model)
    for l in range(depth):
        lam_init = 0.8 - 0.6 * math.exp(-0.3 * l)
        wl = w_in[l]
        w_t = wl.T.astype(BF16)
        later = [w_ssm_out[l], w_attn_out[l], w_mix_out[l], w_ffn_gate[l], w_ffn_up[l], w_ffn_down[l]]
        proj, dt, w_gate_t, w_so, w_ao, w_mix, w_fg, w_fu, w_fd = _in_proj(
            x2d, positions, norm_pre_mix[l][None, :], w_t, pad_heads(dt_bias[l]), inv_freq, later,
            n_out=col_v + att_w, row_dt=xbc_end, n_dt=n_ssm_heads, row_q=dt_end, row_tail=v_end,
            col_q=col_q, col_k=col_k, col_v=col_v)

        cw = jnp.repeat(conv_w[l].astype(BF16), CONV_CARRY, axis=0)
        y_ssm = _ssd(proj, dt,
                     cw[:, :d_inner], conv_b[l][None, :d_inner],
                     cw[:, d_inner:], conv_b[l][None, d_inner:],
                     pad_heads(a_log[l]), jnp.repeat(d_skip[l], SSM_HEAD_DIM)[None, :],
                     ssm_norm_w[l][None, :], e_mat, tri, shift,
                     batch=batch, seq=seq, d_inner=d_inner, col_xs=col_xs, col_z=col_z, col_bc=col_bc)

        y_att = _attention(proj, lam_q1[l][None, :], lam_k1[l][None, :], lam_q2[l][None, :],
                           lam_k2[l][None, :], attn_subln_w[l][:, None],
                           batch=batch, seq=seq, n_heads=n_att_heads,
                           col_q=col_q, col_k=col_k, col_v=col_v, lam_init=lam_init)

        x2d = _mix(x2d, y_ssm, y_att, norm_pre_mix[l][None, :], norm_post_mix[l][None, :],
                   w_gate_t, w_so, w_ao, w_mix)
        x2d = _ffn(x2d, norm_pre_ffn[l][None, :], norm_post_ffn[l][None, :], w_fg, w_fu, w_fd)
    return x2d.reshape(batch, seq, d_model)
```

```python
import functools
import math

import jax
import jax.numpy as jnp
import numpy as np
from jax import lax
from jax.experimental import pallas as pl
from jax.experimental.pallas import tpu as pltpu

F32 = jnp.float32
BF16 = jnp.bfloat16

LANES = 128

SSM_HEAD_DIM = 64
SSM_N_GROUPS = 4
SSM_D_STATE = 128
SSM_CONV = 4
SSM_CHUNK = 128
CONV_CARRY = 16
ATT_HEAD_DIM = 64
ATT_V_DIM = 2 * ATT_HEAD_DIM
ROPE_THETA = 500000.0
ROPE_DIM = ATT_HEAD_DIM // 4
RMS_EPS = 1e-6
Q_SCALE = ATT_HEAD_DIM ** -0.5 * math.log2(math.e)

MIB = 1024 * 1024
V7X_VMEM_BYTES = 64 * MIB
VMEM_RESERVE = 8 * MIB


def _nbytes(shape, dtype):
    return math.prod(shape) * jnp.dtype(dtype).itemsize


def _vmem_limit(pipelined, resident, live):
    budget = V7X_VMEM_BYTES - VMEM_RESERVE
    need = 2 * pipelined + resident + live
    assert need <= budget, need
    return budget


def _rms(x, w):
    return x * lax.rsqrt(jnp.mean(x * x, axis=-1, keepdims=True) + RMS_EPS) * w


def _silu(x):
    hx = 0.5 * x
    return hx + hx * jnp.tanh(hx)


def _dot(a, b):
    return jnp.dot(a, b, preferred_element_type=F32)


def _dot_nt(a, b):
    return lax.dot_general(a, b, (((1,), (1,)), ((), ())), preferred_element_type=F32)


def _split3(x):
    hi = x.astype(BF16)
    r = x - hi.astype(F32)
    mid = r.astype(BF16)
    lo = (r - mid.astype(F32)).astype(BF16)
    return hi, mid, lo


def _const_spec(shape):
    nd = len(shape)
    return pl.BlockSpec(shape, lambda *_: (0,) * nd, pipeline_mode=pl.Buffered(1))


def _inproj_kernel(x_ref, pos_ref, nw_ref, wt_ref, dtb_ref, invf_ref, rope_e_ref, rope_c_ref,
                   *refs, n_cast, row_dt, n_dt, row_q, row_tail, col_q, col_k, col_v, sub):
    cast_in = refs[:n_cast]
    out_ref, dt_ref, tail_ref = refs[n_cast:n_cast + 3]
    cast_out = refs[n_cast + 3:2 * n_cast + 3]
    h_scr, cos_scr, s1_scr, s2_scr = refs[2 * n_cast + 3:]
    for src, dst in zip(cast_in, cast_out):
        dst[...] = src[...].astype(dst.dtype)
    tail_rows = tail_ref.shape[0]
    tail_start = pl.multiple_of(row_tail + pl.program_id(0) * tail_rows, tail_rows)
    tail_ref[...] = wt_ref[pl.ds(tail_start, tail_rows), :]

    hb = _rms(x_ref[...], nw_ref[...]).astype(BF16)
    h_scr[...] = hb
    dt_lane = lax.broadcasted_iota(jnp.int32, dt_ref.shape, 1)
    dt = jax.nn.softplus(_dot_nt(hb, wt_ref[row_dt:row_dt + LANES, :]) + dtb_ref[...])
    dt_ref[...] = jnp.where(dt_lane < n_dt, dt, 0.0)
    half = ROPE_DIM // 2
    ang = invf_ref[...] * pos_ref[...].astype(F32)
    terms = [t.astype(F32) for t in _split3(jnp.cos(ang)) + _split3(jnp.sin(ang))]
    fill = jnp.zeros((LANES - len(terms) * half, ang.shape[1]), F32)
    grid_t = jnp.concatenate(terms + [fill], axis=0).T.astype(BF16)
    tables = _dot(grid_t, rope_e_ref[...])
    cos_scr[...] = tables[:, 0:LANES] + rope_c_ref[...]
    s1_scr[...] = tables[:, LANES:2 * LANES]
    s2_scr[...] = tables[:, 2 * LANES:3 * LANES]

    for c0 in range(0, out_ref.shape[1], sub):
        r0 = c0 if c0 < col_q else c0 - col_q + row_q
        acc = _dot_nt(h_scr[...], wt_ref[r0:r0 + sub, :])
        if col_q <= c0 < col_v:
            scale = Q_SCALE if c0 < col_k else 1.0
            for c in range(c0, c0 + sub, LANES):
                a = acc[:, c - c0:c - c0 + LANES]
                r = (a * cos_scr[...] + pltpu.roll(a, half, 1) * s1_scr[...]
                     + pltpu.roll(a, LANES - half, 1) * s2_scr[...])
                out_ref[:, c:c + LANES] = (r * scale).astype(out_ref.dtype)
        else:
            out_ref[:, c0:c0 + sub] = acc.astype(out_ref.dtype)


def _in_proj(x2d, positions, norm_w, w_t, dt_bias, inv_freq, casts, *, n_out, row_dt, n_dt, row_q,
             row_tail, col_q, col_k, col_v, tm=512, sub=512):
    t, d = x2d.shape
    steps = t // tm
    assert all(c % sub == 0 for c in (col_q, col_k, col_v, n_out))
    tail_rows = (w_t.shape[0] - row_tail) // steps
    assert tail_rows * steps == w_t.shape[0] - row_tail and tail_rows % 16 == 0 and row_tail % 16 == 0

    def cast_spec(w):
        hold = next(h for h in (1, 2, 4, 8) if (w.shape[0] * h) % (16 * steps) == 0)
        return pl.BlockSpec((w.shape[0] * hold // steps, w.shape[1]), lambda i: (i // hold, 0))

    cast_specs = [cast_spec(w) for w in casts]
    half = ROPE_DIM // 2
    lane = np.arange(LANES) % ATT_HEAD_DIM
    freq_row = np.arange(LANES) % half
    is_cos = np.arange(LANES) < 3 * half
    is_sin = (np.arange(LANES) >= 3 * half) & (np.arange(LANES) < 6 * half)
    same_f = freq_row[:, None] == (lane % half)[None, :]
    rope_e = np.concatenate([
        (is_cos[:, None] & same_f & (lane < ROPE_DIM)[None, :]) * 1.0,
        (is_sin[:, None] & same_f & ((lane >= half) & (lane < ROPE_DIM))[None, :]) * 1.0,
        (is_sin[:, None] & same_f & (lane < half)[None, :]) * -1.0], axis=1)
    rope_c = (lane >= ROPE_DIM)[None, :] * 1.0
    kern = functools.partial(_inproj_kernel, n_cast=len(casts), row_dt=row_dt, n_dt=n_dt, row_q=row_q,
                             row_tail=row_tail, col_q=col_q, col_k=col_k, col_v=col_v, sub=sub)
    return pl.pallas_call(
        kern,
        grid=(t // tm,),
        in_specs=[
            pl.BlockSpec((tm, d), lambda i: (i, 0)),
            pl.BlockSpec((None, 1, tm), lambda i: (i, 0, 0)),
            _const_spec((1, d)),
            _const_spec(w_t.shape),
            _const_spec((1, LANES)),
            _const_spec((half, 1)),
            _const_spec(rope_e.shape),
            _const_spec(rope_c.shape),
        ] + cast_specs,
        out_specs=[
            pl.BlockSpec((tm, n_out), lambda i: (i, 0)),
            pl.BlockSpec((tm, LANES), lambda i: (i, 0)),
            pl.BlockSpec((tail_rows, d), lambda i: (i, 0)),
        ] + cast_specs,
        out_shape=[
            jax.ShapeDtypeStruct((t, n_out), BF16),
            jax.ShapeDtypeStruct((t, LANES), F32),
            jax.ShapeDtypeStruct((w_t.shape[0] - row_tail, d), BF16),
        ] + [jax.ShapeDtypeStruct(w.shape, BF16) for w in casts],
        scratch_shapes=[
            pltpu.VMEM((tm, d), BF16),
            pltpu.VMEM((tm, LANES), F32),
            pltpu.VMEM((tm, LANES), F32),
            pltpu.VMEM((tm, LANES), F32),
        ],
        compiler_params=pltpu.CompilerParams(
            dimension_semantics=("arbitrary",),
            vmem_limit_bytes=_vmem_limit(
                pipelined=(_nbytes((tm, d), F32) + _nbytes((tm, n_out), BF16) + _nbytes((tm, LANES), F32)
                           + _nbytes((tail_rows, d), BF16)
                           + sum(_nbytes(s.block_shape, F32) + _nbytes(s.block_shape, BF16) for s in cast_specs)),
                resident=_nbytes(w_t.shape, BF16) + _nbytes((tm, d), BF16) + 3 * _nbytes((tm, LANES), F32),
                live=3 * _nbytes((tm, sub), F32))),
        name="in_proj",
    )(x2d, positions.reshape(t // tm, 1, tm), norm_w, w_t, dt_bias, inv_freq[:, None],
      jnp.asarray(rope_e, BF16), jnp.asarray(rope_c, F32), *casts)


def _ssd_kernel(xs_ref, z_ref, bc_ref, dt_ref, cwx_ref, cbx_ref, cwbc_ref, cbbc_ref,
                alog_ref, dskip_ref, nw_ref, e_ref, tri_ref, shift_ref,
                y_ref, xprev, bcprev, hstate, *, n_groups, d_state, head_dim, chunks):
    L = SSM_CHUNK
    d_inner = xs_ref.shape[1]
    gw = d_inner // n_groups
    heads_per_group = gw // head_dim
    pairs_per_group = heads_per_group // 2

    @pl.when(pl.program_id(1) == 0)
    def _():
        xprev[...] = jnp.zeros(xprev.shape, BF16)
        bcprev[...] = jnp.zeros(bcprev.shape, BF16)
        hstate[...] = jnp.zeros(hstate.shape, F32)

    a_neg = -jnp.exp(alog_ref[...]) * math.log2(math.e)
    row = lax.broadcasted_iota(jnp.int32, (L, L), 0)
    col = lax.broadcasted_iota(jnp.int32, (L, L), 1)
    causal = col <= row
    lane = lax.broadcasted_iota(jnp.int32, (L, LANES), 1)

    def conv_silu(cur_ref, prev, w_ref, b_ref, r0):
        body = cur_ref[r0:r0 + L - CONV_CARRY, :]
        last = cur_ref[r0 + L - CONV_CARRY:r0 + L, :]
        old = prev[...] if r0 == 0 else cur_ref[r0 - CONV_CARRY:r0, :]
        row = lax.broadcasted_iota(jnp.int32, last.shape, 0)
        taps = []
        for k in range(SSM_CONV):
            back = SSM_CONV - 1 - k
            wk = w_ref[k * CONV_CARRY:(k + 1) * CONV_CARRY, :]
            taps.append(body * jnp.concatenate([wk] * (L // CONV_CARRY - 1), axis=0))
            taps.append(jnp.where(row >= CONV_CARRY - back, old, last) * wk)
        acc = b_ref[...] + _dot(shift_ref[...], jnp.concatenate(taps, axis=0))
        return _silu(acc)

    h = [hstate[g] for g in range(n_groups)]
    for ci in range(chunks):
        r0 = ci * L
        xs = conv_silu(xs_ref, xprev, cwx_ref, cbx_ref, r0)
        bc = conv_silu(bc_ref, bcprev, cwbc_ref, cbbc_ref, r0).astype(BF16)

        dt = dt_ref[r0:r0 + L, :]
        a_cs = _dot(tri_ref[...], jnp.concatenate(_split3(dt * a_neg), axis=0))
        a_cs_t = a_cs.T
        ea = jnp.exp2(a_cs)
        dte = jnp.exp2(a_cs[L - 1:L, :] - a_cs)
        stacked = jnp.concatenate([dt, ea, dte], axis=0).astype(BF16)
        expanded = _dot(stacked, e_ref[...])
        dt_e = expanded[0:L]
        ea_e = expanded[L:2 * L]
        dte_e = expanded[2 * L:3 * L]

        xdt = xs * dt_e
        xdt_b = xdt.astype(BF16)
        xw_b = (xdt * dte_e).astype(BF16)
        gate = _silu(z_ref[r0:r0 + L, :].astype(F32))

        for g in range(n_groups):
            b_g = bc[:, g * d_state:(g + 1) * d_state]
            c_g = bc[:, (n_groups + g) * d_state:(n_groups + g + 1) * d_state]
            cb = _dot_nt(c_g, b_g)
            y_diag = []
            for p2 in range(pairs_per_group):
                pair = g * pairs_per_group + p2
                ms = []
                for hh in (2 * pair, 2 * pair + 1):
                    seg = a_cs[:, hh:hh + 1] - a_cs_t[hh:hh + 1, :]
                    dec = jnp.exp2(jnp.where(causal, seg, -jnp.inf))
                    ms.append((cb * dec).astype(BF16))
                lhs = jnp.concatenate(ms, axis=1)
                xp = xdt_b[:, pair * LANES:(pair + 1) * LANES]
                zero = jnp.zeros_like(xp)
                rhs = jnp.concatenate([jnp.where(lane < head_dim, xp, zero),
                                       jnp.where(lane >= head_dim, xp, zero)], axis=0)
                y_diag.append(_dot(lhs, rhs))
            sl = slice(g * gw, (g + 1) * gw)
            ea_g = ea_e[:, sl]
            y_off = _dot(c_g, h[g].astype(BF16)) * ea_g
            h[g] = h[g] * ea_g[L - 1:L, :] + _dot(b_g.T, xw_b[:, sl])
            yg = (jnp.concatenate(y_diag, axis=1) + y_off + xs[:, sl] * dskip_ref[:, sl]) * gate[:, sl]
            y_ref[r0:r0 + L, sl] = _rms(yg, nw_ref[:, sl]).astype(y_ref.dtype)

    for g in range(n_groups):
        hstate[g] = h[g]
    xprev[...] = xs_ref[chunks * L - CONV_CARRY:chunks * L, :]
    bcprev[...] = bc_ref[chunks * L - CONV_CARRY:chunks * L, :]


def _ssd(proj, dt, cw_x, cb_x, cw_bc, cb_bc, a_log, d_skip_e, norm_w, e_mat, tri, shift,
         *, batch, seq, d_inner, col_xs, col_z, col_bc, chunks=2):
    L = SSM_CHUNK
    rows = chunks * L
    nc = seq // rows
    t = batch * seq
    bc_w = 2 * SSM_N_GROUPS * SSM_D_STATE
    row = lambda b, c: b * nc + c
    kern = functools.partial(_ssd_kernel, n_groups=SSM_N_GROUPS, d_state=SSM_D_STATE,
                             head_dim=SSM_HEAD_DIM, chunks=chunks)
    return pl.pallas_call(
        kern,
        grid=(batch, nc),
        in_specs=[
            pl.BlockSpec((rows, d_inner), lambda b, c: (row(b, c), col_xs // d_inner)),
            pl.BlockSpec((rows, d_inner), lambda b, c: (row(b, c), col_z // d_inner)),
            pl.BlockSpec((rows, bc_w), lambda b, c: (row(b, c), col_bc // bc_w)),
            pl.BlockSpec((rows, LANES), lambda b, c: (row(b, c), 0)),
            pl.BlockSpec((SSM_CONV * CONV_CARRY, d_inner), lambda b, c: (0, 0)),
            pl.BlockSpec((1, d_inner), lambda b, c: (0, 0)),
            pl.BlockSpec((SSM_CONV * CONV_CARRY, bc_w), lambda b, c: (0, 0)),
            pl.BlockSpec((1, bc_w), lambda b, c: (0, 0)),
            pl.BlockSpec((1, LANES), lambda b, c: (0, 0)),
            pl.BlockSpec((1, d_inner), lambda b, c: (0, 0)),
            pl.BlockSpec((1, d_inner), lambda b, c: (0, 0)),
            pl.BlockSpec((LANES, d_inner), lambda b, c: (0, 0)),
            pl.BlockSpec((L, 3 * L), lambda b, c: (0, 0)),
            pl.BlockSpec(shift.shape, lambda b, c: (0, 0)),
        ],
        out_specs=pl.BlockSpec((rows, d_inner), lambda b, c: (row(b, c), 0)),
        out_shape=jax.ShapeDtypeStruct((t, d_inner), BF16),
        scratch_shapes=[
            pltpu.VMEM((CONV_CARRY, d_inner), BF16),
            pltpu.VMEM((CONV_CARRY, bc_w), BF16),
            pltpu.VMEM((SSM_N_GROUPS, SSM_D_STATE, d_inner // SSM_N_GROUPS), F32),
        ],
        compiler_params=pltpu.CompilerParams(
            dimension_semantics=("parallel", "arbitrary"),
            vmem_limit_bytes=_vmem_limit(
                pipelined=(3 * _nbytes((rows, d_inner), BF16) + _nbytes((rows, bc_w), BF16)
                           + _nbytes((rows, LANES), F32) + _nbytes(cw_x.shape, BF16) + _nbytes(cw_bc.shape, BF16)
                           + _nbytes(e_mat.shape, BF16) + _nbytes(tri.shape, BF16) + _nbytes(shift.shape, BF16)),
                resident=_nbytes((SSM_D_STATE, d_inner), F32),
                live=chunks * (_nbytes((3 * L, d_inner), F32) + _nbytes((L, d_inner + bc_w), F32)
                               + 2 * _nbytes((L, d_inner), F32)))),
        name="ssd",
    )(proj, proj, proj, dt, cw_x, cb_x, cw_bc, cb_bc, a_log, d_skip_e, norm_w, e_mat, tri, shift)


def _attn_kernel(q_ref, k_ref, v_ref, lq1_ref, lk1_ref, lq2_ref, lk2_ref, subw_ref,
                 o_ref, vt_scr, *, tq, tk, lam_init):
    seq = q_ref.shape[0]
    for jb in range(seq // tk):
        vt_scr[jb] = v_ref[jb * tk:(jb + 1) * tk, :].T

    lam = (jnp.exp(jnp.sum(lq1_ref[...] * lk1_ref[...]))
           - jnp.exp(jnp.sum(lq2_ref[...] * lk2_ref[...])) + lam_init)
    subw = subw_ref[...] * (1.0 - lam_init)

    stacked_q = {}

    def scores(qi, j):
        if qi not in stacked_q:
            q = q_ref[qi * tq:(qi + 1) * tq, :]
            lane = lax.broadcasted_iota(jnp.int32, q.shape, 1)
            zero = jnp.zeros_like(q)
            stacked_q[qi] = jnp.concatenate([jnp.where(lane < ATT_HEAD_DIM, q, zero),
                                             jnp.where(lane >= ATT_HEAD_DIM, q, zero)], axis=0)
        return _dot_nt(k_ref[j * tk:(j + 1) * tk, :], stacked_q[qi])

    steps = [(qi, j) for qi in range(seq // tq) for j in range((qi + 1) * tq // tk)]
    s_next = scores(*steps[0])
    m = l = acc = None
    for t, (qi, j) in enumerate(steps):
        s = s_next
        if t + 1 < len(steps):
            s_next = scores(*steps[t + 1])
        if (j + 1) * tk > qi * tq:
            kv = lax.broadcasted_iota(jnp.int32, s.shape, 0) + (j * tk - qi * tq)
            r = lax.broadcasted_iota(jnp.int32, s.shape, 1)
            r = jnp.where(r >= tq, r - tq, r)
            s = jnp.where(kv <= r, s, -jnp.inf)
        s_max = jnp.max(s, axis=0, keepdims=True)
        if j == 0:
            m = s_max
            p = jnp.exp2(s - m)
            l = jnp.sum(p, axis=0, keepdims=True)
            acc = _dot(vt_scr[j], p.astype(BF16))
        else:
            m_new = jnp.maximum(m, s_max)
            alpha = jnp.exp2(m - m_new)
            p = jnp.exp2(s - m_new)
            l = alpha * l + jnp.sum(p, axis=0, keepdims=True)
            acc = alpha * acc + _dot(vt_scr[j], p.astype(BF16))
            m = m_new
        if (j + 1) * tk == (qi + 1) * tq:
            o = acc / l
            od = o[:, 0:tq] - lam * o[:, tq:2 * tq]
            od = od * lax.rsqrt(jnp.mean(od * od, axis=0, keepdims=True) + RMS_EPS)
            o_ref[qi * tq:(qi + 1) * tq, :] = (od * subw).T.astype(o_ref.dtype)


def _attention(proj, lq1, lk1, lq2, lk2, subw, *, batch, seq, n_heads, col_q, col_k, col_v,
               lam_init, tq=512, tk=512):
    t = batch * seq
    hd = ATT_V_DIM
    kern = functools.partial(_attn_kernel, tq=tq, tk=tk, lam_init=lam_init)
    small = pl.BlockSpec((1, ATT_HEAD_DIM), lambda b, h: (0, 0))
    return pl.pallas_call(
        kern,
        grid=(batch, n_heads),
        in_specs=[
            pl.BlockSpec((seq, hd), lambda b, h: (b, col_q // hd + h)),
            pl.BlockSpec((seq, hd), lambda b, h: (b, col_k // hd + h)),
            pl.BlockSpec((seq, hd), lambda b, h: (b, col_v // hd + h)),
            small, small, small, small,
            pl.BlockSpec((hd, 1), lambda b, h: (0, 0)),
        ],
        out_specs=pl.BlockSpec((seq, hd), lambda b, h: (b, h)),
        out_shape=jax.ShapeDtypeStruct((t, n_heads * hd), BF16),
        scratch_shapes=[pltpu.VMEM((seq // tk, hd, tk), BF16)],
        compiler_params=pltpu.CompilerParams(
            dimension_semantics=("parallel", "parallel"),
            vmem_limit_bytes=_vmem_limit(
                pipelined=4 * _nbytes((seq, hd), BF16),
                resident=_nbytes((seq, hd), BF16),
                live=(4 * _nbytes((tk, 2 * tq), F32) + _nbytes((tk, 2 * tq), BF16)
                      + 2 * _nbytes((hd, 2 * tq), F32)))),
        name="diff_attn",
    )(proj, proj, proj, lq1, lk1, lq2, lk2, subw)


def _mix_kernel(x_ref, ys_ref, ya_ref, npre_ref, npost_ref, wg_ref, wso_ref, wao_ref, wmix_ref, o_ref):
    x = x_ref[...]
    d = x.shape[1]
    h = _rms(x, npre_ref[...]).astype(BF16)
    gates = jax.nn.sigmoid(_dot_nt(h, wg_ref[...]))
    y_ssm = _dot(ys_ref[...], wso_ref[...])
    y_att = _dot(ya_ref[...], wao_ref[...])
    blend = gates[:, 0:d] * y_ssm + gates[:, d:2 * d] * y_att
    mixed = _dot(blend.astype(BF16), wmix_ref[...])
    o_ref[...] = x + _rms(mixed, npost_ref[...])


def _mix(x2d, y_ssm, y_att, n_pre, n_post, w_gate, w_so, w_ao, w_mix, *, tm=512):
    t, d = x2d.shape
    rows = lambda w: pl.BlockSpec((tm, w), lambda i: (i, 0))
    return pl.pallas_call(
        _mix_kernel,
        grid=(t // tm,),
        in_specs=[rows(d), rows(y_ssm.shape[1]), rows(y_att.shape[1]),
                  _const_spec(n_pre.shape), _const_spec(n_post.shape),
                  _const_spec(w_gate.shape), _const_spec(w_so.shape),
                  _const_spec(w_ao.shape), _const_spec(w_mix.shape)],
        out_specs=rows(d),
        out_shape=jax.ShapeDtypeStruct((t, d), F32),
        compiler_params=pltpu.CompilerParams(
            dimension_semantics=("parallel",),
            vmem_limit_bytes=_vmem_limit(
                pipelined=(2 * _nbytes((tm, d), F32) + _nbytes((tm, y_ssm.shape[1]), BF16)
                           + _nbytes((tm, y_att.shape[1]), BF16)),
                resident=sum(_nbytes(w.shape, BF16) for w in (w_gate, w_so, w_ao, w_mix)),
                live=_nbytes((tm, 2 * d), F32) + 4 * _nbytes((tm, d), F32))),
        name="mix",
    )(x2d, y_ssm, y_att, n_pre, n_post, w_gate, w_so, w_ao, w_mix)


def _ffn_kernel(x_ref, npre_ref, npost_ref, wg_ref, wu_ref, wd_ref, o_ref):
    x = x_ref[...]
    h = _rms(x, npre_ref[...]).astype(BF16)
    act = (_silu(_dot(h, wg_ref[...])) * _dot(h, wu_ref[...])).astype(BF16)
    f = _dot(act, wd_ref[...])
    o_ref[...] = x + _rms(f, npost_ref[...])


def _ffn(x2d, n_pre, n_post, w_g, w_u, w_d, *, tm=512):
    t, d = x2d.shape
    rows = pl.BlockSpec((tm, d), lambda i: (i, 0))
    return pl.pallas_call(
        _ffn_kernel,
        grid=(t // tm,),
        in_specs=[rows, _const_spec(n_pre.shape), _const_spec(n_post.shape),
                  _const_spec(w_g.shape), _const_spec(w_u.shape), _const_spec(w_d.shape)],
        out_specs=rows,
        out_shape=jax.ShapeDtypeStruct((t, d), F32),
        compiler_params=pltpu.CompilerParams(
            dimension_semantics=("parallel",),
            vmem_limit_bytes=_vmem_limit(
                pipelined=2 * _nbytes((tm, d), F32),
                resident=sum(_nbytes(w.shape, BF16) for w in (w_g, w_u, w_d)),
                live=2 * _nbytes((tm, w_g.shape[1]), F32) + _nbytes((tm, w_g.shape[1]), BF16))),
        name="ffn",
    )(x2d, n_pre, n_post, w_g, w_u, w_d)


def kernel(x, positions, w_in, conv_w, conv_b, dt_bias, a_log, d_skip, ssm_norm_w, w_ssm_out, lam_q1, lam_k1, lam_q2, lam_k2, attn_subln_w, w_attn_out, w_mix_out, norm_pre_mix, norm_post_mix, norm_pre_ffn, norm_post_ffn, w_ffn_gate, w_ffn_up, w_ffn_down):
    batch, seq, d_model = x.shape
    depth = w_in.shape[0]
    d_inner = w_ssm_out.shape[1]
    n_ssm_heads = dt_bias.shape[1]
    bc_w = 2 * SSM_N_GROUPS * SSM_D_STATE
    att_w = w_attn_out.shape[1]
    n_att_heads = att_w // ATT_V_DIM
    t = batch * seq

    z_end = d_inner
    xbc_end = z_end + d_inner + bc_w
    dt_end = xbc_end + n_ssm_heads
    q_end = dt_end + att_w
    k_end = q_end + att_w
    v_end = k_end + att_w

    col_z, col_xs = 0, d_inner
    col_bc = 2 * d_inner
    col_q = col_bc + bc_w
    col_k = col_q + att_w
    col_v = col_k + att_w

    inv_freq = ROPE_THETA ** (-jnp.arange(0, ROPE_DIM, 2, dtype=F32) / ROPE_DIM)
    head_of_col = np.arange(d_inner) // SSM_HEAD_DIM
    assert n_ssm_heads <= LANES
    e_mat = jnp.asarray(np.arange(LANES)[:, None] == head_of_col[None, :], BF16)
    tri = np.arange(SSM_CHUNK)[None, :] <= np.arange(SSM_CHUNK)[:, None]
    tri = jnp.asarray(np.concatenate([tri, tri, tri], axis=1), BF16)
    L = SSM_CHUNK
    t_out = np.arange(L)[:, None, None]
    tap = np.arange(SSM_CONV)[None, :, None]
    src = t_out - (SSM_CONV - 1) + tap
    src_col = tap * L + np.where(src >= 0, src, L + src)
    shift = jnp.asarray((np.arange(SSM_CONV * L)[None, None, :] == src_col).any(axis=1), BF16)

    def pad_heads(v):
        return jnp.pad(v.astype(F32), (0, LANES - n_ssm_heads))[None, :]

    x2d = x.reshape(t, d_model)
    for l in range(depth):
        lam_init = 0.8 - 0.6 * math.exp(-0.3 * l)
        wl = w_in[l]
        w_t = wl.T.astype(BF16)
        later = [w_ssm_out[l], w_attn_out[l], w_mix_out[l], w_ffn_gate[l], w_ffn_up[l], w_ffn_down[l]]
        proj, dt, w_gate_t, w_so, w_ao, w_mix, w_fg, w_fu, w_fd = _in_proj(
            x2d, positions, norm_pre_mix[l][None, :], w_t, pad_heads(dt_bias[l]), inv_freq, later,
            n_out=col_v + att_w, row_dt=xbc_end, n_dt=n_ssm_heads, row_q=dt_end, row_tail=v_end,
            col_q=col_q, col_k=col_k, col_v=col_v)

        cw = jnp.repeat(conv_w[l].astype(BF16), CONV_CARRY, axis=0)
        y_ssm = _ssd(proj, dt,
                     cw[:, :d_inner], conv_b[l][None, :d_inner],
                     cw[:, d_inner:], conv_b[l][None, d_inner:],
                     pad_heads(a_log[l]), jnp.repeat(d_skip[l], SSM_HEAD_DIM)[None, :],
                     ssm_norm_w[l][None, :], e_mat, tri, shift,
                     batch=batch, seq=seq, d_inner=d_inner, col_xs=col_xs, col_z=col_z, col_bc=col_bc)

        y_att = _attention(proj, lam_q1[l][None, :], lam_k1[l][None, :], lam_q2[l][None, :],
                           lam_k2[l][None, :], attn_subln_w[l][:, None],
                           batch=batch, seq=seq, n_heads=n_att_heads,
                           col_q=col_q, col_k=col_k, col_v=col_v, lam_init=lam_init)

        x2d = _mix(x2d, y_ssm, y_att, norm_pre_mix[l][None, :], norm_post_mix[l][None, :],
                   w_gate_t, w_so, w_ao, w_mix)
        x2d = _ffn(x2d, norm_pre_ffn[l][None, :], norm_post_ffn[l][None, :], w_fg, w_fu, w_fd)
    return x2d.reshape(batch, seq, d_model)
```

```python
import functools
import math

import jax
import jax.numpy as jnp
import numpy as np
from jax import lax
from jax.experimental import pallas as pl
from jax.experimental.pallas import tpu as pltpu

F32 = jnp.float32
BF16 = jnp.bfloat16

LANES = 128

SSM_HEAD_DIM = 64
SSM_N_GROUPS = 4
SSM_D_STATE = 128
SSM_CONV = 4
SSM_CHUNK = 128
CONV_CARRY = 16
ATT_HEAD_DIM = 64
ATT_V_DIM = 2 * ATT_HEAD_DIM
ROPE_THETA = 500000.0
ROPE_DIM = ATT_HEAD_DIM // 4
RMS_EPS = 1e-6
Q_SCALE = ATT_HEAD_DIM ** -0.5 * math.log2(math.e)

MIB = 1024 * 1024
V7X_VMEM_BYTES = 64 * MIB
VMEM_RESERVE = 8 * MIB


def _nbytes(shape, dtype):
    return math.prod(shape) * jnp.dtype(dtype).itemsize


def _vmem_limit(pipelined, resident, live):
    budget = V7X_VMEM_BYTES - VMEM_RESERVE
    need = 2 * pipelined + resident + live
    assert need <= budget, need
    return budget


def _rms(x, w):
    return x * lax.rsqrt(jnp.mean(x * x, axis=-1, keepdims=True) + RMS_EPS) * w


def _silu(x):
    hx = 0.5 * x
    return hx + hx * jnp.tanh(hx)


def _dot(a, b):
    return jnp.dot(a, b, preferred_element_type=F32)


def _dot_nt(a, b):
    return lax.dot_general(a, b, (((1,), (1,)), ((), ())), preferred_element_type=F32)


def _split3(x):
    hi = x.astype(BF16)
    r = x - hi.astype(F32)
    mid = r.astype(BF16)
    lo = (r - mid.astype(F32)).astype(BF16)
    return hi, mid, lo


def _const_spec(shape):
    nd = len(shape)
    return pl.BlockSpec(shape, lambda *_: (0,) * nd, pipeline_mode=pl.Buffered(1))


def _inproj_kernel(x_ref, pos_ref, nw_ref, wt_ref, dtb_ref, invf_ref, rope_e_ref, rope_c_ref,
                   *refs, n_cast, row_dt, n_dt, row_q, row_tail, col_q, col_k, col_v, sub):
    cast_in = refs[:n_cast]
    out_ref, dt_ref, tail_ref = refs[n_cast:n_cast + 3]
    cast_out = refs[n_cast + 3:2 * n_cast + 3]
    h_scr, cos_scr, s1_scr, s2_scr = refs[2 * n_cast + 3:]
    for src, dst in zip(cast_in, cast_out):
        dst[...] = src[...].astype(dst.dtype)
    tail_rows = tail_ref.shape[0]
    tail_start = pl.multiple_of(row_tail + pl.program_id(0) * tail_rows, tail_rows)
    tail_ref[...] = wt_ref[pl.ds(tail_start, tail_rows), :]

    hb = _rms(x_ref[...], nw_ref[...]).astype(BF16)
    h_scr[...] = hb
    dt_lane = lax.broadcasted_iota(jnp.int32, dt_ref.shape, 1)
    dt = jax.nn.softplus(_dot_nt(hb, wt_ref[row_dt:row_dt + LANES, :]) + dtb_ref[...])
    dt_ref[...] = jnp.where(dt_lane < n_dt, dt, 0.0)
    half = ROPE_DIM // 2
    ang = invf_ref[...] * pos_ref[...].astype(F32)
    terms = [t.astype(F32) for t in _split3(jnp.cos(ang)) + _split3(jnp.sin(ang))]
    fill = jnp.zeros((LANES - len(terms) * half, ang.shape[1]), F32)
    grid_t = jnp.concatenate(terms + [fill], axis=0).T.astype(BF16)
    tables = _dot(grid_t, rope_e_ref[...])
    cos_scr[...] = tables[:, 0:LANES] + rope_c_ref[...]
    s1_scr[...] = tables[:, LANES:2 * LANES]
    s2_scr[...] = tables[:, 2 * LANES:3 * LANES]

    for c0 in range(0, out_ref.shape[1], sub):
        r0 = c0 if c0 < col_q else c0 - col_q + row_q
        acc = _dot_nt(h_scr[...], wt_ref[r0:r0 + sub, :])
        if col_q <= c0 < col_v:
            scale = Q_SCALE if c0 < col_k else 1.0
            for c in range(c0, c0 + sub, LANES):
                a = acc[:, c - c0:c - c0 + LANES]
                r = (a * cos_scr[...] + pltpu.roll(a, half, 1) * s1_scr[...]
                     + pltpu.roll(a, LANES - half, 1) * s2_scr[...])
                out_ref[:, c:c + LANES] = (r * scale).astype(out_ref.dtype)
        else:
            out_ref[:, c0:c0 + sub] = acc.astype(out_ref.dtype)


def _in_proj(x2d, positions, norm_w, w_t, dt_bias, inv_freq, casts, *, n_out, row_dt, n_dt, row_q,
             row_tail, col_q, col_k, col_v, tm=512, sub=512):
    t, d = x2d.shape
    steps = t // tm
    assert all(c % sub == 0 for c in (col_q, col_k, col_v, n_out))
    tail_rows = (w_t.shape[0] - row_tail) // steps
    assert tail_rows * steps == w_t.shape[0] - row_tail and tail_rows % 16 == 0 and row_tail % 16 == 0

    def cast_spec(w):
        hold = next(h for h in (1, 2, 4, 8) if (w.shape[0] * h) % (16 * steps) == 0)
        return pl.BlockSpec((w.shape[0] * hold // steps, w.shape[1]), lambda i: (i // hold, 0))

    cast_specs = [cast_spec(w) for w in casts]
    half = ROPE_DIM // 2
    lane = np.arange(LANES) % ATT_HEAD_DIM
    freq_row = np.arange(LANES) % half
    is_cos = np.arange(LANES) < 3 * half
    is_sin = (np.arange(LANES) >= 3 * half) & (np.arange(LANES) < 6 * half)
    same_f = freq_row[:, None] == (lane % half)[None, :]
    rope_e = np.concatenate([
        (is_cos[:, None] & same_f & (lane < ROPE_DIM)[None, :]) * 1.0,
        (is_sin[:, None] & same_f & ((lane >= half) & (lane < ROPE_DIM))[None, :]) * 1.0,
        (is_sin[:, None] & same_f & (lane < half)[None, :]) * -1.0], axis=1)
    rope_c = (lane >= ROPE_DIM)[None, :] * 1.0
    kern = functools.partial(_inproj_kernel, n_cast=len(casts), row_dt=row_dt, n_dt=n_dt, row_q=row_q,
                             row_tail=row_tail, col_q=col_q, col_k=col_k, col_v=col_v, sub=sub)
    return pl.pallas_call(
        kern,
        grid=(t // tm,),
        in_specs=[
            pl.BlockSpec((tm, d), lambda i: (i, 0)),
            pl.BlockSpec((None, 1, tm), lambda i: (i, 0, 0)),
            _const_spec((1, d)),
            _const_spec(w_t.shape),
            _const_spec((1, LANES)),
            _const_spec((half, 1)),
            _const_spec(rope_e.shape),
            _const_spec(rope_c.shape),
        ] + cast_specs,
        out_specs=[
            pl.BlockSpec((tm, n_out), lambda i: (i, 0)),
            pl.BlockSpec((tm, LANES), lambda i: (i, 0)),
            pl.BlockSpec((tail_rows, d), lambda i: (i, 0)),
        ] + cast_specs,
        out_shape=[
            jax.ShapeDtypeStruct((t, n_out), BF16),
            jax.ShapeDtypeStruct((t, LANES), F32),
            jax.ShapeDtypeStruct((w_t.shape[0] - row_tail, d), BF16),
        ] + [jax.ShapeDtypeStruct(w.shape, BF16) for w in casts],
        scratch_shapes=[
            pltpu.VMEM((tm, d), BF16),
            pltpu.VMEM((tm, LANES), F32),
            pltpu.VMEM((tm, LANES), F32),
            pltpu.VMEM((tm, LANES), F32),
        ],
        compiler_params=pltpu.CompilerParams(
            dimension_semantics=("arbitrary",),
            vmem_limit_bytes=_vmem_limit(
                pipelined=(_nbytes((tm, d), F32) + _nbytes((tm, n_out), BF16) + _nbytes((tm, LANES), F32)
                           + _nbytes((tail_rows, d), BF16)
                           + sum(_nbytes(s.block_shape, F32) + _nbytes(s.block_shape, BF16) for s in cast_specs)),
                resident=_nbytes(w_t.shape, BF16) + _nbytes((tm, d), BF16) + 3 * _nbytes((tm, LANES), F32),
                live=3 * _nbytes((tm, sub), F32))),
        name="in_proj",
    )(x2d, positions.reshape(t // tm, 1, tm), norm_w, w_t, dt_bias, inv_freq[:, None],
      jnp.asarray(rope_e, BF16), jnp.asarray(rope_c, F32), *casts)


def _ssd_kernel(xs_ref, z_ref, bc_ref, dt_ref, cwx_ref, cbx_ref, cwbc_ref, cbbc_ref,
                alog_ref, dskip_ref, nw_ref, e_ref, tri_ref, shift_ref,
                y_ref, xprev, bcprev, hstate, *, n_groups, d_state, head_dim, chunks):
    L = SSM_CHUNK
    d_inner = xs_ref.shape[1]
    gw = d_inner // n_groups
    heads_per_group = gw // head_dim
    pairs_per_group = heads_per_group // 2

    @pl.when(pl.program_id(1) == 0)
    def _():
        xprev[...] = jnp.zeros(xprev.shape, BF16)
        bcprev[...] = jnp.zeros(bcprev.shape, BF16)
        hstate[...] = jnp.zeros(hstate.shape, F32)

    a_neg = -jnp.exp(alog_ref[...]) * math.log2(math.e)
    row = lax.broadcasted_iota(jnp.int32, (L, L), 0)
    col = lax.broadcasted_iota(jnp.int32, (L, L), 1)
    causal = col <= row
    lane = lax.broadcasted_iota(jnp.int32, (L, LANES), 1)

    def conv_silu(cur_ref, prev, w_ref, b_ref, r0):
        body = cur_ref[r0:r0 + L - CONV_CARRY, :]
        last = cur_ref[r0 + L - CONV_CARRY:r0 + L, :]
        old = prev[...] if r0 == 0 else cur_ref[r0 - CONV_CARRY:r0, :]
        row = lax.broadcasted_iota(jnp.int32, last.shape, 0)
        taps = []
        for k in range(SSM_CONV):
            back = SSM_CONV - 1 - k
            wk = w_ref[k * CONV_CARRY:(k + 1) * CONV_CARRY, :]
            taps.append(body * jnp.concatenate([wk] * (L // CONV_CARRY - 1), axis=0))
            taps.append(jnp.where(row >= CONV_CARRY - back, old, last) * wk)
        acc = b_ref[...] + _dot(shift_ref[...], jnp.concatenate(taps, axis=0))
        return _silu(acc)

    h = [hstate[g] for g in range(n_groups)]
    for ci in range(chunks):
        r0 = ci * L
        xs = conv_silu(xs_ref, xprev, cwx_ref, cbx_ref, r0)
        bc = conv_silu(bc_ref, bcprev, cwbc_ref, cbbc_ref, r0).astype(BF16)

        dt = dt_ref[r0:r0 + L, :]
        a_cs = _dot(tri_ref[...], jnp.concatenate(_split3(dt * a_neg), axis=0))
        a_cs_t = a_cs.T
        ea = jnp.exp2(a_cs)
        dte = jnp.exp2(a_cs[L - 1:L, :] - a_cs)
        stacked = jnp.concatenate([ea, dt * dte], axis=0).astype(BF16)
        expanded = _dot(stacked, e_ref[...])
        ea_e = expanded[0:L]
        xs_b = xs.astype(BF16)
        xw_b = (xs * expanded[L:2 * L]).astype(BF16)
        dt_t = dt.T
        gate = _silu(z_ref[r0:r0 + L, :].astype(F32))

        for g in range(n_groups):
            b_g = bc[:, g * d_state:(g + 1) * d_state]
            c_g = bc[:, (n_groups + g) * d_state:(n_groups + g + 1) * d_state]
            cb = _dot_nt(c_g, b_g)
            y_diag = []
            for p2 in range(pairs_per_group):
                pair = g * pairs_per_group + p2
                ms = []
                for hh in (2 * pair, 2 * pair + 1):
                    seg = a_cs[:, hh:hh + 1] - a_cs_t[hh:hh + 1, :]
                    dec = jnp.exp2(jnp.where(causal, seg, -jnp.inf))
                    ms.append((cb * dec * dt_t[hh:hh + 1, :]).astype(BF16))
                lhs = jnp.concatenate(ms, axis=1)
                xp = xs_b[:, pair * LANES:(pair + 1) * LANES]
                zero = jnp.zeros_like(xp)
                rhs = jnp.concatenate([jnp.where(lane < head_dim, xp, zero),
                                       jnp.where(lane >= head_dim, xp, zero)], axis=0)
                y_diag.append(_dot(lhs, rhs))
            sl = slice(g * gw, (g + 1) * gw)
            ea_g = ea_e[:, sl]
            y_off = _dot(c_g, h[g].astype(BF16)) * ea_g
            h[g] = h[g] * ea_g[L - 1:L, :] + _dot(b_g.T, xw_b[:, sl])
            yg = (jnp.concatenate(y_diag, axis=1) + y_off + xs[:, sl] * dskip_ref[:, sl]) * gate[:, sl]
            y_ref[r0:r0 + L, sl] = _rms(yg, nw_ref[:, sl]).astype(y_ref.dtype)

    for g in range(n_groups):
        hstate[g] = h[g]
    xprev[...] = xs_ref[chunks * L - CONV_CARRY:chunks * L, :]
    bcprev[...] = bc_ref[chunks * L - CONV_CARRY:chunks * L, :]


def _ssd(proj, dt, cw_x, cb_x, cw_bc, cb_bc, a_log, d_skip_e, norm_w, e_mat, tri, shift,
         *, batch, seq, d_inner, col_xs, col_z, col_bc, chunks=2):
    L = SSM_CHUNK
    rows = chunks * L
    nc = seq // rows
    t = batch * seq
    bc_w = 2 * SSM_N_GROUPS * SSM_D_STATE
    row = lambda b, c: b * nc + c
    kern = functools.partial(_ssd_kernel, n_groups=SSM_N_GROUPS, d_state=SSM_D_STATE,
                             head_dim=SSM_HEAD_DIM, chunks=chunks)
    return pl.pallas_call(
        kern,
        grid=(batch, nc),
        in_specs=[
            pl.BlockSpec((rows, d_inner), lambda b, c: (row(b, c), col_xs // d_inner)),
            pl.BlockSpec((rows, d_inner), lambda b, c: (row(b, c), col_z // d_inner)),
            pl.BlockSpec((rows, bc_w), lambda b, c: (row(b, c), col_bc // bc_w)),
            pl.BlockSpec((rows, LANES), lambda b, c: (row(b, c), 0)),
            pl.BlockSpec((SSM_CONV * CONV_CARRY, d_inner), lambda b, c: (0, 0)),
            pl.BlockSpec((1, d_inner), lambda b, c: (0, 0)),
            pl.BlockSpec((SSM_CONV * CONV_CARRY, bc_w), lambda b, c: (0, 0)),
            pl.BlockSpec((1, bc_w), lambda b, c: (0, 0)),
            pl.BlockSpec((1, LANES), lambda b, c: (0, 0)),
            pl.BlockSpec((1, d_inner), lambda b, c: (0, 0)),
            pl.BlockSpec((1, d_inner), lambda b, c: (0, 0)),
            pl.BlockSpec((LANES, d_inner), lambda b, c: (0, 0)),
            pl.BlockSpec((L, 3 * L), lambda b, c: (0, 0)),
            pl.BlockSpec(shift.shape, lambda b, c: (0, 0)),
        ],
        out_specs=pl.BlockSpec((rows, d_inner), lambda b, c: (row(b, c), 0)),
        out_shape=jax.ShapeDtypeStruct((t, d_inner), BF16),
        scratch_shapes=[
            pltpu.VMEM((CONV_CARRY, d_inner), BF16),
            pltpu.VMEM((CONV_CARRY, bc_w), BF16),
            pltpu.VMEM((SSM_N_GROUPS, SSM_D_STATE, d_inner // SSM_N_GROUPS), F32),
        ],
        compiler_params=pltpu.CompilerParams(
            dimension_semantics=("parallel", "arbitrary"),
            vmem_limit_bytes=_vmem_limit(
                pipelined=(3 * _nbytes((rows, d_inner), BF16) + _nbytes((rows, bc_w), BF16)
                           + _nbytes((rows, LANES), F32) + _nbytes(cw_x.shape, BF16) + _nbytes(cw_bc.shape, BF16)
                           + _nbytes(e_mat.shape, BF16) + _nbytes(tri.shape, BF16) + _nbytes(shift.shape, BF16)),
                resident=_nbytes((SSM_D_STATE, d_inner), F32),
                live=chunks * (_nbytes((2 * L, d_inner), F32) + _nbytes((L, d_inner + bc_w), F32)
                               + 2 * _nbytes((L, d_inner), F32)))),
        name="ssd",
    )(proj, proj, proj, dt, cw_x, cb_x, cw_bc, cb_bc, a_log, d_skip_e, norm_w, e_mat, tri, shift)


def _attn_kernel(q_ref, k_ref, v_ref, lq1_ref, lk1_ref, lq2_ref, lk2_ref, subw_ref,
                 o_ref, vt_scr, *, tq, tk, lam_init):
    seq = q_ref.shape[0]
    for jb in range(seq // tk):
        vt_scr[jb] = v_ref[jb * tk:(jb + 1) * tk, :].T

    lam = (jnp.exp(jnp.sum(lq1_ref[...] * lk1_ref[...]))
           - jnp.exp(jnp.sum(lq2_ref[...] * lk2_ref[...])) + lam_init)
    subw = subw_ref[...] * (1.0 - lam_init)

    stacked_q = {}

    def scores(qi, j):
        if qi not in stacked_q:
            q = q_ref[qi * tq:(qi + 1) * tq, :]
            lane = lax.broadcasted_iota(jnp.int32, q.shape, 1)
            zero = jnp.zeros_like(q)
            stacked_q[qi] = jnp.concatenate([jnp.where(lane < ATT_HEAD_DIM, q, zero),
                                             jnp.where(lane >= ATT_HEAD_DIM, q, zero)], axis=0)
        return _dot_nt(k_ref[j * tk:(j + 1) * tk, :], stacked_q[qi])

    steps = [(qi, j) for qi in range(seq // tq) for j in range((qi + 1) * tq // tk)]
    s_next = scores(*steps[0])
    m = l = acc = None
    for t, (qi, j) in enumerate(steps):
        s = s_next
        if t + 1 < len(steps):
            s_next = scores(*steps[t + 1])
        if (j + 1) * tk > qi * tq:
            kv = lax.broadcasted_iota(jnp.int32, s.shape, 0) + (j * tk - qi * tq)
            r = lax.broadcasted_iota(jnp.int32, s.shape, 1)
            r = jnp.where(r >= tq, r - tq, r)
            s = jnp.where(kv <= r, s, -jnp.inf)
        s_max = jnp.max(s, axis=0, keepdims=True)
        if j == 0:
            m = s_max
            p = jnp.exp2(s - m)
            l = jnp.sum(p, axis=0, keepdims=True)
            acc = _dot(vt_scr[j], p.astype(BF16))
        else:
            m_new = jnp.maximum(m, s_max)
            alpha = jnp.exp2(m - m_new)
            p = jnp.exp2(s - m_new)
            l = alpha * l + jnp.sum(p, axis=0, keepdims=True)
            acc = alpha * acc + _dot(vt_scr[j], p.astype(BF16))
            m = m_new
        if (j + 1) * tk == (qi + 1) * tq:
            o = acc / l
            od = o[:, 0:tq] - lam * o[:, tq:2 * tq]
            od = od * lax.rsqrt(jnp.mean(od * od, axis=0, keepdims=True) + RMS_EPS)
            o_ref[qi * tq:(qi + 1) * tq, :] = (od * subw).T.astype(o_ref.dtype)


def _attention(proj, lq1, lk1, lq2, lk2, subw, *, batch, seq, n_heads, col_q, col_k, col_v,
               lam_init, tq=512, tk=512):
    t = batch * seq
    hd = ATT_V_DIM
    kern = functools.partial(_attn_kernel, tq=tq, tk=tk, lam_init=lam_init)
    small = pl.BlockSpec((1, ATT_HEAD_DIM), lambda b, h: (0, 0))
    return pl.pallas_call(
        kern,
        grid=(batch, n_heads),
        in_specs=[
            pl.BlockSpec((seq, hd), lambda b, h: (b, col_q // hd + h)),
            pl.BlockSpec((seq, hd), lambda b, h: (b, col_k // hd + h)),
            pl.BlockSpec((seq, hd), lambda b, h: (b, col_v // hd + h)),
            small, small, small, small,
            pl.BlockSpec((hd, 1), lambda b, h: (0, 0)),
        ],
        out_specs=pl.BlockSpec((seq, hd), lambda b, h: (b, h)),
        out_shape=jax.ShapeDtypeStruct((t, n_heads * hd), BF16),
        scratch_shapes=[pltpu.VMEM((seq // tk, hd, tk), BF16)],
        compiler_params=pltpu.CompilerParams(
            dimension_semantics=("parallel", "parallel"),
            vmem_limit_bytes=_vmem_limit(
                pipelined=4 * _nbytes((seq, hd), BF16),
                resident=_nbytes((seq, hd), BF16),
                live=(4 * _nbytes((tk, 2 * tq), F32) + _nbytes((tk, 2 * tq), BF16)
                      + 2 * _nbytes((hd, 2 * tq), F32)))),
        name="diff_attn",
    )(proj, proj, proj, lq1, lk1, lq2, lk2, subw)


def _mix_kernel(x_ref, ys_ref, ya_ref, npre_ref, npost_ref, wg_ref, wso_ref, wao_ref, wmix_ref, o_ref):
    x = x_ref[...]
    d = x.shape[1]
    h = _rms(x, npre_ref[...]).astype(BF16)
    gates = jax.nn.sigmoid(_dot_nt(h, wg_ref[...]))
    y_ssm = _dot(ys_ref[...], wso_ref[...])
    y_att = _dot(ya_ref[...], wao_ref[...])
    blend = gates[:, 0:d] * y_ssm + gates[:, d:2 * d] * y_att
    mixed = _dot(blend.astype(BF16), wmix_ref[...])
    o_ref[...] = x + _rms(mixed, npost_ref[...])


def _mix(x2d, y_ssm, y_att, n_pre, n_post, w_gate, w_so, w_ao, w_mix, *, tm=512):
    t, d = x2d.shape
    rows = lambda w: pl.BlockSpec((tm, w), lambda i: (i, 0))
    return pl.pallas_call(
        _mix_kernel,
        grid=(t // tm,),
        in_specs=[rows(d), rows(y_ssm.shape[1]), rows(y_att.shape[1]),
                  _const_spec(n_pre.shape), _const_spec(n_post.shape),
                  _const_spec(w_gate.shape), _const_spec(w_so.shape),
                  _const_spec(w_ao.shape), _const_spec(w_mix.shape)],
        out_specs=rows(d),
        out_shape=jax.ShapeDtypeStruct((t, d), F32),
        compiler_params=pltpu.CompilerParams(
            dimension_semantics=("parallel",),
            vmem_limit_bytes=_vmem_limit(
                pipelined=(2 * _nbytes((tm, d), F32) + _nbytes((tm, y_ssm.shape[1]), BF16)
                           + _nbytes((tm, y_att.shape[1]), BF16)),
                resident=sum(_nbytes(w.shape, BF16) for w in (w_gate, w_so, w_ao, w_mix)),
                live=_nbytes((tm, 2 * d), F32) + 4 * _nbytes((tm, d), F32))),
        name="mix",
    )(x2d, y_ssm, y_att, n_pre, n_post, w_gate, w_so, w_ao, w_mix)


def _ffn_kernel(x_ref, npre_ref, npost_ref, wg_ref, wu_ref, wd_ref, o_ref):
    x = x_ref[...]
    h = _rms(x, npre_ref[...]).astype(BF16)
    act = (_silu(_dot(h, wg_ref[...])) * _dot(h, wu_ref[...])).astype(BF16)
    f = _dot(act, wd_ref[...])
    o_ref[...] = x + _rms(f, npost_ref[...])


def _ffn(x2d, n_pre, n_post, w_g, w_u, w_d, *, tm=512):
    t, d = x2d.shape
    rows = pl.BlockSpec((tm, d), lambda i: (i, 0))
    return pl.pallas_call(
        _ffn_kernel,
        grid=(t // tm,),
        in_specs=[rows, _const_spec(n_pre.shape), _const_spec(n_post.shape),
                  _const_spec(w_g.shape), _const_spec(w_u.shape), _const_spec(w_d.shape)],
        out_specs=rows,
        out_shape=jax.ShapeDtypeStruct((t, d), F32),
        compiler_params=pltpu.CompilerParams(
            dimension_semantics=("parallel",),
            vmem_limit_bytes=_vmem_limit(
                pipelined=2 * _nbytes((tm, d), F32),
                resident=sum(_nbytes(w.shape, BF16) for w in (w_g, w_u, w_d)),
                live=2 * _nbytes((tm, w_g.shape[1]), F32) + _nbytes((tm, w_g.shape[1]), BF16))),
        name="ffn",
    )(x2d, n_pre, n_post, w_g, w_u, w_d)


def kernel(x, positions, w_in, conv_w, conv_b, dt_bias, a_log, d_skip, ssm_norm_w, w_ssm_out, lam_q1, lam_k1, lam_q2, lam_k2, attn_subln_w, w_attn_out, w_mix_out, norm_pre_mix, norm_post_mix, norm_pre_ffn, norm_post_ffn, w_ffn_gate, w_ffn_up, w_ffn_down):
    batch, seq, d_model = x.shape
    depth = w_in.shape[0]
    d_inner = w_ssm_out.shape[1]
    n_ssm_heads = dt_bias.shape[1]
    bc_w = 2 * SSM_N_GROUPS * SSM_D_STATE
    att_w = w_attn_out.shape[1]
    n_att_heads = att_w // ATT_V_DIM
    t = batch * seq

    z_end = d_inner
    xbc_end = z_end + d_inner + bc_w
    dt_end = xbc_end + n_ssm_heads
    q_end = dt_end + att_w
    k_end = q_end + att_w
    v_end = k_end + att_w

    col_z, col_xs = 0, d_inner
    col_bc = 2 * d_inner
    col_q = col_bc + bc_w
    col_k = col_q + att_w
    col_v = col_k + att_w

    inv_freq = ROPE_THETA ** (-jnp.arange(0, ROPE_DIM, 2, dtype=F32) / ROPE_DIM)
    head_of_col = np.arange(d_inner) // SSM_HEAD_DIM
    assert n_ssm_heads <= LANES
    e_mat = jnp.asarray(np.arange(LANES)[:, None] == head_of_col[None, :], BF16)
    tri = np.arange(SSM_CHUNK)[None, :] <= np.arange(SSM_CHUNK)[:, None]
    tri = jnp.asarray(np.concatenate([tri, tri, tri], axis=1), BF16)
    L = SSM_CHUNK
    t_out = np.arange(L)[:, None, None]
    tap = np.arange(SSM_CONV)[None, :, None]
    src = t_out - (SSM_CONV - 1) + tap
    src_col = tap * L + np.where(src >= 0, src, L + src)
    shift = jnp.asarray((np.arange(SSM_CONV * L)[None, None, :] == src_col).any(axis=1), BF16)

    def pad_heads(v):
        return jnp.pad(v.astype(F32), (0, LANES - n_ssm_heads))[None, :]

    x2d = x.reshape(t, d_model)
    for l in range(depth):
        lam_init = 0.8 - 0.6 * math.exp(-0.3 * l)
        wl = w_in[l]
        w_t = wl.T.astype(BF16)
        later = [w_ssm_out[l], w_attn_out[l], w_mix_out[l], w_ffn_gate[l], w_ffn_up[l], w_ffn_down[l]]
        proj, dt, w_gate_t, w_so, w_ao, w_mix, w_fg, w_fu, w_fd = _in_proj(
            x2d, positions, norm_pre_mix[l][None, :], w_t, pad_heads(dt_bias[l]), inv_freq, later,
            n_out=col_v + att_w, row_dt=xbc_end, n_dt=n_ssm_heads, row_q=dt_end, row_tail=v_end,
            col_q=col_q, col_k=col_k, col_v=col_v)

        cw = jnp.repeat(conv_w[l].astype(BF16), CONV_CARRY, axis=0)
        y_ssm = _ssd(proj, dt,
                     cw[:, :d_inner], conv_b[l][None, :d_inner],
                     cw[:, d_inner:], conv_b[l][None, d_inner:],
                     pad_heads(a_log[l]), jnp.repeat(d_skip[l], SSM_HEAD_DIM)[None, :],
                     ssm_norm_w[l][None, :], e_mat, tri, shift,
                     batch=batch, seq=seq, d_inner=d_inner, col_xs=col_xs, col_z=col_z, col_bc=col_bc)

        y_att = _attention(proj, lam_q1[l][None, :], lam_k1[l][None, :], lam_q2[l][None, :],
                           lam_k2[l][None, :], attn_subln_w[l][:, None],
                           batch=batch, seq=seq, n_heads=n_att_heads,
                           col_q=col_q, col_k=col_k, col_v=col_v, lam_init=lam_init)

        x2d = _mix(x2d, y_ssm, y_att, norm_pre_mix[l][None, :], norm_post_mix[l][None, :],
                   w_gate_t, w_so, w_ao, w_mix)
        x2d = _ffn(x2d, norm_pre_ffn[l][None, :], norm_post_ffn[l][None, :], w_fg, w_fu, w_fd)
    return x2d.reshape(batch, seq, d_model)
```

```python
import functools
import math

import jax
import jax.numpy as jnp
import numpy as np
from jax import lax
from jax.experimental import pallas as pl
from jax.experimental.pallas import tpu as pltpu

F32 = jnp.float32
BF16 = jnp.bfloat16

LANES = 128

SSM_HEAD_DIM = 64
SSM_N_GROUPS = 4
SSM_D_STATE = 128
SSM_CONV = 4
SSM_CHUNK = 128
CONV_CARRY = 16
ATT_HEAD_DIM = 64
ATT_V_DIM = 2 * ATT_HEAD_DIM
ROPE_THETA = 500000.0
ROPE_DIM = ATT_HEAD_DIM // 4
RMS_EPS = 1e-6
Q_SCALE = ATT_HEAD_DIM ** -0.5 * math.log2(math.e)

MIB = 1024 * 1024
V7X_VMEM_BYTES = 64 * MIB
VMEM_RESERVE = 8 * MIB


def _nbytes(shape, dtype):
    return math.prod(shape) * jnp.dtype(dtype).itemsize


def _vmem_limit(pipelined, resident, live):
    budget = V7X_VMEM_BYTES - VMEM_RESERVE
    need = 2 * pipelined + resident + live
    assert need <= budget, need
    return budget


def _rms(x, w):
    return x * lax.rsqrt(jnp.mean(x * x, axis=-1, keepdims=True) + RMS_EPS) * w


def _silu(x):
    hx = 0.5 * x
    return hx + hx * jnp.tanh(hx)


def _dot(a, b):
    return jnp.dot(a, b, preferred_element_type=F32)


def _dot_nt(a, b):
    return lax.dot_general(a, b, (((1,), (1,)), ((), ())), preferred_element_type=F32)


def _split3(x):
    hi = x.astype(BF16)
    r = x - hi.astype(F32)
    mid = r.astype(BF16)
    lo = (r - mid.astype(F32)).astype(BF16)
    return hi, mid, lo


def _const_spec(shape):
    nd = len(shape)
    return pl.BlockSpec(shape, lambda *_: (0,) * nd, pipeline_mode=pl.Buffered(1))


def _inproj_kernel(x_ref, pos_ref, nw_ref, wt_ref, dtb_ref, invf_ref, rope_e_ref, rope_c_ref,
                   *refs, n_cast, n_tail, row_dt, n_dt, row_q, col_q, col_k, col_v, sub):
    cast_in = refs[:n_cast]
    tail_in = refs[n_cast:n_cast + n_tail]
    out_ref, dt_ref, tail_ref = refs[n_cast + n_tail:n_cast + n_tail + 3]
    cast_out = refs[n_cast + n_tail + 3:2 * n_cast + n_tail + 3]
    h_scr, cos_scr, s1_scr, s2_scr = refs[2 * n_cast + n_tail + 3:]
    for src, dst in zip(cast_in, cast_out):
        dst[...] = src[...].astype(dst.dtype)
    for p, src in enumerate(tail_in):
        tail_ref[p * src.shape[0]:(p + 1) * src.shape[0], :] = src[...]

    hb = _rms(x_ref[...], nw_ref[...]).astype(BF16)
    h_scr[...] = hb
    dt_lane = lax.broadcasted_iota(jnp.int32, dt_ref.shape, 1)
    dt = jax.nn.softplus(_dot_nt(hb, wt_ref[row_dt:row_dt + LANES, :]) + dtb_ref[...])
    dt_ref[...] = jnp.where(dt_lane < n_dt, dt, 0.0)
    half = ROPE_DIM // 2
    ang = invf_ref[...] * pos_ref[...].astype(F32)
    terms = [t.astype(F32) for t in _split3(jnp.cos(ang)) + _split3(jnp.sin(ang))]
    fill = jnp.zeros((LANES - len(terms) * half, ang.shape[1]), F32)
    grid_t = jnp.concatenate(terms + [fill], axis=0).T.astype(BF16)
    tables = _dot(grid_t, rope_e_ref[...])
    cos_scr[...] = tables[:, 0:LANES] + rope_c_ref[...]
    s1_scr[...] = tables[:, LANES:2 * LANES]
    s2_scr[...] = tables[:, 2 * LANES:3 * LANES]

    for c0 in range(0, out_ref.shape[1], sub):
        r0 = c0 if c0 < col_q else c0 - col_q + row_q
        acc = _dot_nt(h_scr[...], wt_ref[r0:r0 + sub, :])
        if col_q <= c0 < col_v:
            scale = Q_SCALE if c0 < col_k else 1.0
            for c in range(c0, c0 + sub, LANES):
                a = acc[:, c - c0:c - c0 + LANES]
                r = (a * cos_scr[...] + pltpu.roll(a, half, 1) * s1_scr[...]
                     + pltpu.roll(a, LANES - half, 1) * s2_scr[...])
                out_ref[:, c:c + LANES] = (r * scale).astype(out_ref.dtype)
        else:
            out_ref[:, c0:c0 + sub] = acc.astype(out_ref.dtype)


def _in_proj(x2d, positions, norm_w, w_t, dt_bias, inv_freq, casts, *, n_out, row_dt, n_dt, row_q,
             row_tail, col_q, col_k, col_v, tm=512, sub=512):
    t, d = x2d.shape
    steps = t // tm
    assert all(c % sub == 0 for c in (col_q, col_k, col_v, n_out))
    tail_rows = (w_t.shape[0] - row_tail) // steps
    tail_blk = math.gcd(row_tail, tail_rows)
    n_tail = tail_rows // tail_blk
    assert tail_rows * steps == w_t.shape[0] - row_tail and tail_blk % 16 == 0
    tail_specs = [pl.BlockSpec((tail_blk, d), lambda i, p=p: (row_tail // tail_blk + n_tail * i + p, 0))
                  for p in range(n_tail)]

    def cast_spec(w):
        hold = next(h for h in (1, 2, 4, 8) if (w.shape[0] * h) % (16 * steps) == 0)
        return pl.BlockSpec((w.shape[0] * hold // steps, w.shape[1]), lambda i: (i // hold, 0))

    cast_specs = [cast_spec(w) for w in casts]
    half = ROPE_DIM // 2
    lane = np.arange(LANES) % ATT_HEAD_DIM
    freq_row = np.arange(LANES) % half
    is_cos = np.arange(LANES) < 3 * half
    is_sin = (np.arange(LANES) >= 3 * half) & (np.arange(LANES) < 6 * half)
    same_f = freq_row[:, None] == (lane % half)[None, :]
    rope_e = np.concatenate([
        (is_cos[:, None] & same_f & (lane < ROPE_DIM)[None, :]) * 1.0,
        (is_sin[:, None] & same_f & ((lane >= half) & (lane < ROPE_DIM))[None, :]) * 1.0,
        (is_sin[:, None] & same_f & (lane < half)[None, :]) * -1.0], axis=1)
    rope_c = (lane >= ROPE_DIM)[None, :] * 1.0
    kern = functools.partial(_inproj_kernel, n_cast=len(casts), n_tail=n_tail, row_dt=row_dt, n_dt=n_dt,
                             row_q=row_q, col_q=col_q, col_k=col_k, col_v=col_v, sub=sub)
    return pl.pallas_call(
        kern,
        grid=(t // tm,),
        in_specs=[
            pl.BlockSpec((tm, d), lambda i: (i, 0)),
            pl.BlockSpec((None, 1, tm), lambda i: (i, 0, 0)),
            _const_spec((1, d)),
            _const_spec(w_t.shape),
            _const_spec((1, LANES)),
            _const_spec((half, 1)),
            _const_spec(rope_e.shape),
            _const_spec(rope_c.shape),
        ] + cast_specs + tail_specs,
        out_specs=[
            pl.BlockSpec((tm, n_out), lambda i: (i, 0)),
            pl.BlockSpec((tm, LANES), lambda i: (i, 0)),
            pl.BlockSpec((tail_rows, d), lambda i: (i, 0)),
        ] + cast_specs,
        out_shape=[
            jax.ShapeDtypeStruct((t, n_out), BF16),
            jax.ShapeDtypeStruct((t, LANES), F32),
            jax.ShapeDtypeStruct((w_t.shape[0] - row_tail, d), BF16),
        ] + [jax.ShapeDtypeStruct(w.shape, BF16) for w in casts],
        scratch_shapes=[
            pltpu.VMEM((tm, d), BF16),
            pltpu.VMEM((tm, LANES), F32),
            pltpu.VMEM((tm, LANES), F32),
            pltpu.VMEM((tm, LANES), F32),
        ],
        compiler_params=pltpu.CompilerParams(
            dimension_semantics=("arbitrary",),
            vmem_limit_bytes=_vmem_limit(
                pipelined=(_nbytes((tm, d), F32) + _nbytes((tm, n_out), BF16) + _nbytes((tm, LANES), F32)
                           + _nbytes((tail_rows, d), BF16)
                           + sum(_nbytes(s.block_shape, F32) + _nbytes(s.block_shape, BF16) for s in cast_specs)),
                resident=_nbytes(w_t.shape, BF16) + _nbytes((tm, d), BF16) + 3 * _nbytes((tm, LANES), F32),
                live=3 * _nbytes((tm, sub), F32))),
        name="in_proj",
    )(x2d, positions.reshape(t // tm, 1, tm), norm_w, w_t, dt_bias, inv_freq[:, None],
      jnp.asarray(rope_e, BF16), jnp.asarray(rope_c, F32), *casts, *([w_t] * n_tail))


def _ssd_kernel(xs_ref, z_ref, bc_ref, dt_ref, cwx_ref, cbx_ref, cwbc_ref, cbbc_ref,
                alog_ref, dskip_ref, nw_ref, e_ref, tri_ref, shift_ref,
                y_ref, xprev, bcprev, hstate, *, n_groups, d_state, head_dim, chunks):
    L = SSM_CHUNK
    d_inner = xs_ref.shape[1]
    gw = d_inner // n_groups
    heads_per_group = gw // head_dim
    pairs_per_group = heads_per_group // 2

    @pl.when(pl.program_id(1) == 0)
    def _():
        xprev[...] = jnp.zeros(xprev.shape, BF16)
        bcprev[...] = jnp.zeros(bcprev.shape, BF16)
        hstate[...] = jnp.zeros(hstate.shape, F32)

    a_neg = -jnp.exp(alog_ref[...]) * math.log2(math.e)
    row = lax.broadcasted_iota(jnp.int32, (L, L), 0)
    col = lax.broadcasted_iota(jnp.int32, (L, L), 1)
    causal = col <= row
    lane = lax.broadcasted_iota(jnp.int32, (L, LANES), 1)

    def conv_silu(cur_ref, prev, w_ref, b_ref, r0):
        body = cur_ref[r0:r0 + L - CONV_CARRY, :]
        last = cur_ref[r0 + L - CONV_CARRY:r0 + L, :]
        old = prev[...] if r0 == 0 else cur_ref[r0 - CONV_CARRY:r0, :]
        row = lax.broadcasted_iota(jnp.int32, last.shape, 0)
        taps = []
        for k in range(SSM_CONV):
            back = SSM_CONV - 1 - k
            wk = w_ref[k * CONV_CARRY:(k + 1) * CONV_CARRY, :]
            taps.append(body * jnp.concatenate([wk] * (L // CONV_CARRY - 1), axis=0))
            taps.append(jnp.where(row >= CONV_CARRY - back, old, last) * wk)
        acc = b_ref[...] + _dot(shift_ref[...], jnp.concatenate(taps, axis=0))
        return _silu(acc)

    h = [hstate[g] for g in range(n_groups)]
    for ci in range(chunks):
        r0 = ci * L
        xs = conv_silu(xs_ref, xprev, cwx_ref, cbx_ref, r0)
        bc = conv_silu(bc_ref, bcprev, cwbc_ref, cbbc_ref, r0).astype(BF16)

        dt = dt_ref[r0:r0 + L, :]
        a_cs = _dot(tri_ref[...], jnp.concatenate(_split3(dt * a_neg), axis=0))
        a_cs_t = a_cs.T
        ea = jnp.exp2(a_cs)
        dte = jnp.exp2(a_cs[L - 1:L, :] - a_cs)
        stacked = jnp.concatenate([ea, dt * dte], axis=0).astype(BF16)
        expanded = _dot(stacked, e_ref[...])
        ea_e = expanded[0:L]
        xs_b = xs.astype(BF16)
        xw_b = (xs * expanded[L:2 * L]).astype(BF16)
        dt_t = dt.T
        gate = _silu(z_ref[r0:r0 + L, :].astype(F32))

        for g in range(n_groups):
            b_g = bc[:, g * d_state:(g + 1) * d_state]
            c_g = bc[:, (n_groups + g) * d_state:(n_groups + g + 1) * d_state]
            cb = _dot_nt(c_g, b_g)
            y_diag = []
            for p2 in range(pairs_per_group):
                pair = g * pairs_per_group + p2
                ms = []
                for hh in (2 * pair, 2 * pair + 1):
                    seg = a_cs[:, hh:hh + 1] - a_cs_t[hh:hh + 1, :]
                    dec = jnp.exp2(jnp.where(causal, seg, -jnp.inf))
                    ms.append((cb * dec * dt_t[hh:hh + 1, :]).astype(BF16))
                lhs = jnp.concatenate(ms, axis=1)
                xp = xs_b[:, pair * LANES:(pair + 1) * LANES]
                zero = jnp.zeros_like(xp)
                rhs = jnp.concatenate([jnp.where(lane < head_dim, xp, zero),
                                       jnp.where(lane >= head_dim, xp, zero)], axis=0)
                y_diag.append(_dot(lhs, rhs))
            sl = slice(g * gw, (g + 1) * gw)
            ea_g = ea_e[:, sl]
            y_off = _dot(c_g, h[g].astype(BF16)) * ea_g
            h[g] = h[g] * ea_g[L - 1:L, :] + _dot(b_g.T, xw_b[:, sl])
            yg = (jnp.concatenate(y_diag, axis=1) + y_off + xs[:, sl] * dskip_ref[:, sl]) * gate[:, sl]
            y_ref[r0:r0 + L, sl] = _rms(yg, nw_ref[:, sl]).astype(y_ref.dtype)

    for g in range(n_groups):
        hstate[g] = h[g]
    xprev[...] = xs_ref[chunks * L - CONV_CARRY:chunks * L, :]
    bcprev[...] = bc_ref[chunks * L - CONV_CARRY:chunks * L, :]


def _ssd(proj, dt, cw_x, cb_x, cw_bc, cb_bc, a_log, d_skip_e, norm_w, e_mat, tri, shift,
         *, batch, seq, d_inner, col_xs, col_z, col_bc, chunks=2):
    L = SSM_CHUNK
    rows = chunks * L
    nc = seq // rows
    t = batch * seq
    bc_w = 2 * SSM_N_GROUPS * SSM_D_STATE
    row = lambda b, c: b * nc + c
    kern = functools.partial(_ssd_kernel, n_groups=SSM_N_GROUPS, d_state=SSM_D_STATE,
                             head_dim=SSM_HEAD_DIM, chunks=chunks)
    return pl.pallas_call(
        kern,
        grid=(batch, nc),
        in_specs=[
            pl.BlockSpec((rows, d_inner), lambda b, c: (row(b, c), col_xs // d_inner)),
            pl.BlockSpec((rows, d_inner), lambda b, c: (row(b, c), col_z // d_inner)),
            pl.BlockSpec((rows, bc_w), lambda b, c: (row(b, c), col_bc // bc_w)),
            pl.BlockSpec((rows, LANES), lambda b, c: (row(b, c), 0)),
            pl.BlockSpec((SSM_CONV * CONV_CARRY, d_inner), lambda b, c: (0, 0)),
            pl.BlockSpec((1, d_inner), lambda b, c: (0, 0)),
            pl.BlockSpec((SSM_CONV * CONV_CARRY, bc_w), lambda b, c: (0, 0)),
            pl.BlockSpec((1, bc_w), lambda b, c: (0, 0)),
            pl.BlockSpec((1, LANES), lambda b, c: (0, 0)),
            pl.BlockSpec((1, d_inner), lambda b, c: (0, 0)),
            pl.BlockSpec((1, d_inner), lambda b, c: (0, 0)),
            pl.BlockSpec((LANES, d_inner), lambda b, c: (0, 0)),
            pl.BlockSpec((L, 3 * L), lambda b, c: (0, 0)),
            pl.BlockSpec(shift.shape, lambda b, c: (0, 0)),
        ],
        out_specs=pl.BlockSpec((rows, d_inner), lambda b, c: (row(b, c), 0)),
        out_shape=jax.ShapeDtypeStruct((t, d_inner), BF16),
        scratch_shapes=[
            pltpu.VMEM((CONV_CARRY, d_inner), BF16),
            pltpu.VMEM((CONV_CARRY, bc_w), BF16),
            pltpu.VMEM((SSM_N_GROUPS, SSM_D_STATE, d_inner // SSM_N_GROUPS), F32),
        ],
        compiler_params=pltpu.CompilerParams(
            dimension_semantics=("parallel", "arbitrary"),
            vmem_limit_bytes=_vmem_limit(
                pipelined=(3 * _nbytes((rows, d_inner), BF16) + _nbytes((rows, bc_w), BF16)
                           + _nbytes((rows, LANES), F32) + _nbytes(cw_x.shape, BF16) + _nbytes(cw_bc.shape, BF16)
                           + _nbytes(e_mat.shape, BF16) + _nbytes(tri.shape, BF16) + _nbytes(shift.shape, BF16)),
                resident=_nbytes((SSM_D_STATE, d_inner), F32),
                live=chunks * (_nbytes((2 * L, d_inner), F32) + _nbytes((L, d_inner + bc_w), F32)
                               + 2 * _nbytes((L, d_inner), F32)))),
        name="ssd",
    )(proj, proj, proj, dt, cw_x, cb_x, cw_bc, cb_bc, a_log, d_skip_e, norm_w, e_mat, tri, shift)


def _attn_kernel(q_ref, k_ref, v_ref, lq1_ref, lk1_ref, lq2_ref, lk2_ref, subw_ref,
                 o_ref, vt_scr, *, tq, tk, lam_init):
    seq = q_ref.shape[0]
    for jb in range(seq // tk):
        vt_scr[jb] = v_ref[jb * tk:(jb + 1) * tk, :].T

    lam = (jnp.exp(jnp.sum(lq1_ref[...] * lk1_ref[...]))
           - jnp.exp(jnp.sum(lq2_ref[...] * lk2_ref[...])) + lam_init)
    subw = subw_ref[...] * (1.0 - lam_init)

    stacked_q = {}

    def scores(qi, j):
        if qi not in stacked_q:
            q = q_ref[qi * tq:(qi + 1) * tq, :]
            lane = lax.broadcasted_iota(jnp.int32, q.shape, 1)
            zero = jnp.zeros_like(q)
            stacked_q[qi] = jnp.concatenate([jnp.where(lane < ATT_HEAD_DIM, q, zero),
                                             jnp.where(lane >= ATT_HEAD_DIM, q, zero)], axis=0)
        return _dot_nt(k_ref[j * tk:(j + 1) * tk, :], stacked_q[qi])

    steps = [(qi, j) for qi in range(seq // tq) for j in range((qi + 1) * tq // tk)]
    s_next = scores(*steps[0])
    m = l = acc = None
    for t, (qi, j) in enumerate(steps):
        s = s_next
        if t + 1 < len(steps):
            s_next = scores(*steps[t + 1])
        if (j + 1) * tk > qi * tq:
            kv = lax.broadcasted_iota(jnp.int32, s.shape, 0) + (j * tk - qi * tq)
            r = lax.broadcasted_iota(jnp.int32, s.shape, 1)
            r = jnp.where(r >= tq, r - tq, r)
            s = jnp.where(kv <= r, s, -jnp.inf)
        s_max = jnp.max(s, axis=0, keepdims=True)
        if j == 0:
            m = s_max
            p = jnp.exp2(s - m)
            l = jnp.sum(p, axis=0, keepdims=True)
            acc = _dot(vt_scr[j], p.astype(BF16))
        else:
            m_new = jnp.maximum(m, s_max)
            alpha = jnp.exp2(m - m_new)
            p = jnp.exp2(s - m_new)
            l = alpha * l + jnp.sum(p, axis=0, keepdims=True)
            acc = alpha * acc + _dot(vt_scr[j], p.astype(BF16))
            m = m_new
        if (j + 1) * tk == (qi + 1) * tq:
            o = acc / l
            od = o[:, 0:tq] - lam * o[:, tq:2 * tq]
            od = od * lax.rsqrt(jnp.mean(od * od, axis=0, keepdims=True) + RMS_EPS)
            o_ref[qi * tq:(qi + 1) * tq, :] = (od * subw).T.astype(o_ref.dtype)


def _attention(proj, lq1, lk1, lq2, lk2, subw, *, batch, seq, n_heads, col_q, col_k, col_v,
               lam_init, tq=512, tk=512):
    t = batch * seq
    hd = ATT_V_DIM
    kern = functools.partial(_attn_kernel, tq=tq, tk=tk, lam_init=lam_init)
    small = pl.BlockSpec((1, ATT_HEAD_DIM), lambda b, h: (0, 0))
    return pl.pallas_call(
        kern,
        grid=(batch, n_heads),
        in_specs=[
            pl.BlockSpec((seq, hd), lambda b, h: (b, col_q // hd + h)),
            pl.BlockSpec((seq, hd), lambda b, h: (b, col_k // hd + h)),
            pl.BlockSpec((seq, hd), lambda b, h: (b, col_v // hd + h)),
            small, small, small, small,
            pl.BlockSpec((hd, 1), lambda b, h: (0, 0)),
        ],
        out_specs=pl.BlockSpec((seq, hd), lambda b, h: (b, h)),
        out_shape=jax.ShapeDtypeStruct((t, n_heads * hd), BF16),
        scratch_shapes=[pltpu.VMEM((seq // tk, hd, tk), BF16)],
        compiler_params=pltpu.CompilerParams(
            dimension_semantics=("parallel", "parallel"),
            vmem_limit_bytes=_vmem_limit(
                pipelined=4 * _nbytes((seq, hd), BF16),
                resident=_nbytes((seq, hd), BF16),
                live=(4 * _nbytes((tk, 2 * tq), F32) + _nbytes((tk, 2 * tq), BF16)
                      + 2 * _nbytes((hd, 2 * tq), F32)))),
        name="diff_attn",
    )(proj, proj, proj, lq1, lk1, lq2, lk2, subw)


def _mix_kernel(x_ref, ys_ref, ya_ref, npre_ref, npost_ref, wg_ref, wso_ref, wao_ref, wmix_ref, o_ref):
    x = x_ref[...]
    d = x.shape[1]
    h = _rms(x, npre_ref[...]).astype(BF16)
    gates = jax.nn.sigmoid(_dot_nt(h, wg_ref[...]))
    y_ssm = _dot(ys_ref[...], wso_ref[...])
    y_att = _dot(ya_ref[...], wao_ref[...])
    blend = gates[:, 0:d] * y_ssm + gates[:, d:2 * d] * y_att
    mixed = _dot(blend.astype(BF16), wmix_ref[...])
    o_ref[...] = x + _rms(mixed, npost_ref[...])


def _mix(x2d, y_ssm, y_att, n_pre, n_post, w_gate, w_so, w_ao, w_mix, *, tm=512):
    t, d = x2d.shape
    rows = lambda w: pl.BlockSpec((tm, w), lambda i: (i, 0))
    return pl.pallas_call(
        _mix_kernel,
        grid=(t // tm,),
        in_specs=[rows(d), rows(y_ssm.shape[1]), rows(y_att.shape[1]),
                  _const_spec(n_pre.shape), _const_spec(n_post.shape),
                  _const_spec(w_gate.shape), _const_spec(w_so.shape),
                  _const_spec(w_ao.shape), _const_spec(w_mix.shape)],
        out_specs=rows(d),
        out_shape=jax.ShapeDtypeStruct((t, d), F32),
        compiler_params=pltpu.CompilerParams(
            dimension_semantics=("parallel",),
            vmem_limit_bytes=_vmem_limit(
                pipelined=(2 * _nbytes((tm, d), F32) + _nbytes((tm, y_ssm.shape[1]), BF16)
                           + _nbytes((tm, y_att.shape[1]), BF16)),
                resident=sum(_nbytes(w.shape, BF16) for w in (w_gate, w_so, w_ao, w_mix)),
                live=_nbytes((tm, 2 * d), F32) + 4 * _nbytes((tm, d), F32))),
        name="mix",
    )(x2d, y_ssm, y_att, n_pre, n_post, w_gate, w_so, w_ao, w_mix)


def _ffn_kernel(x_ref, npre_ref, npost_ref, wg_ref, wu_ref, wd_ref, o_ref):
    x = x_ref[...]
    h = _rms(x, npre_ref[...]).astype(BF16)
    act = (_silu(_dot(h, wg_ref[...])) * _dot(h, wu_ref[...])).astype(BF16)
    f = _dot(act, wd_ref[...])
    o_ref[...] = x + _rms(f, npost_ref[...])


def _ffn(x2d, n_pre, n_post, w_g, w_u, w_d, *, tm=512):
    t, d = x2d.shape
    rows = pl.BlockSpec((tm, d), lambda i: (i, 0))
    return pl.pallas_call(
        _ffn_kernel,
        grid=(t // tm,),
        in_specs=[rows, _const_spec(n_pre.shape), _const_spec(n_post.shape),
                  _const_spec(w_g.shape), _const_spec(w_u.shape), _const_spec(w_d.shape)],
        out_specs=rows,
        out_shape=jax.ShapeDtypeStruct((t, d), F32),
        compiler_params=pltpu.CompilerParams(
            dimension_semantics=("parallel",),
            vmem_limit_bytes=_vmem_limit(
                pipelined=2 * _nbytes((tm, d), F32),
                resident=sum(_nbytes(w.shape, BF16) for w in (w_g, w_u, w_d)),
                live=2 * _nbytes((tm, w_g.shape[1]), F32) + _nbytes((tm, w_g.shape[1]), BF16))),
        name="ffn",
    )(x2d, n_pre, n_post, w_g, w_u, w_d)


def kernel(x, positions, w_in, conv_w, conv_b, dt_bias, a_log, d_skip, ssm_norm_w, w_ssm_out, lam_q1, lam_k1, lam_q2, lam_k2, attn_subln_w, w_attn_out, w_mix_out, norm_pre_mix, norm_post_mix, norm_pre_ffn, norm_post_ffn, w_ffn_gate, w_ffn_up, w_ffn_down):
    batch, seq, d_model = x.shape
    depth = w_in.shape[0]
    d_inner = w_ssm_out.shape[1]
    n_ssm_heads = dt_bias.shape[1]
    bc_w = 2 * SSM_N_GROUPS * SSM_D_STATE
    att_w = w_attn_out.shape[1]
    n_att_heads = att_w // ATT_V_DIM
    t = batch * seq

    z_end = d_inner
    xbc_end = z_end + d_inner + bc_w
    dt_end = xbc_end + n_ssm_heads
    q_end = dt_end + att_w
    k_end = q_end + att_w
    v_end = k_end + att_w

    col_z, col_xs = 0, d_inner
    col_bc = 2 * d_inner
    col_q = col_bc + bc_w
    col_k = col_q + att_w
    col_v = col_k + att_w

    inv_freq = ROPE_THETA ** (-jnp.arange(0, ROPE_DIM, 2, dtype=F32) / ROPE_DIM)
    head_of_col = np.arange(d_inner) // SSM_HEAD_DIM
    assert n_ssm_heads <= LANES
    e_mat = jnp.asarray(np.arange(LANES)[:, None] == head_of_col[None, :], BF16)
    tri = np.arange(SSM_CHUNK)[None, :] <= np.arange(SSM_CHUNK)[:, None]
    tri = jnp.asarray(np.concatenate([tri, tri, tri], axis=1), BF16)
    L = SSM_CHUNK
    t_out = np.arange(L)[:, None, None]
    tap = np.arange(SSM_CONV)[None, :, None]
    src = t_out - (SSM_CONV - 1) + tap
    src_col = tap * L + np.where(src >= 0, src, L + src)
    shift = jnp.asarray((np.arange(SSM_CONV * L)[None, None, :] == src_col).any(axis=1), BF16)

    def pad_heads(v):
        return jnp.pad(v.astype(F32), (0, LANES - n_ssm_heads))[None, :]

    x2d = x.reshape(t, d_model)
    for l in range(depth):
        lam_init = 0.8 - 0.6 * math.exp(-0.3 * l)
        wl = w_in[l]
        w_t = wl.T.astype(BF16)
        later = [w_ssm_out[l], w_attn_out[l], w_mix_out[l], w_ffn_gate[l], w_ffn_up[l], w_ffn_down[l]]
        proj, dt, w_gate_t, w_so, w_ao, w_mix, w_fg, w_fu, w_fd = _in_proj(
            x2d, positions, norm_pre_mix[l][None, :], w_t, pad_heads(dt_bias[l]), inv_freq, later,
            n_out=col_v + att_w, row_dt=xbc_end, n_dt=n_ssm_heads, row_q=dt_end, row_tail=v_end,
            col_q=col_q, col_k=col_k, col_v=col_v)

        cw = jnp.repeat(conv_w[l].astype(BF16), CONV_CARRY, axis=0)
        y_ssm = _ssd(proj, dt,
                     cw[:, :d_inner], conv_b[l][None, :d_inner],
                     cw[:, d_inner:], conv_b[l][None, d_inner:],
                     pad_heads(a_log[l]), jnp.repeat(d_skip[l], SSM_HEAD_DIM)[None, :],
                     ssm_norm_w[l][None, :], e_mat, tri, shift,
                     batch=batch, seq=seq, d_inner=d_inner, col_xs=col_xs, col_z=col_z, col_bc=col_bc)

        y_att = _attention(proj, lam_q1[l][None, :], lam_k1[l][None, :], lam_q2[l][None, :],
                           lam_k2[l][None, :], attn_subln_w[l][:, None],
                           batch=batch, seq=seq, n_heads=n_att_heads,
                           col_q=col_q, col_k=col_k, col_v=col_v, lam_init=lam_init)

        x2d = _mix(x2d, y_ssm, y_att, norm_pre_mix[l][None, :], norm_post_mix[l][None, :],
                   w_gate_t, w_so, w_ao, w_mix)
        x2d = _ffn(x2d, norm_pre_ffn[l][None, :], norm_post_ffn[l][None, :], w_fg, w_fu, w_fd)
    return x2d.reshape(batch, seq, d_model)
```

```python
import functools
import math

import jax
import jax.numpy as jnp
import numpy as np
from jax import lax
from jax.experimental import pallas as pl
from jax.experimental.pallas import tpu as pltpu

F32 = jnp.float32
BF16 = jnp.bfloat16

LANES = 128
MXU_WIDTH = 256

SSM_HEAD_DIM = 64
SSM_N_GROUPS = 4
SSM_D_STATE = 128
SSM_CONV = 4
SSM_CHUNK = 128
CONV_CARRY = 16
ATT_HEAD_DIM = 64
ATT_V_DIM = 2 * ATT_HEAD_DIM
ROPE_THETA = 500000.0
ROPE_DIM = ATT_HEAD_DIM // 4
RMS_EPS = 1e-6
Q_SCALE = ATT_HEAD_DIM ** -0.5 * math.log2(math.e)

MIB = 1024 * 1024
V7X_VMEM_BYTES = 64 * MIB
VMEM_RESERVE = 8 * MIB


def _nbytes(shape, dtype):
    return math.prod(shape) * jnp.dtype(dtype).itemsize


def _vmem_limit(pipelined, resident, live):
    budget = V7X_VMEM_BYTES - VMEM_RESERVE
    need = 2 * pipelined + resident + live
    assert need <= budget, need
    return budget


def _rms(x, w):
    return x * lax.rsqrt(jnp.mean(x * x, axis=-1, keepdims=True) + RMS_EPS) * w


def _silu(x):
    hx = 0.5 * x
    return hx + hx * jnp.tanh(hx)


def _dot(a, b):
    return jnp.dot(a, b, preferred_element_type=F32)


def _dot_nt(a, b):
    return lax.dot_general(a, b, (((1,), (1,)), ((), ())), preferred_element_type=F32)


def _split3(x):
    hi = x.astype(BF16)
    r = x - hi.astype(F32)
    mid = r.astype(BF16)
    lo = (r - mid.astype(F32)).astype(BF16)
    return hi, mid, lo


def _const_spec(shape):
    nd = len(shape)
    return pl.BlockSpec(shape, lambda *_: (0,) * nd, pipeline_mode=pl.Buffered(1))


def _inproj_kernel(x_ref, pos_ref, nw_ref, wt_ref, dtb_ref, invf_ref, rope_e_ref, rope_c_ref,
                   *refs, n_cast, n_tail, row_dt, n_dt, row_q, col_q, col_k, col_v, sub):
    cast_in = refs[:n_cast]
    tail_in = refs[n_cast:n_cast + n_tail]
    out_ref, dt_ref, tail_ref = refs[n_cast + n_tail:n_cast + n_tail + 3]
    cast_out = refs[n_cast + n_tail + 3:2 * n_cast + n_tail + 3]
    h_scr, cos_scr, s1_scr, s2_scr = refs[2 * n_cast + n_tail + 3:]
    for src, dst in zip(cast_in, cast_out):
        dst[...] = src[...].astype(dst.dtype)
    for p, src in enumerate(tail_in):
        tail_ref[p * src.shape[0]:(p + 1) * src.shape[0], :] = src[...]

    hb = _rms(x_ref[...], nw_ref[...]).astype(BF16)
    h_scr[...] = hb
    dt_lane = lax.broadcasted_iota(jnp.int32, dt_ref.shape, 1)
    dt = jax.nn.softplus(_dot_nt(hb, wt_ref[row_dt:row_dt + LANES, :]) + dtb_ref[...])
    dt_ref[...] = jnp.where(dt_lane < n_dt, dt, 0.0)
    half = ROPE_DIM // 2
    ang = invf_ref[...] * pos_ref[...].astype(F32)
    terms = [t.astype(F32) for t in _split3(jnp.cos(ang)) + _split3(jnp.sin(ang))]
    fill = jnp.zeros((LANES - len(terms) * half, ang.shape[1]), F32)
    grid_t = jnp.concatenate(terms + [fill], axis=0).T.astype(BF16)
    tables = _dot(grid_t, rope_e_ref[...])
    cos_scr[...] = tables[:, 0:LANES] + rope_c_ref[...]
    s1_scr[...] = tables[:, LANES:2 * LANES]
    s2_scr[...] = tables[:, 2 * LANES:3 * LANES]

    for c0 in range(0, out_ref.shape[1], sub):
        r0 = c0 if c0 < col_q else c0 - col_q + row_q
        acc = _dot_nt(h_scr[...], wt_ref[r0:r0 + sub, :])
        if col_q <= c0 < col_v:
            scale = Q_SCALE if c0 < col_k else 1.0
            for c in range(c0, c0 + sub, LANES):
                a = acc[:, c - c0:c - c0 + LANES]
                r = (a * cos_scr[...] + pltpu.roll(a, half, 1) * s1_scr[...]
                     + pltpu.roll(a, LANES - half, 1) * s2_scr[...])
                out_ref[:, c:c + LANES] = (r * scale).astype(out_ref.dtype)
        else:
            out_ref[:, c0:c0 + sub] = acc.astype(out_ref.dtype)


def _in_proj(x2d, positions, norm_w, w_t, dt_bias, inv_freq, casts, *, n_out, row_dt, n_dt, row_q,
             row_tail, col_q, col_k, col_v, tm=512, sub=512):
    t, d = x2d.shape
    steps = t // tm
    assert all(c % sub == 0 for c in (col_q, col_k, col_v, n_out))
    tail_rows = (w_t.shape[0] - row_tail) // steps
    tail_blk = math.gcd(row_tail, tail_rows)
    n_tail = tail_rows // tail_blk
    assert tail_rows * steps == w_t.shape[0] - row_tail and tail_blk % 16 == 0
    tail_specs = [pl.BlockSpec((tail_blk, d), lambda i, p=p: (row_tail // tail_blk + n_tail * i + p, 0))
                  for p in range(n_tail)]

    def cast_spec(w):
        hold = next(h for h in (1, 2, 4, 8) if (w.shape[0] * h) % (16 * steps) == 0)
        return pl.BlockSpec((w.shape[0] * hold // steps, w.shape[1]), lambda i: (i // hold, 0))

    cast_specs = [cast_spec(w) for w in casts]
    half = ROPE_DIM // 2
    lane = np.arange(LANES) % ATT_HEAD_DIM
    freq_row = np.arange(LANES) % half
    is_cos = np.arange(LANES) < 3 * half
    is_sin = (np.arange(LANES) >= 3 * half) & (np.arange(LANES) < 6 * half)
    same_f = freq_row[:, None] == (lane % half)[None, :]
    rope_e = np.concatenate([
        (is_cos[:, None] & same_f & (lane < ROPE_DIM)[None, :]) * 1.0,
        (is_sin[:, None] & same_f & ((lane >= half) & (lane < ROPE_DIM))[None, :]) * 1.0,
        (is_sin[:, None] & same_f & (lane < half)[None, :]) * -1.0], axis=1)
    rope_c = (lane >= ROPE_DIM)[None, :] * 1.0
    kern = functools.partial(_inproj_kernel, n_cast=len(casts), n_tail=n_tail, row_dt=row_dt, n_dt=n_dt,
                             row_q=row_q, col_q=col_q, col_k=col_k, col_v=col_v, sub=sub)
    return pl.pallas_call(
        kern,
        grid=(t // tm,),
        in_specs=[
            pl.BlockSpec((tm, d), lambda i: (i, 0)),
            pl.BlockSpec((None, 1, tm), lambda i: (i, 0, 0)),
            _const_spec((1, d)),
            _const_spec(w_t.shape),
            _const_spec((1, LANES)),
            _const_spec((half, 1)),
            _const_spec(rope_e.shape),
            _const_spec(rope_c.shape),
        ] + cast_specs + tail_specs,
        out_specs=[
            pl.BlockSpec((tm, n_out), lambda i: (i, 0)),
            pl.BlockSpec((tm, LANES), lambda i: (i, 0)),
            pl.BlockSpec((tail_rows, d), lambda i: (i, 0)),
        ] + cast_specs,
        out_shape=[
            jax.ShapeDtypeStruct((t, n_out), BF16),
            jax.ShapeDtypeStruct((t, LANES), F32),
            jax.ShapeDtypeStruct((w_t.shape[0] - row_tail, d), BF16),
        ] + [jax.ShapeDtypeStruct(w.shape, BF16) for w in casts],
        scratch_shapes=[
            pltpu.VMEM((tm, d), BF16),
            pltpu.VMEM((tm, LANES), F32),
            pltpu.VMEM((tm, LANES), F32),
            pltpu.VMEM((tm, LANES), F32),
        ],
        compiler_params=pltpu.CompilerParams(
            dimension_semantics=("arbitrary",),
            vmem_limit_bytes=_vmem_limit(
                pipelined=(_nbytes((tm, d), F32) + _nbytes((tm, n_out), BF16) + _nbytes((tm, LANES), F32)
                           + _nbytes((tail_rows, d), BF16)
                           + sum(_nbytes(s.block_shape, F32) + _nbytes(s.block_shape, BF16) for s in cast_specs)),
                resident=_nbytes(w_t.shape, BF16) + _nbytes((tm, d), BF16) + 3 * _nbytes((tm, LANES), F32),
                live=3 * _nbytes((tm, sub), F32))),
        name="in_proj",
    )(x2d, positions.reshape(t // tm, 1, tm), norm_w, w_t, dt_bias, inv_freq[:, None],
      jnp.asarray(rope_e, BF16), jnp.asarray(rope_c, F32), *casts, *([w_t] * n_tail))


def _ssd_kernel(xs_ref, z_ref, bc_ref, dt_ref, cwx_ref, cbx_ref, cwbc_ref, cbbc_ref,
                alog_ref, dskip_ref, nw_ref, e_ref, tri_ref, shift_ref,
                y_ref, xprev, bcprev, hstate, *, n_groups, d_state, head_dim, chunks):
    L = SSM_CHUNK
    d_inner = xs_ref.shape[1]
    gw = d_inner // n_groups
    heads_per_group = gw // head_dim
    pairs_per_group = heads_per_group // 2

    @pl.when(pl.program_id(1) == 0)
    def _():
        xprev[...] = jnp.zeros(xprev.shape, BF16)
        bcprev[...] = jnp.zeros(bcprev.shape, BF16)
        hstate[...] = jnp.zeros(hstate.shape, F32)

    a_neg = -jnp.exp(alog_ref[...]) * math.log2(math.e)
    row = lax.broadcasted_iota(jnp.int32, (L, L), 0)
    col = lax.broadcasted_iota(jnp.int32, (L, L), 1)
    causal = col <= row
    lane = lax.broadcasted_iota(jnp.int32, (L, LANES), 1)

    def conv_silu(cur_ref, prev, w_ref, b_ref, r0):
        body = cur_ref[r0:r0 + L - CONV_CARRY, :]
        last = cur_ref[r0 + L - CONV_CARRY:r0 + L, :]
        old = prev[...] if r0 == 0 else cur_ref[r0 - CONV_CARRY:r0, :]
        row = lax.broadcasted_iota(jnp.int32, last.shape, 0)
        taps = []
        for k in range(SSM_CONV):
            back = SSM_CONV - 1 - k
            wk = w_ref[k * CONV_CARRY:(k + 1) * CONV_CARRY, :]
            taps.append(body * jnp.concatenate([wk] * (L // CONV_CARRY - 1), axis=0))
            taps.append(jnp.where(row >= CONV_CARRY - back, old, last) * wk)
        acc = b_ref[...] + _dot(shift_ref[...], jnp.concatenate(taps, axis=0))
        return _silu(acc)

    h = [hstate[g] for g in range(n_groups)]
    for ci in range(chunks):
        r0 = ci * L
        xs = conv_silu(xs_ref, xprev, cwx_ref, cbx_ref, r0)
        bc = conv_silu(bc_ref, bcprev, cwbc_ref, cbbc_ref, r0).astype(BF16)

        dt = dt_ref[r0:r0 + L, :]
        a_cs = _dot(tri_ref[...], jnp.concatenate(_split3(dt * a_neg), axis=0))
        a_cs_t = a_cs.T
        ea = jnp.exp2(a_cs)
        dte = jnp.exp2(a_cs[L - 1:L, :] - a_cs)
        stacked = jnp.concatenate([ea, dt * dte], axis=0).astype(BF16)
        expanded = _dot(stacked, e_ref[...])
        ea_e = expanded[0:L]
        xs_b = xs.astype(BF16)
        xw_b = (xs * expanded[L:2 * L]).astype(BF16)
        dt_t = dt.T
        gate = _silu(z_ref[r0:r0 + L, :].astype(F32))

        for g in range(n_groups):
            b_g = bc[:, g * d_state:(g + 1) * d_state]
            c_g = bc[:, (n_groups + g) * d_state:(n_groups + g + 1) * d_state]
            cb = _dot_nt(c_g, b_g)
            y_diag = []
            for p2 in range(pairs_per_group):
                pair = g * pairs_per_group + p2
                ms = []
                for hh in (2 * pair, 2 * pair + 1):
                    seg = a_cs[:, hh:hh + 1] - a_cs_t[hh:hh + 1, :]
                    dec = jnp.exp2(jnp.where(causal, seg, -jnp.inf))
                    ms.append((cb * dec * dt_t[hh:hh + 1, :]).astype(BF16))
                lhs = jnp.concatenate(ms, axis=1)
                xp = xs_b[:, pair * LANES:(pair + 1) * LANES]
                zero = jnp.zeros_like(xp)
                rhs = jnp.concatenate([jnp.where(lane < head_dim, xp, zero),
                                       jnp.where(lane >= head_dim, xp, zero)], axis=0)
                y_diag.append(_dot(lhs, rhs))
            sl = slice(g * gw, (g + 1) * gw)
            ea_g = ea_e[:, sl]
            y_off = _dot(c_g, h[g].astype(BF16)) * ea_g
            h[g] = h[g] * ea_g[L - 1:L, :] + _dot(b_g.T, xw_b[:, sl])
            yg = (jnp.concatenate(y_diag, axis=1) + y_off + xs[:, sl] * dskip_ref[:, sl]) * gate[:, sl]
            y_ref[r0:r0 + L, sl] = _rms(yg, nw_ref[:, sl]).astype(y_ref.dtype)

    for g in range(n_groups):
        hstate[g] = h[g]
    xprev[...] = xs_ref[chunks * L - CONV_CARRY:chunks * L, :]
    bcprev[...] = bc_ref[chunks * L - CONV_CARRY:chunks * L, :]


def _ssd(proj, dt, cw_x, cb_x, cw_bc, cb_bc, a_log, d_skip_e, norm_w, e_mat, tri, shift,
         *, batch, seq, d_inner, col_xs, col_z, col_bc, chunks=2):
    L = SSM_CHUNK
    rows = chunks * L
    nc = seq // rows
    t = batch * seq
    bc_w = 2 * SSM_N_GROUPS * SSM_D_STATE
    row = lambda b, c: b * nc + c
    kern = functools.partial(_ssd_kernel, n_groups=SSM_N_GROUPS, d_state=SSM_D_STATE,
                             head_dim=SSM_HEAD_DIM, chunks=chunks)
    return pl.pallas_call(
        kern,
        grid=(batch, nc),
        in_specs=[
            pl.BlockSpec((rows, d_inner), lambda b, c: (row(b, c), col_xs // d_inner)),
            pl.BlockSpec((rows, d_inner), lambda b, c: (row(b, c), col_z // d_inner)),
            pl.BlockSpec((rows, bc_w), lambda b, c: (row(b, c), col_bc // bc_w)),
            pl.BlockSpec((rows, LANES), lambda b, c: (row(b, c), 0)),
            pl.BlockSpec((SSM_CONV * CONV_CARRY, d_inner), lambda b, c: (0, 0)),
            pl.BlockSpec((1, d_inner), lambda b, c: (0, 0)),
            pl.BlockSpec((SSM_CONV * CONV_CARRY, bc_w), lambda b, c: (0, 0)),
            pl.BlockSpec((1, bc_w), lambda b, c: (0, 0)),
            pl.BlockSpec((1, LANES), lambda b, c: (0, 0)),
            pl.BlockSpec((1, d_inner), lambda b, c: (0, 0)),
            pl.BlockSpec((1, d_inner), lambda b, c: (0, 0)),
            pl.BlockSpec((LANES, d_inner), lambda b, c: (0, 0)),
            pl.BlockSpec((L, 3 * L), lambda b, c: (0, 0)),
            pl.BlockSpec(shift.shape, lambda b, c: (0, 0)),
        ],
        out_specs=pl.BlockSpec((rows, d_inner), lambda b, c: (row(b, c), 0)),
        out_shape=jax.ShapeDtypeStruct((t, d_inner), BF16),
        scratch_shapes=[
            pltpu.VMEM((CONV_CARRY, d_inner), BF16),
            pltpu.VMEM((CONV_CARRY, bc_w), BF16),
            pltpu.VMEM((SSM_N_GROUPS, SSM_D_STATE, d_inner // SSM_N_GROUPS), F32),
        ],
        compiler_params=pltpu.CompilerParams(
            dimension_semantics=("parallel", "arbitrary"),
            vmem_limit_bytes=_vmem_limit(
                pipelined=(3 * _nbytes((rows, d_inner), BF16) + _nbytes((rows, bc_w), BF16)
                           + _nbytes((rows, LANES), F32) + _nbytes(cw_x.shape, BF16) + _nbytes(cw_bc.shape, BF16)
                           + _nbytes(e_mat.shape, BF16) + _nbytes(tri.shape, BF16) + _nbytes(shift.shape, BF16)),
                resident=_nbytes((SSM_D_STATE, d_inner), F32),
                live=chunks * (_nbytes((2 * L, d_inner), F32) + _nbytes((L, d_inner + bc_w), F32)
                               + 2 * _nbytes((L, d_inner), F32)))),
        name="ssd",
    )(proj, proj, proj, dt, cw_x, cb_x, cw_bc, cb_bc, a_log, d_skip_e, norm_w, e_mat, tri, shift)


def _attn_kernel(q_ref, k_ref, v_ref, lq1_ref, lk1_ref, lq2_ref, lk2_ref, subw_ref,
                 o_ref, vt_scr, *, tq, tk, lam_init):
    seq = q_ref.shape[0]
    for jb in range(seq // tk):
        vt_scr[jb] = v_ref[jb * tk:(jb + 1) * tk, :].T

    lam = (jnp.exp(jnp.sum(lq1_ref[...] * lk1_ref[...]))
           - jnp.exp(jnp.sum(lq2_ref[...] * lk2_ref[...])) + lam_init)
    subw = subw_ref[...] * (1.0 - lam_init)

    stacked_q = {}

    def scores(qi, j):
        if qi not in stacked_q:
            q = q_ref[qi * tq:(qi + 1) * tq, :]
            lane = lax.broadcasted_iota(jnp.int32, q.shape, 1)
            zero = jnp.zeros_like(q)
            stacked_q[qi] = jnp.concatenate([jnp.where(lane < ATT_HEAD_DIM, q, zero),
                                             jnp.where(lane >= ATT_HEAD_DIM, q, zero)], axis=0)
        return _dot_nt(k_ref[j * tk:(j + 1) * tk, :], stacked_q[qi])

    steps = [(qi, j) for qi in range(seq // tq) for j in range((qi + 1) * tq // tk)]
    s_next = scores(*steps[0])
    m = l = acc = None
    for t, (qi, j) in enumerate(steps):
        s = s_next
        if t + 1 < len(steps):
            s_next = scores(*steps[t + 1])
        if (j + 1) * tk > qi * tq:
            kv = lax.broadcasted_iota(jnp.int32, s.shape, 0) + (j * tk - qi * tq)
            r = lax.broadcasted_iota(jnp.int32, s.shape, 1)
            r = jnp.where(r >= tq, r - tq, r)
            s = jnp.where(kv <= r, s, -jnp.inf)
        s_max = jnp.max(s, axis=0, keepdims=True)
        if j == 0:
            m = s_max
            p = jnp.exp2(s - m)
            l = jnp.sum(p, axis=0, keepdims=True)
            acc = _dot(vt_scr[j], p.astype(BF16))
        else:
            m_new = jnp.maximum(m, s_max)
            alpha = jnp.exp2(m - m_new)
            p = jnp.exp2(s - m_new)
            l = alpha * l + jnp.sum(p, axis=0, keepdims=True)
            acc = alpha * acc + _dot(vt_scr[j], p.astype(BF16))
            m = m_new
        if (j + 1) * tk == (qi + 1) * tq:
            o = acc / l
            od = o[:, 0:tq] - lam * o[:, tq:2 * tq]
            od = od * lax.rsqrt(jnp.mean(od * od, axis=0, keepdims=True) + RMS_EPS)
            o_ref[qi * tq:(qi + 1) * tq, :] = (od * subw).T.astype(o_ref.dtype)


def _attention(proj, lq1, lk1, lq2, lk2, subw, *, batch, seq, n_heads, col_q, col_k, col_v,
               lam_init, tq=512, tk=512):
    t = batch * seq
    hd = ATT_V_DIM
    kern = functools.partial(_attn_kernel, tq=tq, tk=tk, lam_init=lam_init)
    small = pl.BlockSpec((1, ATT_HEAD_DIM), lambda b, h: (0, 0))
    return pl.pallas_call(
        kern,
        grid=(batch, n_heads),
        in_specs=[
            pl.BlockSpec((seq, hd), lambda b, h: (b, col_q // hd + h)),
            pl.BlockSpec((seq, hd), lambda b, h: (b, col_k // hd + h)),
            pl.BlockSpec((seq, hd), lambda b, h: (b, col_v // hd + h)),
            small, small, small, small,
            pl.BlockSpec((hd, 1), lambda b, h: (0, 0)),
        ],
        out_specs=pl.BlockSpec((seq, hd), lambda b, h: (b, h)),
        out_shape=jax.ShapeDtypeStruct((t, n_heads * hd), BF16),
        scratch_shapes=[pltpu.VMEM((seq // tk, hd, tk), BF16)],
        compiler_params=pltpu.CompilerParams(
            dimension_semantics=("parallel", "parallel"),
            vmem_limit_bytes=_vmem_limit(
                pipelined=4 * _nbytes((seq, hd), BF16),
                resident=_nbytes((seq, hd), BF16),
                live=(4 * _nbytes((tk, 2 * tq), F32) + _nbytes((tk, 2 * tq), BF16)
                      + 2 * _nbytes((hd, 2 * tq), F32)))),
        name="diff_attn",
    )(proj, proj, proj, lq1, lk1, lq2, lk2, subw)


def _mix_ffn_kernel(x_ref, ys_ref, ya_ref, npm_ref, nqm_ref, npf_ref, nqf_ref,
                    wg_ref, wso_ref, wao_ref, wmix_ref, wfg_ref, wfu_ref, wfd_ref, o_ref):
    x = x_ref[...]
    d = x.shape[1]
    h = _rms(x, npm_ref[...]).astype(BF16)
    gates = jax.nn.sigmoid(_dot_nt(h, wg_ref[...]))
    y_ssm = _dot(ys_ref[...], wso_ref[...])
    y_att = _dot(ya_ref[...], wao_ref[...])
    blend = gates[:, 0:d] * y_ssm + gates[:, d:2 * d] * y_att
    mixed = _dot(blend.astype(BF16), wmix_ref[...])
    x = x + _rms(mixed, nqm_ref[...])
    h = _rms(x, npf_ref[...]).astype(BF16)
    hid = wfg_ref.shape[1]
    cut = -(-hid // (2 * MXU_WIDTH)) * MXU_WIDTH
    f = 0.0
    for c0, c1 in ((0, cut), (cut, hid)):
        act = (_silu(_dot(h, wfg_ref[:, c0:c1])) * _dot(h, wfu_ref[:, c0:c1])).astype(BF16)
        f = f + _dot(act, wfd_ref[c0:c1, :])
    o_ref[...] = x + _rms(f, nqf_ref[...])


def _mix_ffn(x2d, y_ssm, y_att, n_pre_mix, n_post_mix, n_pre_ffn, n_post_ffn,
             w_gate, w_so, w_ao, w_mix, w_fg, w_fu, w_fd, *, tm=512):
    t, d = x2d.shape
    rows = lambda w: pl.BlockSpec((tm, w), lambda i: (i, 0))
    norms = (n_pre_mix, n_post_mix, n_pre_ffn, n_post_ffn)
    weights = (w_gate, w_so, w_ao, w_mix, w_fg, w_fu, w_fd)
    return pl.pallas_call(
        _mix_ffn_kernel,
        grid=(t // tm,),
        in_specs=[rows(d), rows(y_ssm.shape[1]), rows(y_att.shape[1])]
                 + [_const_spec(a.shape) for a in norms + weights],
        out_specs=rows(d),
        out_shape=jax.ShapeDtypeStruct((t, d), F32),
        compiler_params=pltpu.CompilerParams(
            dimension_semantics=("parallel",),
            vmem_limit_bytes=_vmem_limit(
                pipelined=(2 * _nbytes((tm, d), F32) + _nbytes((tm, y_ssm.shape[1]), BF16)
                           + _nbytes((tm, y_att.shape[1]), BF16)),
                resident=sum(_nbytes(w.shape, BF16) for w in weights),
                live=_nbytes((tm, d), F32) + max(
                    _nbytes((tm, 2 * d), F32) + 3 * _nbytes((tm, d), F32),
                    _nbytes((tm, d), F32) + _nbytes((tm, w_fg.shape[1]), F32)
                    + _nbytes((tm, w_fg.shape[1]), BF16)))),
        name="mix_ffn",
    )(x2d, y_ssm, y_att, *norms, *weights)


def kernel(x, positions, w_in, conv_w, conv_b, dt_bias, a_log, d_skip, ssm_norm_w, w_ssm_out, lam_q1, lam_k1, lam_q2, lam_k2, attn_subln_w, w_attn_out, w_mix_out, norm_pre_mix, norm_post_mix, norm_pre_ffn, norm_post_ffn, w_ffn_gate, w_ffn_up, w_ffn_down):
    batch, seq, d_model = x.shape
    depth = w_in.shape[0]
    d_inner = w_ssm_out.shape[1]
    n_ssm_heads = dt_bias.shape[1]
    bc_w = 2 * SSM_N_GROUPS * SSM_D_STATE
    att_w = w_attn_out.shape[1]
    n_att_heads = att_w // ATT_V_DIM
    t = batch * seq

    z_end = d_inner
    xbc_end = z_end + d_inner + bc_w
    dt_end = xbc_end + n_ssm_heads
    q_end = dt_end + att_w
    k_end = q_end + att_w
    v_end = k_end + att_w

    col_z, col_xs = 0, d_inner
    col_bc = 2 * d_inner
    col_q = col_bc + bc_w
    col_k = col_q + att_w
    col_v = col_k + att_w

    inv_freq = ROPE_THETA ** (-jnp.arange(0, ROPE_DIM, 2, dtype=F32) / ROPE_DIM)
    head_of_col = np.arange(d_inner) // SSM_HEAD_DIM
    assert n_ssm_heads <= LANES
    e_mat = jnp.asarray(np.arange(LANES)[:, None] == head_of_col[None, :], BF16)
    tri = np.arange(SSM_CHUNK)[None, :] <= np.arange(SSM_CHUNK)[:, None]
    tri = jnp.asarray(np.concatenate([tri, tri, tri], axis=1), BF16)
    L = SSM_CHUNK
    t_out = np.arange(L)[:, None, None]
    tap = np.arange(SSM_CONV)[None, :, None]
    src = t_out - (SSM_CONV - 1) + tap
    src_col = tap * L + np.where(src >= 0, src, L + src)
    shift = jnp.asarray((np.arange(SSM_CONV * L)[None, None, :] == src_col).any(axis=1), BF16)

    def pad_heads(v):
        return jnp.pad(v.astype(F32), (0, LANES - n_ssm_heads))[None, :]

    x2d = x.reshape(t, d_model)
    for l in range(depth):
        lam_init = 0.8 - 0.6 * math.exp(-0.3 * l)
        wl = w_in[l]
        w_t = wl.T.astype(BF16)
        later = [w_ssm_out[l], w_attn_out[l], w_mix_out[l], w_ffn_gate[l], w_ffn_up[l], w_ffn_down[l]]
        proj, dt, w_gate_t, w_so, w_ao, w_mix, w_fg, w_fu, w_fd = _in_proj(
            x2d, positions, norm_pre_mix[l][None, :], w_t, pad_heads(dt_bias[l]), inv_freq, later,
            n_out=col_v + att_w, row_dt=xbc_end, n_dt=n_ssm_heads, row_q=dt_end, row_tail=v_end,
            col_q=col_q, col_k=col_k, col_v=col_v)

        cw = jnp.repeat(conv_w[l].astype(BF16), CONV_CARRY, axis=0)
        y_ssm = _ssd(proj, dt,
                     cw[:, :d_inner], conv_b[l][None, :d_inner],
                     cw[:, d_inner:], conv_b[l][None, d_inner:],
                     pad_heads(a_log[l]), jnp.repeat(d_skip[l], SSM_HEAD_DIM)[None, :],
                     ssm_norm_w[l][None, :], e_mat, tri, shift,
                     batch=batch, seq=seq, d_inner=d_inner, col_xs=col_xs, col_z=col_z, col_bc=col_bc)

        y_att = _attention(proj, lam_q1[l][None, :], lam_k1[l][None, :], lam_q2[l][None, :],
                           lam_k2[l][None, :], attn_subln_w[l][:, None],
                           batch=batch, seq=seq, n_heads=n_att_heads,
                           col_q=col_q, col_k=col_k, col_v=col_v, lam_init=lam_init)

        x2d = _mix_ffn(x2d, y_ssm, y_att, norm_pre_mix[l][None, :], norm_post_mix[l][None, :],
                       norm_pre_ffn[l][None, :], norm_post_ffn[l][None, :],
                       w_gate_t, w_so, w_ao, w_mix, w_fg, w_fu, w_fd)
    return x2d.reshape(batch, seq, d_model)
```

```python
import functools
import math

import jax
import jax.numpy as jnp
import numpy as np
from jax import lax
from jax.experimental import pallas as pl
from jax.experimental.pallas import tpu as pltpu

F32 = jnp.float32
BF16 = jnp.bfloat16

LANES = 128
MXU_WIDTH = 256

SSM_HEAD_DIM = 64
SSM_N_GROUPS = 4
SSM_D_STATE = 128
SSM_CONV = 4
SSM_CHUNK = 128
CONV_CARRY = 16
ATT_HEAD_DIM = 64
ATT_V_DIM = 2 * ATT_HEAD_DIM
ROPE_THETA = 500000.0
ROPE_DIM = ATT_HEAD_DIM // 4
RMS_EPS = 1e-6
Q_SCALE = ATT_HEAD_DIM ** -0.5 * math.log2(math.e)

MIB = 1024 * 1024
V7X_VMEM_BYTES = 64 * MIB
VMEM_RESERVE = 8 * MIB


def _nbytes(shape, dtype):
    return math.prod(shape) * jnp.dtype(dtype).itemsize


def _vmem_limit(pipelined, resident, live):
    budget = V7X_VMEM_BYTES - VMEM_RESERVE
    need = 2 * pipelined + resident + live
    assert need <= budget, need
    return budget


def _rms(x, w):
    return x * lax.rsqrt(jnp.mean(x * x, axis=-1, keepdims=True) + RMS_EPS) * w


def _silu(x):
    hx = 0.5 * x
    return hx + hx * jnp.tanh(hx)


def _dot(a, b):
    return jnp.dot(a, b, preferred_element_type=F32)


def _dot_nt(a, b):
    return lax.dot_general(a, b, (((1,), (1,)), ((), ())), preferred_element_type=F32)


def _split3(x):
    hi = x.astype(BF16)
    r = x - hi.astype(F32)
    mid = r.astype(BF16)
    lo = (r - mid.astype(F32)).astype(BF16)
    return hi, mid, lo


def _const_spec(shape):
    nd = len(shape)
    return pl.BlockSpec(shape, lambda *_: (0,) * nd, pipeline_mode=pl.Buffered(1))


def _inproj_kernel(x_ref, pos_ref, nw_ref, wt_ref, dtb_ref, invf_ref, rope_e_ref, rope_c_ref,
                   *refs, n_cast, n_tail, row_dt, n_dt, row_q, col_q, col_k, col_v, sub):
    cast_in = refs[:n_cast]
    tail_in = refs[n_cast:n_cast + n_tail]
    out_ref, dt_ref, tail_ref = refs[n_cast + n_tail:n_cast + n_tail + 3]
    cast_out = refs[n_cast + n_tail + 3:2 * n_cast + n_tail + 3]
    h_scr, cos_scr, s1_scr, s2_scr = refs[2 * n_cast + n_tail + 3:]
    for src, dst in zip(cast_in, cast_out):
        dst[...] = src[...].astype(dst.dtype)
    for p, src in enumerate(tail_in):
        tail_ref[p * src.shape[0]:(p + 1) * src.shape[0], :] = src[...]

    hb = _rms(x_ref[...], nw_ref[...]).astype(BF16)
    h_scr[...] = hb
    dt_lane = lax.broadcasted_iota(jnp.int32, dt_ref.shape, 1)
    dt = jax.nn.softplus(_dot_nt(hb, wt_ref[row_dt:row_dt + LANES, :]) + dtb_ref[...])
    dt_ref[...] = jnp.where(dt_lane < n_dt, dt, 0.0)
    half = ROPE_DIM // 2
    ang = invf_ref[...] * pos_ref[...].astype(F32)
    terms = [t.astype(F32) for t in _split3(jnp.cos(ang)) + _split3(jnp.sin(ang))]
    fill = jnp.zeros((LANES - len(terms) * half, ang.shape[1]), F32)
    grid_t = jnp.concatenate(terms + [fill], axis=0).T.astype(BF16)
    tables = _dot(grid_t, rope_e_ref[...])
    cos_scr[...] = tables[:, 0:LANES] + rope_c_ref[...]
    s1_scr[...] = tables[:, LANES:2 * LANES]
    s2_scr[...] = tables[:, 2 * LANES:3 * LANES]

    for c0 in range(0, out_ref.shape[1], sub):
        r0 = c0 if c0 < col_q else c0 - col_q + row_q
        acc = _dot_nt(h_scr[...], wt_ref[r0:r0 + sub, :])
        if col_q <= c0 < col_v:
            scale = Q_SCALE if c0 < col_k else 1.0
            for c in range(c0, c0 + sub, LANES):
                a = acc[:, c - c0:c - c0 + LANES]
                r = (a * cos_scr[...] + pltpu.roll(a, half, 1) * s1_scr[...]
                     + pltpu.roll(a, LANES - half, 1) * s2_scr[...])
                out_ref[:, c:c + LANES] = (r * scale).astype(out_ref.dtype)
        else:
            out_ref[:, c0:c0 + sub] = acc.astype(out_ref.dtype)


def _in_proj(x2d, positions, norm_w, w_t, dt_bias, inv_freq, casts, *, n_out, row_dt, n_dt, row_q,
             row_tail, col_q, col_k, col_v, tm=512, sub=512):
    t, d = x2d.shape
    steps = t // tm
    assert all(c % sub == 0 for c in (col_q, col_k, col_v, n_out))
    tail_rows = (w_t.shape[0] - row_tail) // steps
    tail_blk = math.gcd(row_tail, tail_rows)
    n_tail = tail_rows // tail_blk
    assert tail_rows * steps == w_t.shape[0] - row_tail and tail_blk % 16 == 0
    tail_specs = [pl.BlockSpec((tail_blk, d), lambda i, p=p: (row_tail // tail_blk + n_tail * i + p, 0))
                  for p in range(n_tail)]

    def cast_spec(w):
        hold = next(h for h in (1, 2, 4, 8) if (w.shape[0] * h) % (16 * steps) == 0)
        return pl.BlockSpec((w.shape[0] * hold // steps, w.shape[1]), lambda i: (i // hold, 0))

    cast_specs = [cast_spec(w) for w in casts]
    half = ROPE_DIM // 2
    lane = np.arange(LANES) % ATT_HEAD_DIM
    freq_row = np.arange(LANES) % half
    is_cos = np.arange(LANES) < 3 * half
    is_sin = (np.arange(LANES) >= 3 * half) & (np.arange(LANES) < 6 * half)
    same_f = freq_row[:, None] == (lane % half)[None, :]
    rope_e = np.concatenate([
        (is_cos[:, None] & same_f & (lane < ROPE_DIM)[None, :]) * 1.0,
        (is_sin[:, None] & same_f & ((lane >= half) & (lane < ROPE_DIM))[None, :]) * 1.0,
        (is_sin[:, None] & same_f & (lane < half)[None, :]) * -1.0], axis=1)
    rope_c = (lane >= ROPE_DIM)[None, :] * 1.0
    kern = functools.partial(_inproj_kernel, n_cast=len(casts), n_tail=n_tail, row_dt=row_dt, n_dt=n_dt,
                             row_q=row_q, col_q=col_q, col_k=col_k, col_v=col_v, sub=sub)
    return pl.pallas_call(
        kern,
        grid=(t // tm,),
        in_specs=[
            pl.BlockSpec((tm, d), lambda i: (i, 0)),
            pl.BlockSpec((None, 1, tm), lambda i: (i, 0, 0)),
            _const_spec((1, d)),
            _const_spec(w_t.shape),
            _const_spec((1, LANES)),
            _const_spec((half, 1)),
            _const_spec(rope_e.shape),
            _const_spec(rope_c.shape),
        ] + cast_specs + tail_specs,
        out_specs=[
            pl.BlockSpec((tm, n_out), lambda i: (i, 0)),
            pl.BlockSpec((tm, LANES), lambda i: (i, 0)),
            pl.BlockSpec((tail_rows, d), lambda i: (i, 0)),
        ] + cast_specs,
        out_shape=[
            jax.ShapeDtypeStruct((t, n_out), BF16),
            jax.ShapeDtypeStruct((t, LANES), F32),
            jax.ShapeDtypeStruct((w_t.shape[0] - row_tail, d), BF16),
        ] + [jax.ShapeDtypeStruct(w.shape, BF16) for w in casts],
        scratch_shapes=[
            pltpu.VMEM((tm, d), BF16),
            pltpu.VMEM((tm, LANES), F32),
            pltpu.VMEM((tm, LANES), F32),
            pltpu.VMEM((tm, LANES), F32),
        ],
        compiler_params=pltpu.CompilerParams(
            dimension_semantics=("arbitrary",),
            vmem_limit_bytes=_vmem_limit(
                pipelined=(_nbytes((tm, d), F32) + _nbytes((tm, n_out), BF16) + _nbytes((tm, LANES), F32)
                           + _nbytes((tail_rows, d), BF16)
                           + sum(_nbytes(s.block_shape, F32) + _nbytes(s.block_shape, BF16) for s in cast_specs)),
                resident=_nbytes(w_t.shape, BF16) + _nbytes((tm, d), BF16) + 3 * _nbytes((tm, LANES), F32),
                live=3 * _nbytes((tm, sub), F32))),
        name="in_proj",
    )(x2d, positions.reshape(t // tm, 1, tm), norm_w, w_t, dt_bias, inv_freq[:, None],
      jnp.asarray(rope_e, BF16), jnp.asarray(rope_c, F32), *casts, *([w_t] * n_tail))


def _ssd_kernel(xs_ref, z_ref, bc_ref, dt_ref, cwx_ref, cbx_ref, cwbc_ref, cbbc_ref,
                alog_ref, dskip_ref, nw_ref, e_ref, tri_ref, shift_ref,
                y_ref, xprev, bcprev, hstate, *, n_groups, d_state, head_dim, chunks):
    L = SSM_CHUNK
    d_inner = xs_ref.shape[1]
    gw = d_inner // n_groups
    heads_per_group = gw // head_dim
    pairs_per_group = heads_per_group // 2

    @pl.when(pl.program_id(1) == 0)
    def _():
        xprev[...] = jnp.zeros(xprev.shape, BF16)
        bcprev[...] = jnp.zeros(bcprev.shape, BF16)
        hstate[...] = jnp.zeros(hstate.shape, F32)

    a_neg = -jnp.exp(alog_ref[...]) * math.log2(math.e)
    row = lax.broadcasted_iota(jnp.int32, (L, L), 0)
    col = lax.broadcasted_iota(jnp.int32, (L, L), 1)
    causal = col <= row
    lane = lax.broadcasted_iota(jnp.int32, (L, LANES), 1)

    def conv_silu(cur_ref, prev, w_ref, b_ref, r0):
        body = cur_ref[r0:r0 + L - CONV_CARRY, :]
        last = cur_ref[r0 + L - CONV_CARRY:r0 + L, :]
        old = prev[...] if r0 == 0 else cur_ref[r0 - CONV_CARRY:r0, :]
        row = lax.broadcasted_iota(jnp.int32, last.shape, 0)
        taps = []
        for k in range(SSM_CONV):
            back = SSM_CONV - 1 - k
            wk = w_ref[k * CONV_CARRY:(k + 1) * CONV_CARRY, :]
            taps.append(body * jnp.concatenate([wk] * (L // CONV_CARRY - 1), axis=0))
            taps.append(jnp.where(row >= CONV_CARRY - back, old, last) * wk)
        acc = b_ref[...] + _dot(shift_ref[...], jnp.concatenate(taps, axis=0))
        return _silu(acc)

    h = [hstate[g] for g in range(n_groups)]
    for ci in range(chunks):
        r0 = ci * L
        xs = conv_silu(xs_ref, xprev, cwx_ref, cbx_ref, r0)
        bc = conv_silu(bc_ref, bcprev, cwbc_ref, cbbc_ref, r0).astype(BF16)

        dt = dt_ref[r0:r0 + L, :]
        a_cs = _dot(tri_ref[...], jnp.concatenate(_split3(dt * a_neg), axis=0))
        a_cs_t = a_cs.T
        ea = jnp.exp2(a_cs)
        dte = jnp.exp2(a_cs[L - 1:L, :] - a_cs)
        stacked = jnp.concatenate([ea, dt * dte], axis=0).astype(BF16)
        expanded = _dot(stacked, e_ref[...])
        ea_e = expanded[0:L]
        xs_b = xs.astype(BF16)
        xw_b = (xs * expanded[L:2 * L]).astype(BF16)
        dt_t = dt.T
        gate = _silu(z_ref[r0:r0 + L, :].astype(F32))

        for g in range(n_groups):
            b_g = bc[:, g * d_state:(g + 1) * d_state]
            c_g = bc[:, (n_groups + g) * d_state:(n_groups + g + 1) * d_state]
            cb = _dot_nt(c_g, b_g)
            y_diag = []
            for p2 in range(pairs_per_group):
                pair = g * pairs_per_group + p2
                ms = []
                for hh in (2 * pair, 2 * pair + 1):
                    seg = a_cs[:, hh:hh + 1] - a_cs_t[hh:hh + 1, :]
                    dec = jnp.exp2(jnp.where(causal, seg, -jnp.inf))
                    ms.append((cb * dec * dt_t[hh:hh + 1, :]).astype(BF16))
                lhs = jnp.concatenate(ms, axis=1)
                xp = xs_b[:, pair * LANES:(pair + 1) * LANES]
                zero = jnp.zeros_like(xp)
                rhs = jnp.concatenate([jnp.where(lane < head_dim, xp, zero),
                                       jnp.where(lane >= head_dim, xp, zero)], axis=0)
                y_diag.append(_dot(lhs, rhs))
            sl = slice(g * gw, (g + 1) * gw)
            ea_g = ea_e[:, sl]
            y_off = _dot(c_g, h[g].astype(BF16)) * ea_g
            h[g] = h[g] * ea_g[L - 1:L, :] + _dot(b_g.T, xw_b[:, sl])
            yg = (jnp.concatenate(y_diag, axis=1) + y_off + xs[:, sl] * dskip_ref[:, sl]) * gate[:, sl]
            y_ref[r0:r0 + L, sl] = _rms(yg, nw_ref[:, sl]).astype(y_ref.dtype)

    for g in range(n_groups):
        hstate[g] = h[g]
    xprev[...] = xs_ref[chunks * L - CONV_CARRY:chunks * L, :]
    bcprev[...] = bc_ref[chunks * L - CONV_CARRY:chunks * L, :]


def _ssd(proj, dt, cw_x, cb_x, cw_bc, cb_bc, a_log, d_skip_e, norm_w, e_mat, tri, shift,
         *, batch, seq, d_inner, col_xs, col_z, col_bc, chunks=2):
    L = SSM_CHUNK
    rows = chunks * L
    nc = seq // rows
    t = batch * seq
    bc_w = 2 * SSM_N_GROUPS * SSM_D_STATE
    row = lambda b, c: b * nc + c
    kern = functools.partial(_ssd_kernel, n_groups=SSM_N_GROUPS, d_state=SSM_D_STATE,
                             head_dim=SSM_HEAD_DIM, chunks=chunks)
    return pl.pallas_call(
        kern,
        grid=(batch, nc),
        in_specs=[
            pl.BlockSpec((rows, d_inner), lambda b, c: (row(b, c), col_xs // d_inner)),
            pl.BlockSpec((rows, d_inner), lambda b, c: (row(b, c), col_z // d_inner)),
            pl.BlockSpec((rows, bc_w), lambda b, c: (row(b, c), col_bc // bc_w)),
            pl.BlockSpec((rows, LANES), lambda b, c: (row(b, c), 0)),
            pl.BlockSpec((SSM_CONV * CONV_CARRY, d_inner), lambda b, c: (0, 0)),
            pl.BlockSpec((1, d_inner), lambda b, c: (0, 0)),
            pl.BlockSpec((SSM_CONV * CONV_CARRY, bc_w), lambda b, c: (0, 0)),
            pl.BlockSpec((1, bc_w), lambda b, c: (0, 0)),
            pl.BlockSpec((1, LANES), lambda b, c: (0, 0)),
            pl.BlockSpec((1, d_inner), lambda b, c: (0, 0)),
            pl.BlockSpec((1, d_inner), lambda b, c: (0, 0)),
            pl.BlockSpec((LANES, d_inner), lambda b, c: (0, 0)),
            pl.BlockSpec((L, 3 * L), lambda b, c: (0, 0)),
            pl.BlockSpec(shift.shape, lambda b, c: (0, 0)),
        ],
        out_specs=pl.BlockSpec((rows, d_inner), lambda b, c: (row(b, c), 0)),
        out_shape=jax.ShapeDtypeStruct((t, d_inner), BF16),
        scratch_shapes=[
            pltpu.VMEM((CONV_CARRY, d_inner), BF16),
            pltpu.VMEM((CONV_CARRY, bc_w), BF16),
            pltpu.VMEM((SSM_N_GROUPS, SSM_D_STATE, d_inner // SSM_N_GROUPS), F32),
        ],
        compiler_params=pltpu.CompilerParams(
            dimension_semantics=("parallel", "arbitrary"),
            vmem_limit_bytes=_vmem_limit(
                pipelined=(3 * _nbytes((rows, d_inner), BF16) + _nbytes((rows, bc_w), BF16)
                           + _nbytes((rows, LANES), F32) + _nbytes(cw_x.shape, BF16) + _nbytes(cw_bc.shape, BF16)
                           + _nbytes(e_mat.shape, BF16) + _nbytes(tri.shape, BF16) + _nbytes(shift.shape, BF16)),
                resident=_nbytes((SSM_D_STATE, d_inner), F32),
                live=chunks * (_nbytes((2 * L, d_inner), F32) + _nbytes((L, d_inner + bc_w), F32)
                               + 2 * _nbytes((L, d_inner), F32)))),
        name="ssd",
    )(proj, proj, proj, dt, cw_x, cb_x, cw_bc, cb_bc, a_log, d_skip_e, norm_w, e_mat, tri, shift)


def _attn_kernel(q_ref, k_ref, v_ref, lq1_ref, lk1_ref, lq2_ref, lk2_ref, subw_ref,
                 o_ref, vt_scr, *, tq, tk, lam_init):
    seq = q_ref.shape[0]
    for jb in range(seq // tk):
        vt_scr[jb] = v_ref[jb * tk:(jb + 1) * tk, :].T

    lam = (jnp.exp(jnp.sum(lq1_ref[...] * lk1_ref[...]))
           - jnp.exp(jnp.sum(lq2_ref[...] * lk2_ref[...])) + lam_init)
    subw = subw_ref[...] * (1.0 - lam_init)

    stacked_q = {}

    def scores(qi, j):
        if qi not in stacked_q:
            q = q_ref[qi * tq:(qi + 1) * tq, :]
            lane = lax.broadcasted_iota(jnp.int32, q.shape, 1)
            zero = jnp.zeros_like(q)
            stacked_q[qi] = jnp.concatenate([jnp.where(lane < ATT_HEAD_DIM, q, zero),
                                             jnp.where(lane >= ATT_HEAD_DIM, q, zero)], axis=0)
        return _dot_nt(k_ref[j * tk:(j + 1) * tk, :], stacked_q[qi])

    steps = [(qi, j) for qi in range(seq // tq) for j in range((qi + 1) * tq // tk)]
    s_next = scores(*steps[0])
    m = l = acc = None
    for t, (qi, j) in enumerate(steps):
        s = s_next
        if t + 1 < len(steps):
            s_next = scores(*steps[t + 1])
        if (j + 1) * tk > qi * tq:
            kv = lax.broadcasted_iota(jnp.int32, s.shape, 0) + (j * tk - qi * tq)
            r = lax.broadcasted_iota(jnp.int32, s.shape, 1)
            r = jnp.where(r >= tq, r - tq, r)
            s = jnp.where(kv <= r, s, -jnp.inf)
        s_max = jnp.max(s, axis=0, keepdims=True)
        if j == 0:
            m = s_max
            p = jnp.exp2(s - m)
            l = jnp.sum(p, axis=0, keepdims=True)
            acc = _dot(vt_scr[j], p.astype(BF16))
        else:
            m_new = jnp.maximum(m, s_max)
            alpha = jnp.exp2(m - m_new)
            p = jnp.exp2(s - m_new)
            l = alpha * l + jnp.sum(p, axis=0, keepdims=True)
            acc = alpha * acc + _dot(vt_scr[j], p.astype(BF16))
            m = m_new
        if (j + 1) * tk == (qi + 1) * tq:
            o = acc / l
            od = o[:, 0:tq] - lam * o[:, tq:2 * tq]
            od = od * lax.rsqrt(jnp.mean(od * od, axis=0, keepdims=True) + RMS_EPS)
            o_ref[qi * tq:(qi + 1) * tq, :] = (od * subw).T.astype(o_ref.dtype)


def _attention(proj, lq1, lk1, lq2, lk2, subw, *, batch, seq, n_heads, col_q, col_k, col_v,
               lam_init, tq=512, tk=512):
    t = batch * seq
    hd = ATT_V_DIM
    kern = functools.partial(_attn_kernel, tq=tq, tk=tk, lam_init=lam_init)
    small = pl.BlockSpec((1, ATT_HEAD_DIM), lambda b, h: (0, 0))
    return pl.pallas_call(
        kern,
        grid=(batch, n_heads),
        in_specs=[
            pl.BlockSpec((seq, hd), lambda b, h: (b, col_q // hd + h)),
            pl.BlockSpec((seq, hd), lambda b, h: (b, col_k // hd + h)),
            pl.BlockSpec((seq, hd), lambda b, h: (b, col_v // hd + h)),
            small, small, small, small,
            pl.BlockSpec((hd, 1), lambda b, h: (0, 0)),
        ],
        out_specs=pl.BlockSpec((seq, hd), lambda b, h: (b, h)),
        out_shape=jax.ShapeDtypeStruct((t, n_heads * hd), BF16),
        scratch_shapes=[pltpu.VMEM((seq // tk, hd, tk), BF16)],
        compiler_params=pltpu.CompilerParams(
            dimension_semantics=("parallel", "parallel"),
            vmem_limit_bytes=_vmem_limit(
                pipelined=4 * _nbytes((seq, hd), BF16),
                resident=_nbytes((seq, hd), BF16),
                live=(4 * _nbytes((tk, 2 * tq), F32) + _nbytes((tk, 2 * tq), BF16)
                      + 2 * _nbytes((hd, 2 * tq), F32)))),
        name="diff_attn",
    )(proj, proj, proj, lq1, lk1, lq2, lk2, subw)


def _mix_ffn_kernel(x_ref, ys_ref, ya_ref, npm_ref, nqm_ref, npf_ref, nqf_ref,
                    wg_ref, wso_ref, wao_ref, wmix_ref, wfg_hbm, wfu_hbm, wfd_hbm, o_ref,
                    wfg_ref, wfu_ref, wfd_ref, sem):
    first = pl.program_id(0) == 0
    ffn_copies = [pltpu.make_async_copy(src, dst, sem.at[k]) for k, (src, dst) in enumerate(
        ((wfg_hbm, wfg_ref), (wfu_hbm, wfu_ref), (wfd_hbm, wfd_ref)))]

    @pl.when(first)
    def _():
        for c in ffn_copies:
            c.start()

    x = x_ref[...]
    d = x.shape[1]
    h = _rms(x, npm_ref[...]).astype(BF16)
    gates = jax.nn.sigmoid(_dot_nt(h, wg_ref[...]))
    y_ssm = _dot(ys_ref[...], wso_ref[...])
    y_att = _dot(ya_ref[...], wao_ref[...])
    blend = gates[:, 0:d] * y_ssm + gates[:, d:2 * d] * y_att
    mixed = _dot(blend.astype(BF16), wmix_ref[...])
    x = x + _rms(mixed, nqm_ref[...])

    @pl.when(first)
    def _():
        for c in ffn_copies:
            c.wait()

    h = _rms(x, npf_ref[...]).astype(BF16)
    hid = wfg_ref.shape[1]
    cut = -(-hid // (2 * MXU_WIDTH)) * MXU_WIDTH
    f = 0.0
    for c0, c1 in ((0, cut), (cut, hid)):
        act = (_silu(_dot(h, wfg_ref[:, c0:c1])) * _dot(h, wfu_ref[:, c0:c1])).astype(BF16)
        f = f + _dot(act, wfd_ref[c0:c1, :])
    o_ref[...] = x + _rms(f, nqf_ref[...])


def _mix_ffn(x2d, y_ssm, y_att, n_pre_mix, n_post_mix, n_pre_ffn, n_post_ffn,
             w_gate, w_so, w_ao, w_mix, w_fg, w_fu, w_fd, *, tm=512):
    t, d = x2d.shape
    rows = lambda w: pl.BlockSpec((tm, w), lambda i: (i, 0))
    norms = (n_pre_mix, n_post_mix, n_pre_ffn, n_post_ffn)
    mix_w, ffn_w = (w_gate, w_so, w_ao, w_mix), (w_fg, w_fu, w_fd)
    weights = mix_w + ffn_w
    return pl.pallas_call(
        _mix_ffn_kernel,
        grid=(t // tm,),
        in_specs=[rows(d), rows(y_ssm.shape[1]), rows(y_att.shape[1])]
                 + [_const_spec(a.shape) for a in norms + mix_w]
                 + [pl.BlockSpec(memory_space=pl.ANY) for _ in ffn_w],
        out_specs=rows(d),
        out_shape=jax.ShapeDtypeStruct((t, d), F32),
        scratch_shapes=[pltpu.VMEM(w.shape, BF16) for w in ffn_w]
                       + [pltpu.SemaphoreType.DMA((len(ffn_w),))],
        compiler_params=pltpu.CompilerParams(
            dimension_semantics=("arbitrary",),
            vmem_limit_bytes=_vmem_limit(
                pipelined=(2 * _nbytes((tm, d), F32) + _nbytes((tm, y_ssm.shape[1]), BF16)
                           + _nbytes((tm, y_att.shape[1]), BF16)),
                resident=sum(_nbytes(w.shape, BF16) for w in weights),
                live=_nbytes((tm, d), F32) + max(
                    _nbytes((tm, 2 * d), F32) + 3 * _nbytes((tm, d), F32),
                    _nbytes((tm, d), F32) + _nbytes((tm, w_fg.shape[1]), F32)
                    + _nbytes((tm, w_fg.shape[1]), BF16)))),
        name="mix_ffn",
    )(x2d, y_ssm, y_att, *norms, *weights)


def kernel(x, positions, w_in, conv_w, conv_b, dt_bias, a_log, d_skip, ssm_norm_w, w_ssm_out, lam_q1, lam_k1, lam_q2, lam_k2, attn_subln_w, w_attn_out, w_mix_out, norm_pre_mix, norm_post_mix, norm_pre_ffn, norm_post_ffn, w_ffn_gate, w_ffn_up, w_ffn_down):
    batch, seq, d_model = x.shape
    depth = w_in.shape[0]
    d_inner = w_ssm_out.shape[1]
    n_ssm_heads = dt_bias.shape[1]
    bc_w = 2 * SSM_N_GROUPS * SSM_D_STATE
    att_w = w_attn_out.shape[1]
    n_att_heads = att_w // ATT_V_DIM
    t = batch * seq

    z_end = d_inner
    xbc_end = z_end + d_inner + bc_w
    dt_end = xbc_end + n_ssm_heads
    q_end = dt_end + att_w
    k_end = q_end + att_w
    v_end = k_end + att_w

    col_z, col_xs = 0, d_inner
    col_bc = 2 * d_inner
    col_q = col_bc + bc_w
    col_k = col_q + att_w
    col_v = col_k + att_w

    inv_freq = ROPE_THETA ** (-jnp.arange(0, ROPE_DIM, 2, dtype=F32) / ROPE_DIM)
    head_of_col = np.arange(d_inner) // SSM_HEAD_DIM
    assert n_ssm_heads <= LANES
    e_mat = jnp.asarray(np.arange(LANES)[:, None] == head_of_col[None, :], BF16)
    tri = np.arange(SSM_CHUNK)[None, :] <= np.arange(SSM_CHUNK)[:, None]
    tri = jnp.asarray(np.concatenate([tri, tri, tri], axis=1), BF16)
    L = SSM_CHUNK
    t_out = np.arange(L)[:, None, None]
    tap = np.arange(SSM_CONV)[None, :, None]
    src = t_out - (SSM_CONV - 1) + tap
    src_col = tap * L + np.where(src >= 0, src, L + src)
    shift = jnp.asarray((np.arange(SSM_CONV * L)[None, None, :] == src_col).any(axis=1), BF16)

    def pad_heads(v):
        return jnp.pad(v.astype(F32), (0, LANES - n_ssm_heads))[None, :]

    x2d = x.reshape(t, d_model)
    for l in range(depth):
        lam_init = 0.8 - 0.6 * math.exp(-0.3 * l)
        wl = w_in[l]
        w_t = wl.T.astype(BF16)
        later = [w_ssm_out[l], w_attn_out[l], w_mix_out[l], w_ffn_gate[l], w_ffn_up[l], w_ffn_down[l]]
        proj, dt, w_gate_t, w_so, w_ao, w_mix, w_fg, w_fu, w_fd = _in_proj(
            x2d, positions, norm_pre_mix[l][None, :], w_t, pad_heads(dt_bias[l]), inv_freq, later,
            n_out=col_v + att_w, row_dt=xbc_end, n_dt=n_ssm_heads, row_q=dt_end, row_tail=v_end,
            col_q=col_q, col_k=col_k, col_v=col_v)

        cw = jnp.repeat(conv_w[l].astype(BF16), CONV_CARRY, axis=0)
        y_ssm = _ssd(proj, dt,
                     cw[:, :d_inner], conv_b[l][None, :d_inner],
                     cw[:, d_inner:], conv_b[l][None, d_inner:],
                     pad_heads(a_log[l]), jnp.repeat(d_skip[l], SSM_HEAD_DIM)[None, :],
                     ssm_norm_w[l][None, :], e_mat, tri, shift,
                     batch=batch, seq=seq, d_inner=d_inner, col_xs=col_xs, col_z=col_z, col_bc=col_bc)

        y_att = _attention(proj, lam_q1[l][None, :], lam_k1[l][None, :], lam_q2[l][None, :],
                           lam_k2[l][None, :], attn_subln_w[l][:, None],
                           batch=batch, seq=seq, n_heads=n_att_heads,
                           col_q=col_q, col_k=col_k, col_v=col_v, lam_init=lam_init)

        x2d = _mix_ffn(x2d, y_ssm, y_att, norm_pre_mix[l][None, :], norm_post_mix[l][None, :],
                       norm_pre_ffn[l][None, :], norm_post_ffn[l][None, :],
                       w_gate_t, w_so, w_ao, w_mix, w_fg, w_fu, w_fd)
    return x2d.reshape(batch, seq, d_model)
```
